```python
import math
import jax, jax.numpy as jnp
from jax import lax
import numpy as np

D_MODEL = 1024
BATCH = 8
SEQ = 2048
DEPTH = 1
DEC_BATCH = 128
DEC_SEQ = 1
PAST_LEN = 16384
PAGE_SIZE = 128

D_MIX = 2 * D_MODEL
D_SSD = D_MIX // 2
SSD_HEAD_DIM = 64
SSD_HEADS = D_SSD // SSD_HEAD_DIM
SSD_STATE = 128
SSD_GROUPS = 2
CONV_K = 4
SSD_CONV_DIM = D_SSD + 2 * SSD_GROUPS * SSD_STATE
D_MLSTM = D_MIX - D_SSD
MLSTM_HEADS = 4
MLSTM_DK = D_MLSTM // MLSTM_HEADS
MLSTM_DV = D_MLSTM // MLSTM_HEADS
CHUNK = 128
PEER_HEADS = 8
PEER_NKEYS = 128
PEER_EXPERTS = PEER_NKEYS * PEER_NKEYS
PEER_DKEY = 256
PEER_TOPK = 16
PEER_BLOCK = 256
ALPHA = (2.0 * DEPTH) ** 0.25
BETA = (8.0 * DEPTH) ** -0.25
LN_EPS = 1e-5
IN_SPLITS = (D_SSD, SSD_CONV_DIM, SSD_HEADS, D_MLSTM, D_MLSTM, D_MLSTM, MLSTM_HEADS, MLSTM_HEADS)
IN_COLS = sum(IN_SPLITS)

kernel_name = 'hybrid_ssd_mlstm_peer_step'


def _layer_norm(x, g, b):
    xf = x.astype(jnp.float32)
    mu = xf.mean(-1, keepdims=True)
    var = jnp.square(xf - mu).mean(-1, keepdims=True)
    return ((xf - mu) * lax.rsqrt(var + LN_EPS) * g.astype(jnp.float32) + b.astype(jnp.float32)).astype(x.dtype)


def _rms_norm(x, g):
    xf = x.astype(jnp.float32)
    return (xf * lax.rsqrt(jnp.mean(xf * xf, -1, keepdims=True) + LN_EPS) * g.astype(jnp.float32)).astype(x.dtype)


def _chunk_len(L):
    return CHUNK if L % CHUNK == 0 else L


def _to_chunks(a, T):
    b, L = a.shape[:2]
    return a.reshape((b, L // T, T) + a.shape[2:]).swapaxes(0, 1)


def _from_chunks(a):
    nc, b, T = a.shape[:3]
    return a.swapaxes(0, 1).reshape((b, nc * T) + a.shape[3:])


def _causal_conv(x, buf, w, b):
    L = x.shape[1]
    xp = jnp.concatenate([buf.astype(x.dtype), x], axis=1)
    y = b + xp[:, 0:L] * w[0]
    for j in range(1, CONV_K):
        y = y + xp[:, j:j + L] * w[j]
    return y, xp[:, L:]


def _ssd_scan(xs, dt, a, bm, cm, h0):
    T = _chunk_len(xs.shape[1])
    mask = jnp.tril(jnp.ones((T, T), bool))[None, :, :, None, None]

    def step(h, inp):
        xc, dtc, bc, cc = inp
        acs = jnp.cumsum(dtc * a, axis=1)
        lmat = jnp.exp(jnp.where(mask, acs[:, :, None] - acs[:, None], -jnp.inf))
        cb = jnp.einsum('btgn,bsgn->btsg', cc, bc)
        mm = cb[..., None] * lmat * dtc[:, None]
        y = jnp.einsum('btsgh,bsghp->btghp', mm, xc)
        y = y + jnp.einsum('btgn,bghpn->btghp', cc, h) * jnp.exp(acs)[..., None]
        decay = jnp.exp(acs[:, -1:] - acs) * dtc
        h_new = h * jnp.exp(acs[:, -1])[..., None, None] + jnp.einsum('bsgn,bsgh,bsghp->bghpn', bc, decay, xc)
        return h_new, y

    h, ys = lax.scan(step, h0, tuple(_to_chunks(t, T) for t in (xs, dt, bm, cm)))
    return _from_chunks(ys), h


def _mlstm_scan(q, k, v, logi, logf, c0, n0, m0):
    T = _chunk_len(q.shape[1])
    mask = jnp.tril(jnp.ones((T, T), bool))[None, :, :, None]

    def step(carry, inp):
        c, n, m = carry
        qc, kc, vc, ic, fc = inp
        bcs = jnp.cumsum(fc, axis=1)
        dmat = jnp.where(mask, bcs[:, :, None] - bcs[:, None] + ic[:, None], -jnp.inf)
        inter = bcs + m[:, None]
        m_t = jnp.maximum(inter, dmat.max(axis=2))
        w_inter = jnp.exp(inter - m_t)
        att = jnp.einsum('bthd,bshd->btsh', qc, kc) * jnp.exp(dmat - m_t[:, :, None])
        num = jnp.einsum('btsh,bshv->bthv', att, vc) + w_inter[..., None] * jnp.einsum('bthd,bhdv->bthv', qc, c)
        den = att.sum(axis=2) + w_inter * jnp.einsum('bthd,bhd->bth', qc, n)
        h = num / jnp.maximum(jnp.abs(den), jnp.exp(-m_t))[..., None]
        b_end = bcs[:, -1]
        g = b_end[:, None] - bcs + ic
        m_new = jnp.maximum(b_end + m, g.max(axis=1))
        w_old = jnp.exp(b_end + m - m_new)
        w_s = jnp.exp(g - m_new[:, None])
        c_new = w_old[..., None, None] * c + jnp.einsum('bsh,bshd,bshv->bhdv', w_s, kc, vc)
        n_new = w_old[..., None] * n + jnp.einsum('bsh,bshd->bhd', w_s, kc)
        return (c_new, n_new, m_new), h

    (c, n, m), hs = lax.scan(step, (c0, n0, m0), tuple(_to_chunks(t, T) for t in (q, k, v, logi, logf)))
    return _from_chunks(hs), c, n, m


def _peer(x, wq, keys, u, v):
    bsz, seq, d = x.shape
    n = bsz * seq
    blk = min(PEER_BLOCK, n)
    n_pad = -(-n // blk) * blk
    xt = jnp.pad(x.reshape(n, d), ((0, n_pad - n), (0, 0)))

    def block(xb):
        q = jnp.einsum('td,dc->tc', xb, wq).reshape(blk, PEER_HEADS, 2, PEER_DKEY // 2).astype(jnp.float32)
        s = jnp.einsum('thcd,hckd->thck', q, keys.astype(jnp.float32))
        s1, i1 = lax.top_k(s[:, :, 0], PEER_TOPK)
        s2, i2 = lax.top_k(s[:, :, 1], PEER_TOPK)
        cand = (s1[..., :, None] + s2[..., None, :]).reshape(blk, PEER_HEADS, PEER_TOPK * PEER_TOPK)
        cidx = (i1[..., :, None] * PEER_NKEYS + i2[..., None, :]).reshape(blk, PEER_HEADS, PEER_TOPK * PEER_TOPK)
        top, pos = lax.top_k(cand, PEER_TOPK)
        eidx = jnp.take_along_axis(cidx, pos, axis=-1)
        gate = jax.nn.softmax(top, axis=-1)
        hid = jax.nn.gelu(jnp.einsum('td,thkd->thk', xb, jnp.take(u, eidx, axis=0)).astype(jnp.float32), approximate=False)
        return jnp.einsum('thk,thkd->td', (gate * hid).astype(xb.dtype), jnp.take(v, eidx, axis=0))

    out = lax.map(block, xt.reshape(n_pad // blk, blk, d))
    return out.reshape(n_pad, d)[:n].reshape(bsz, seq, d)


def _hybrid_layer(x, st_ssd, st_ssd_conv, st_c, st_n, st_m, st_mconv,
                  w_in, conv_ssd_w, conv_ssd_b, dt_bias, a_log, d_skip, ssd_norm_g,
                  conv_m_w, conv_m_b, w_mq, w_mk, b_igate, b_fgate, mlstm_norm_g,
                  w_out, ln1_g, ln1_b, peer_wq, peer_keys, peer_u, peer_v, ln2_g, ln2_b):
    f32 = jnp.float32
    bsz, seq, _ = x.shape
    proj = jnp.einsum('bld,dc->blc', x, w_in)
    offs = [int(o) for o in np.cumsum(IN_SPLITS)[:-1]]
    z_s, xbc, dt_pre, u_m, v_m, o_m, i_pre, f_pre = jnp.split(proj, offs, axis=-1)

    xbc, new_ssd_conv = _causal_conv(xbc, st_ssd_conv, conv_ssd_w, conv_ssd_b)
    xbc = jax.nn.silu(xbc)
    xs, bm, cm = jnp.split(xbc, [D_SSD, D_SSD + SSD_GROUPS * SSD_STATE], axis=-1)
    hg = SSD_HEADS // SSD_GROUPS
    xs5 = xs.reshape(bsz, seq, SSD_GROUPS, hg, SSD_HEAD_DIM).astype(f32)
    bm4 = bm.reshape(bsz, seq, SSD_GROUPS, SSD_STATE).astype(f32)
    cm4 = cm.reshape(bsz, seq, SSD_GROUPS, SSD_STATE).astype(f32)
    dt = jax.nn.softplus(dt_pre.astype(f32) + dt_bias.astype(f32)).reshape(bsz, seq, SSD_GROUPS, hg)
    a = -jnp.exp(a_log.astype(f32)).reshape(SSD_GROUPS, hg)
    h0 = st_ssd.astype(f32).reshape(bsz, SSD_GROUPS, hg, SSD_HEAD_DIM, SSD_STATE)
    y_s, h_new = _ssd_scan(xs5, dt, a, bm4, cm4, h0)
    y_s = y_s + d_skip.astype(f32).reshape(SSD_GROUPS, hg)[:, :, None] * xs5
    y_s = y_s.reshape(bsz, seq, D_SSD).astype(x.dtype)
    y_s = _rms_norm(y_s * jax.nn.silu(z_s), ssd_norm_g)
    new_ssd = h_new.reshape(bsz, SSD_HEADS, SSD_HEAD_DIM, SSD_STATE).astype(st_ssd.dtype)

    u_c, new_mconv = _causal_conv(u_m, st_mconv, conv_m_w, conv_m_b)
    u_c = jax.nn.silu(u_c).reshape(bsz, seq, MLSTM_HEADS, MLSTM_DK).astype(f32)
    q = jnp.einsum('blhd,hde->blhe', u_c, w_mq.astype(f32))
    k = jnp.einsum('blhd,hde->blhe', u_c, w_mk.astype(f32)) * (MLSTM_DK ** -0.5)
    v = v_m.reshape(bsz, seq, MLSTM_HEADS, MLSTM_DV).astype(f32)
    logi = i_pre.astype(f32) + b_igate.astype(f32)
    logf = jax.nn.log_sigmoid(f_pre.astype(f32) + b_fgate.astype(f32))
    h_m, c_new, n_new, m_new = _mlstm_scan(q, k, v, logi, logf, st_c.astype(f32), st_n.astype(f32), st_m.astype(f32))
    mu = h_m.mean(-1, keepdims=True)
    var = jnp.square(h_m - mu).mean(-1, keepdims=True)
    h_m = (h_m - mu) * lax.rsqrt(var + LN_EPS) * mlstm_norm_g.astype(f32)
    h_m = jax.nn.sigmoid(o_m.astype(f32)).reshape(bsz, seq, MLSTM_HEADS, MLSTM_DV) * h_m
    h_m = h_m.reshape(bsz, seq, D_MLSTM).astype(x.dtype)

    mix = jnp.einsum('blc,cd->bld', jnp.concatenate([y_s, h_m], axis=-1), w_out)
    x1 = _layer_norm(ALPHA * x + mix, ln1_g, ln1_b)
    x2 = _layer_norm(ALPHA * x1 + _peer(x1, peer_wq, peer_keys, peer_u, peer_v), ln2_g, ln2_b)
    new_states = (new_ssd, new_ssd_conv.astype(st_ssd_conv.dtype), c_new.astype(st_c.dtype),
                  n_new.astype(st_n.dtype), m_new.astype(st_m.dtype), new_mconv.astype(st_mconv.dtype))
    return x2, new_states


def setup_inputs(seed: int = 0) -> dict:
    key = jax.random.key(seed)
    ks = jax.random.split(key, 32)

    def nrm(k, shape, s):
        return jax.random.normal(k, shape, jnp.float32) * s

    L = DEPTH
    dt0 = jnp.exp(jax.random.uniform(ks[11], (L, SSD_HEADS), jnp.float32, math.log(1e-3), math.log(1e-1)))
    return {
        'x_prompt': nrm(ks[0], (BATCH, SEQ, D_MODEL), 1.0),
        'x_sample': nrm(ks[1], (DEC_BATCH, DEC_SEQ, D_MODEL), 1.0),
        'state_ssd': nrm(ks[2], (L, DEC_BATCH, SSD_HEADS, SSD_HEAD_DIM, SSD_STATE), 0.5),
        'state_ssd_conv': nrm(ks[3], (L, DEC_BATCH, CONV_K - 1, SSD_CONV_DIM), 1.0),
        'state_mlstm_c': nrm(ks[4], (L, DEC_BATCH, MLSTM_HEADS, MLSTM_DK, MLSTM_DV), 0.1),
        'state_mlstm_n': nrm(ks[5], (L, DEC_BATCH, MLSTM_HEADS, MLSTM_DK), 0.1),
        'state_mlstm_m': nrm(ks[6], (L, DEC_BATCH, MLSTM_HEADS), 1.0),
        'state_mlstm_conv': nrm(ks[7], (L, DEC_BATCH, CONV_K - 1, D_MLSTM), 1.0),
        'w_in': nrm(ks[8], (L, D_MODEL, IN_COLS), D_MODEL ** -0.5),
        'conv_ssd_w': nrm(ks[9], (L, CONV_K, SSD_CONV_DIM), CONV_K ** -0.5),
        'conv_ssd_b': nrm(ks[10], (L, SSD_CONV_DIM), 0.02),
        'dt_bias': dt0 + jnp.log(-jnp.expm1(-dt0)),
        'a_log': jnp.log(jax.random.uniform(ks[12], (L, SSD_HEADS), jnp.float32, 1.0, 16.0)),
        'd_skip': 1.0 + nrm(ks[13], (L, SSD_HEADS), 0.02),
        'ssd_norm_g': 1.0 + nrm(ks[14], (L, D_SSD), 0.02),
        'conv_m_w': nrm(ks[15], (L, CONV_K, D_MLSTM), CONV_K ** -0.5),
        'conv_m_b': nrm(ks[16], (L, D_MLSTM), 0.02),
        'w_mq': nrm(ks[17], (L, MLSTM_HEADS, MLSTM_DK, MLSTM_DK), MLSTM_DK ** -0.5),
        'w_mk': nrm(ks[18], (L, MLSTM_HEADS, MLSTM_DK, MLSTM_DK), MLSTM_DK ** -0.5),
        'b_igate': nrm(ks[19], (L, MLSTM_HEADS), 0.1),
        'b_fgate': jnp.linspace(3.0, 6.0, MLSTM_HEADS, dtype=jnp.float32)[None] + nrm(ks[20], (L, MLSTM_HEADS), 0.02),
        'mlstm_norm_g': 1.0 + nrm(ks[21], (L, MLSTM_HEADS, MLSTM_DV), 0.02),
        'w_out': nrm(ks[22], (L, D_MIX, D_MODEL), BETA * D_MIX ** -0.5),
        'ln1_g': 1.0 + nrm(ks[23], (L, D_MODEL), 0.02),
        'ln1_b': nrm(ks[24], (L, D_MODEL), 0.02),
        'peer_wq': nrm(ks[25], (L, D_MODEL, PEER_HEADS * PEER_DKEY), D_MODEL ** -0.5),
        'peer_keys': nrm(ks[26], (L, PEER_HEADS, 2, PEER_NKEYS, PEER_DKEY // 2), (PEER_DKEY // 2) ** -0.5),
        'peer_u': nrm(ks[27], (L, PEER_EXPERTS, D_MODEL), D_MODEL ** -0.5),
        'peer_v': nrm(ks[28], (L, PEER_EXPERTS, D_MODEL), BETA),
        'ln2_g': 1.0 + nrm(ks[29], (L, D_MODEL), 0.02),
        'ln2_b': nrm(ks[30], (L, D_MODEL), 0.02),
    }


def reference(x_prompt, x_sample, state_ssd, state_ssd_conv, state_mlstm_c, state_mlstm_n,
              state_mlstm_m, state_mlstm_conv, w_in, conv_ssd_w, conv_ssd_b, dt_bias, a_log,
              d_skip, ssd_norm_g, conv_m_w, conv_m_b, w_mq, w_mk, b_igate, b_fgate, mlstm_norm_g,
              w_out, ln1_g, ln1_b, peer_wq, peer_keys, peer_u, peer_v, ln2_g, ln2_b):
    states = (state_ssd, state_ssd_conv, state_mlstm_c, state_mlstm_n, state_mlstm_m, state_mlstm_conv)
    bp = x_prompt.shape[0]
    yp, ys = x_prompt, x_sample
    p_list, s_list = [], []
    for l in range(DEPTH):
        weights = (w_in[l], conv_ssd_w[l], conv_ssd_b[l], dt_bias[l], a_log[l], d_skip[l], ssd_norm_g[l],
                   conv_m_w[l], conv_m_b[l], w_mq[l], w_mk[l], b_igate[l], b_fgate[l], mlstm_norm_g[l],
                   w_out[l], ln1_g[l], ln1_b[l], peer_wq[l], peer_keys[l], peer_u[l], peer_v[l], ln2_g[l], ln2_b[l])
        zero_states = tuple(jnp.zeros((bp,) + s.shape[2:], s.dtype) for s in states)
        yp, p_new = _hybrid_layer(yp, *zero_states, *weights)
        ys, s_new = _hybrid_layer(ys, *(s[l] for s in states), *weights)
        p_list.append(p_new)
        s_list.append(s_new)
    p_ssd, p_ssd_conv, p_c, p_n, p_m, p_mconv = (jnp.stack([p[i] for p in p_list]) for i in range(6))
    s_ssd, s_ssd_conv, s_c, s_n, s_m, s_mconv = (jnp.stack([s[i] for s in s_list]) for i in range(6))
    return (yp, ys, p_ssd, p_ssd_conv, p_c, p_n, p_m, p_mconv, s_ssd, s_ssd_conv, s_c, s_n, s_m, s_mconv)
```

```python
import functools
import math

import jax
import jax.numpy as jnp
from jax import lax
from jax.experimental import pallas as pl
from jax.experimental.pallas import tpu as pltpu

F32 = jnp.float32
BF16 = jnp.bfloat16

LANES = 128
SUBLANES = 8
CHUNK = 128
CONV_K = 4
LN_EPS = 1e-5
VMEM_LIMIT = 52 * 1024 * 1024

SSD_HEAD_DIM = 64
SSD_STATE = 128
SSD_GROUPS = 2
PEER_TOPK = 16
GATE_I0 = 16


def _cparams(*sem):
    return pltpu.CompilerParams(dimension_semantics=sem, vmem_limit_bytes=VMEM_LIMIT)


def _dot(a, b):
    return jnp.dot(a, b, preferred_element_type=F32)


def _dot_nt(a, b):
    return lax.dot_general(a, b, (((1,), (1,)), ((), ())), preferred_element_type=F32)


def _split3(x):
    h = x.astype(BF16)
    r = x - h.astype(F32)
    m = r.astype(BF16)
    lo = (r - m.astype(F32)).astype(BF16)
    return h, m, lo


def _dot_sel_l(sel_bf16, x):
    h, m, lo = _split3(x)
    return _dot(sel_bf16, h) + _dot(sel_bf16, m) + _dot(sel_bf16, lo)


def _dot_sel_r(x, sel_bf16):
    h, m, lo = _split3(x)
    return _dot(h, sel_bf16) + _dot(m, sel_bf16) + _dot(lo, sel_bf16)


def _sigmoid(x):
    return 1.0 / (1.0 + jnp.exp(-x))


def _silu(x):
    return x * _sigmoid(x)


def _softplus(x):
    return jnp.maximum(x, 0.0) + jnp.log1p(jnp.exp(-jnp.abs(x)))


def _log_sigmoid(x):
    return -_softplus(-x)


def _gelu_exact(x):
    return 0.5 * x * (1.0 + lax.erf(x * (1.0 / math.sqrt(2.0))))


def _inproj_kernel(x_ref, w_ref, *out_refs):
    xb = x_ref[...].astype(BF16)
    off = 0
    for ref in out_refs:
        n = ref.shape[-1]
        ref[...] = _dot(xb, w_ref[:, off:off + n])
        off += n


def _inproj(x2, w_cat, widths, tm):
    m, d = x2.shape
    n_all = w_cat.shape[1]
    return pl.pallas_call(
        _inproj_kernel,
        grid=(m // tm,),
        in_specs=[pl.BlockSpec((tm, d), lambda i: (i, 0)),
                  pl.BlockSpec((d, n_all), lambda i: (0, 0))],
        out_specs=[pl.BlockSpec((tm, n), lambda i: (i, 0)) for n in widths],
        out_shape=[jax.ShapeDtypeStruct((m, n), F32) for n in widths],
        compiler_params=_cparams("arbitrary"),
        name="inproj",
    )(x2, w_cat)


def _ssd_chunk_kernel(xbc_ref, z_ref, g_ref, cw_ref, cb_ref, dtb_ref, alog_ref, dsk_ref, ng_ref,
                      tril_ref, exp_ref, y_ref, hT_out_ref, xpad, hT, ybuf, *, nc, d_ssd, n_heads):
    c = pl.program_id(1)
    cd = xpad.shape[1]
    hg = n_heads // SSD_GROUPS
    gw = d_ssd // SSD_GROUPS

    @pl.when(c == 0)
    def _():
        xpad[0:SUBLANES, :] = jnp.zeros((SUBLANES, cd), F32)
        hT[...] = jnp.zeros_like(hT)

    xpad[SUBLANES:SUBLANES + CHUNK, :] = xbc_ref[...]
    conv = cb_ref[...]
    for j in range(CONV_K):
        r0 = SUBLANES - (CONV_K - 1) + j
        conv = conv + cw_ref[j:j + 1, :] * xpad[r0:r0 + CHUNK, :]
    xpad[0:SUBLANES, :] = xpad[CHUNK:CHUNK + SUBLANES, :]
    xc = _silu(conv)
    xs = xc[:, 0:d_ssd]

    lane = lax.broadcasted_iota(jnp.int32, (1, LANES), 1)
    a_row = jnp.where(lane < n_heads, -jnp.exp(alog_ref[...]), 0.0)
    dt = _softplus(g_ref[:, 0:LANES] + dtb_ref[...])
    acs = _dot_sel_l(tril_ref[...], dt * a_row)
    dt_t = dt.T
    acs_t = acs.T
    eacs = jnp.exp(acs)
    dec = jnp.exp(acs[CHUNK - 1:CHUNK, :] - acs) * dt
    both = _dot_sel_r(jnp.concatenate([eacs, dec], axis=0), exp_ref[...])
    eacs_full = both[0:CHUNK]
    dec_full = both[CHUNK:2 * CHUNK]

    row = lax.broadcasted_iota(jnp.int32, (CHUNK, CHUNK), 0)
    col = lax.broadcasted_iota(jnp.int32, (CHUNK, CHUNK), 1)
    causal = row >= col

    for g in range(SSD_GROUPS):
        b_g = xc[:, d_ssd + g * SSD_STATE:d_ssd + (g + 1) * SSD_STATE]
        c_g = xc[:, d_ssd + (SSD_GROUPS + g) * SSD_STATE:d_ssd + (SSD_GROUPS + g + 1) * SSD_STATE]
        b_gb = b_g.astype(BF16)
        c_gb = c_g.astype(BF16)
        cb = _dot_nt(c_gb, b_gb)
        h_g = hT[g]
        y_inter = _dot(c_gb, h_g.astype(BF16)) * eacs_full[:, g * gw:(g + 1) * gw]
        for hl in range(hg):
            h = g * hg + hl
            diff = acs[:, h:h + 1] - acs_t[h:h + 1, :]
            lmat = jnp.exp(jnp.where(causal, diff, -jnp.inf))
            mm = cb * lmat * dt_t[h:h + 1, :]
            x_h = xs[:, h * SSD_HEAD_DIM:(h + 1) * SSD_HEAD_DIM]
            ybuf[:, h * SSD_HEAD_DIM:(h + 1) * SSD_HEAD_DIM] = _dot(mm.astype(BF16), x_h.astype(BF16))
        xd = xs[:, g * gw:(g + 1) * gw] * dec_full[:, g * gw:(g + 1) * gw]
        hT[g] = h_g * eacs_full[CHUNK - 1:CHUNK, g * gw:(g + 1) * gw] + _dot(b_g.T.astype(BF16), xd.astype(BF16))
        ybuf[:, g * gw:(g + 1) * gw] = ybuf[:, g * gw:(g + 1) * gw] + y_inter

    y = ybuf[...] + dsk_ref[...] * xs
    y = y * _silu(z_ref[...])
    y = y * lax.rsqrt(jnp.mean(y * y, axis=-1, keepdims=True) + LN_EPS) * ng_ref[...]
    y_ref[...] = y.astype(y_ref.dtype)

    @pl.when(c == nc - 1)
    def _():
        hT_out_ref[0] = hT[...]


def _ssd_chunk(xbc, z, gates, p, bsz, seq):
    nc = seq // CHUNK
    cd = xbc.shape[1]
    d_ssd = z.shape[1]
    n_heads = d_ssd // SSD_HEAD_DIM
    gw = d_ssd // SSD_GROUPS
    tok = lambda b, c: (b * nc + c, 0)
    cst = lambda b, c: (0, 0)
    return pl.pallas_call(
        functools.partial(_ssd_chunk_kernel, nc=nc, d_ssd=d_ssd, n_heads=n_heads),
        grid=(bsz, nc),
        in_specs=[pl.BlockSpec((CHUNK, cd), tok),
                  pl.BlockSpec((CHUNK, d_ssd), tok),
                  pl.BlockSpec((CHUNK, 2 * LANES), tok),
                  pl.BlockSpec((CONV_K, cd), cst),
                  pl.BlockSpec((1, cd), cst),
                  pl.BlockSpec((1, LANES), cst),
                  pl.BlockSpec((1, LANES), cst),
                  pl.BlockSpec((1, d_ssd), cst),
                  pl.BlockSpec((1, d_ssd), cst),
                  pl.BlockSpec((CHUNK, CHUNK), cst),
                  pl.BlockSpec((LANES, d_ssd), cst)],
        out_specs=[pl.BlockSpec((CHUNK, d_ssd), tok),
                   pl.BlockSpec((1, SSD_GROUPS, SSD_STATE, gw), lambda b, c: (b, 0, 0, 0))],
        out_shape=[jax.ShapeDtypeStruct((bsz * seq, d_ssd), BF16),
                   jax.ShapeDtypeStruct((bsz, SSD_GROUPS, SSD_STATE, gw), F32)],
        scratch_shapes=[pltpu.VMEM((CHUNK + SUBLANES, cd), F32),
                        pltpu.VMEM((SSD_GROUPS, SSD_STATE, gw), F32),
                        pltpu.VMEM((CHUNK, d_ssd), F32)],
        compiler_params=_cparams("arbitrary", "arbitrary"),
        name="ssd_chunk",
    )(xbc, z, gates, p["conv_ssd_w"], p["conv_ssd_b"], p["dt_bias_row"], p["a_log_row"],
      p["d_skip_row"], p["ssd_norm_g"], p["tril"], p["ssd_expand"])


def _mlstm_chunk_kernel(u_ref, v_ref, o_ref, g_ref, cw_ref, cb_ref, wq_ref, wk_ref, bi_ref, bf_ref, ng_ref,
                        tril_ref, y_ref, c_out_ref, n_out_ref, m_out_ref, upad, c_st, n_st, m_st,
                        *, nc, n_heads, dk):
    c = pl.program_id(1)
    dm = upad.shape[1]

    @pl.when(c == 0)
    def _():
        upad[0:SUBLANES, :] = jnp.zeros((SUBLANES, dm), F32)
        c_st[...] = jnp.zeros_like(c_st)
        n_st[...] = jnp.zeros_like(n_st)
        m_st[...] = jnp.zeros_like(m_st)

    upad[SUBLANES:SUBLANES + CHUNK, :] = u_ref[...]
    conv = cb_ref[...]
    for j in range(CONV_K):
        r0 = SUBLANES - (CONV_K - 1) + j
        conv = conv + cw_ref[j:j + 1, :] * upad[r0:r0 + CHUNK, :]
    upad[0:SUBLANES, :] = upad[CHUNK:CHUNK + SUBLANES, :]
    uc = _silu(conv).astype(BF16)

    logi = g_ref[:, 0:LANES] + bi_ref[...]
    logf = _log_sigmoid(g_ref[:, LANES:2 * LANES] + bf_ref[...])
    bcs = _dot_sel_l(tril_ref[...], logf)
    m_prev = m_st[0:1, :]
    b_end = bcs[CHUNK - 1:CHUNK, :]
    g_all = b_end - bcs + logi
    m_new = jnp.maximum(b_end + m_prev, jnp.max(g_all, axis=0, keepdims=True))
    w_old = jnp.exp(b_end + m_prev - m_new)
    w_s_all = jnp.exp(g_all - m_new)
    inter_all = bcs + m_prev
    bcs_t = bcs.T
    logi_t = logi.T

    row = lax.broadcasted_iota(jnp.int32, (CHUNK, CHUNK), 0)
    col = lax.broadcasted_iota(jnp.int32, (CHUNK, CHUNK), 1)
    causal = row >= col
    scale = dk ** -0.5

    for h in range(n_heads):
        gl = GATE_I0 + h
        sl = slice(h * dk, (h + 1) * dk)
        uc_h = uc[:, sl]
        q = _dot(uc_h, wq_ref[h])
        k = _dot(uc_h, wk_ref[h]) * scale
        qb = q.astype(BF16)
        v_h = v_ref[:, sl]
        vb = v_h.astype(BF16)
        dmat = jnp.where(causal, bcs[:, gl:gl + 1] - bcs_t[gl:gl + 1, :] + logi_t[gl:gl + 1, :], -jnp.inf)
        inter = inter_all[:, gl:gl + 1]
        m_t = jnp.maximum(inter, jnp.max(dmat, axis=1, keepdims=True))
        w_inter = jnp.exp(inter - m_t)
        att = _dot_nt(qb, k.astype(BF16)) * jnp.exp(dmat - m_t)
        c_h = c_st[h]
        n_h = n_st[h][0:1, :]
        num = _dot(att.astype(BF16), vb) + w_inter * _dot(qb, c_h.astype(BF16))
        den = jnp.sum(att, axis=1, keepdims=True) + w_inter * jnp.sum(q * n_h, axis=1, keepdims=True)
        hout = num / jnp.maximum(jnp.abs(den), jnp.exp(-m_t))
        mu = jnp.mean(hout, axis=-1, keepdims=True)
        var = jnp.mean(jnp.square(hout - mu), axis=-1, keepdims=True)
        hn = (hout - mu) * lax.rsqrt(var + LN_EPS) * ng_ref[:, sl]
        y_ref[:, sl] = (_sigmoid(o_ref[:, sl]) * hn).astype(y_ref.dtype)
        kw = k * w_s_all[:, gl:gl + 1]
        wo = w_old[:, gl:gl + 1]
        c_st[h] = wo * c_h + _dot(kw.T.astype(BF16), vb)
        n_st[h] = jnp.broadcast_to(wo * n_h + jnp.sum(kw, axis=0, keepdims=True), (SUBLANES, dk))

    m_st[...] = jnp.broadcast_to(m_new, (SUBLANES, LANES))

    @pl.when(c == nc - 1)
    def _():
        c_out_ref[0] = c_st[...]
        n_out_ref[0] = n_st[...]
        m_out_ref[0] = m_st[...]


def _mlstm_chunk(u, v, o, gates, p, bsz, seq):
    nc = seq // CHUNK
    dm = u.shape[1]
    n_heads, dk, _ = p["w_mq"].shape
    tok = lambda b, c: (b * nc + c, 0)
    cst = lambda b, c: (0, 0)
    cst3 = lambda b, c: (0, 0, 0)
    return pl.pallas_call(
        functools.partial(_mlstm_chunk_kernel, nc=nc, n_heads=n_heads, dk=dk),
        grid=(bsz, nc),
        in_specs=[pl.BlockSpec((CHUNK, dm), tok),
                  pl.BlockSpec((CHUNK, dm), tok),
                  pl.BlockSpec((CHUNK, dm), tok),
                  pl.BlockSpec((CHUNK, 2 * LANES), tok),
                  pl.BlockSpec((CONV_K, dm), cst),
                  pl.BlockSpec((1, dm), cst),
                  pl.BlockSpec((n_heads, dk, dk), cst3),
                  pl.BlockSpec((n_heads, dk, dk), cst3),
                  pl.BlockSpec((1, LANES), cst),
                  pl.BlockSpec((1, LANES), cst),
                  pl.BlockSpec((1, dm), cst),
                  pl.BlockSpec((CHUNK, CHUNK), cst)],
        out_specs=[pl.BlockSpec((CHUNK, dm), tok),
                   pl.BlockSpec((1, n_heads, dk, dk), lambda b, c: (b, 0, 0, 0)),
                   pl.BlockSpec((1, n_heads, SUBLANES, dk), lambda b, c: (b, 0, 0, 0)),
                   pl.BlockSpec((1, SUBLANES, LANES), lambda b, c: (b, 0, 0))],
        out_shape=[jax.ShapeDtypeStruct((bsz * seq, dm), BF16),
                   jax.ShapeDtypeStruct((bsz, n_heads, dk, dk), F32),
                   jax.ShapeDtypeStruct((bsz, n_heads, SUBLANES, dk), F32),
                   jax.ShapeDtypeStruct((bsz, SUBLANES, LANES), F32)],
        scratch_shapes=[pltpu.VMEM((CHUNK + SUBLANES, dm), F32),
                        pltpu.VMEM((n_heads, dk, dk), F32),
                        pltpu.VMEM((n_heads, SUBLANES, dk), F32),
                        pltpu.VMEM((SUBLANES, LANES), F32)],
        compiler_params=_cparams("arbitrary", "arbitrary"),
        name="mlstm_chunk",
    )(u, v, o, gates, p["conv_m_w"], p["conv_m_b"], p["w_mq"], p["w_mk"], p["b_i_row"], p["b_f_row"],
      p["mlstm_norm_g"], p["tril"])


def _col_bcast(row, n_rows):
    return jnp.broadcast_to(row, (LANES, row.shape[1])).T[0:n_rows, :]


def _ssd_step_kernel(xbc_ref, z_ref, g_ref, buf_ref, st_ref, cw_ref, cb_ref, dtb_ref, alog_ref, dsk_ref, ng_ref,
                     exp_ref, y_ref, st_out_ref, *, d_ssd, n_heads):
    gw = d_ssd // SSD_GROUPS
    x_row = xbc_ref[0]
    conv = cb_ref[...] + cw_ref[CONV_K - 1:CONV_K, :] * x_row
    for j in range(CONV_K - 1):
        conv = conv + cw_ref[j:j + 1, :] * buf_ref[0, j:j + 1, :]
    xc = _silu(conv)
    xs = xc[:, 0:d_ssd]

    lane = lax.broadcasted_iota(jnp.int32, (1, LANES), 1)
    a_row = jnp.where(lane < n_heads, -jnp.exp(alog_ref[...]), 0.0)
    dt = _softplus(g_ref[0][:, 0:LANES] + dtb_ref[...])
    dec = jnp.exp(dt * a_row)
    both = _dot_sel_r(jnp.concatenate([jnp.broadcast_to(dt, (SUBLANES, LANES)),
                                       jnp.broadcast_to(dec, (SUBLANES, LANES))], axis=0), exp_ref[...])
    dt_full = both[0:1]
    dec_full = both[SUBLANES:SUBLANES + 1]
    dtx = dt_full * xs
    dtx_col = _col_bcast(dtx, d_ssd)
    dec_col = _col_bcast(dec_full, d_ssd)

    y_parts = []
    for g in range(SSD_GROUPS):
        b_g = xc[:, d_ssd + g * SSD_STATE:d_ssd + (g + 1) * SSD_STATE]
        c_g = xc[:, d_ssd + (SSD_GROUPS + g) * SSD_STATE:d_ssd + (SSD_GROUPS + g + 1) * SSD_STATE]
        h_g = st_ref[0, g * gw:(g + 1) * gw, :]
        cb = jnp.sum(c_g * b_g, axis=-1, keepdims=True)
        c8 = jnp.broadcast_to(c_g, (SUBLANES, SSD_STATE)).astype(BF16)
        y_int = _dot_nt(c8, h_g.astype(BF16))[0:1]
        sl = slice(g * gw, (g + 1) * gw)
        y_parts.append(cb * dtx[:, sl] + dec_full[:, sl] * y_int)
        st_out_ref[0, sl, :] = h_g * dec_col[sl, :] + dtx_col[sl, :] * b_g
    y = jnp.concatenate(y_parts, axis=1) + dsk_ref[...] * xs
    y = y * _silu(z_ref[0])
    y = y * lax.rsqrt(jnp.mean(y * y, axis=-1, keepdims=True) + LN_EPS) * ng_ref[...]
    y_ref[0] = y.astype(y_ref.dtype)


def _ssd_step(xbc, z, gates, buf, state, p):
    nb, cd = xbc.shape
    d_ssd = z.shape[1]
    n_heads = d_ssd // SSD_HEAD_DIM
    row3 = lambda b: (b, 0, 0)
    cst = lambda b: (0, 0)
    st2 = state.reshape(nb, d_ssd, SSD_STATE)
    y, st_new = pl.pallas_call(
        functools.partial(_ssd_step_kernel, d_ssd=d_ssd, n_heads=n_heads),
        grid=(nb,),
        in_specs=[pl.BlockSpec((1, 1, cd), row3),
                  pl.BlockSpec((1, 1, d_ssd), row3),
                  pl.BlockSpec((1, 1, 2 * LANES), row3),
                  pl.BlockSpec((1, CONV_K - 1, cd), row3),
                  pl.BlockSpec((1, d_ssd, SSD_STATE), row3),
                  pl.BlockSpec((CONV_K, cd), cst),
                  pl.BlockSpec((1, cd), cst),
                  pl.BlockSpec((1, LANES), cst),
                  pl.BlockSpec((1, LANES), cst),
                  pl.BlockSpec((1, d_ssd), cst),
                  pl.BlockSpec((1, d_ssd), cst),
                  pl.BlockSpec((LANES, d_ssd), cst)],
        out_specs=[pl.BlockSpec((1, 1, d_ssd), row3),
                   pl.BlockSpec((1, d_ssd, SSD_STATE), row3)],
        out_shape=[jax.ShapeDtypeStruct((nb, 1, d_ssd), BF16),
                   jax.ShapeDtypeStruct((nb, d_ssd, SSD_STATE), F32)],
        compiler_params=_cparams("arbitrary"),
        name="ssd_step",
    )(xbc.reshape(nb, 1, cd), z.reshape(nb, 1, d_ssd), gates.reshape(nb, 1, 2 * LANES), buf, st2,
      p["conv_ssd_w"], p["conv_ssd_b"], p["dt_bias_row"], p["a_log_row"], p["d_skip_row"], p["ssd_norm_g"],
      p["ssd_expand"])
    return y.reshape(nb, d_ssd), st_new.reshape(state.shape)


def _mlstm_step_kernel(u_ref, v_ref, o_ref, g_ref, buf_ref, c_ref, n_ref, m_ref, cw_ref, cb_ref, wq_ref, wk_ref,
                       bi_ref, bf_ref, ng_ref, hexp_ref, y_ref, c_out_ref, n_out_ref, m_out_ref, *, n_heads, dk):
    conv = cb_ref[...] + cw_ref[CONV_K - 1:CONV_K, :] * u_ref[0]
    for j in range(CONV_K - 1):
        conv = conv + cw_ref[j:j + 1, :] * buf_ref[0, j:j + 1, :]
    uc = jnp.broadcast_to(_silu(conv), (SUBLANES, conv.shape[1])).astype(BF16)

    gt = g_ref[0]
    logi = gt[:, 0:LANES] + bi_ref[...]
    logf = _log_sigmoid(gt[:, LANES:2 * LANES] + bf_ref[...])
    m_prev = m_ref[0]
    inter = logf + m_prev
    m_t = jnp.maximum(inter, logi)
    w_inter = jnp.exp(inter - m_t)
    w_in = jnp.exp(logi - m_t)
    floor = jnp.exp(-m_t)
    rows = jnp.concatenate([w_inter, w_in, floor, jnp.zeros((SUBLANES - 3, LANES), F32)], axis=0)
    full = _dot_sel_r(rows, hexp_ref[...])
    scale = dk ** -0.5

    for h in range(n_heads):
        sl = slice(h * dk, (h + 1) * dk)
        q = _dot(uc[:, sl], wq_ref[h])[0:1]
        k = _dot(uc[:, sl], wk_ref[h])[0:1] * scale
        v_h = v_ref[0][:, sl]
        wi = full[0:1, sl]
        ws = full[1:2, sl]
        fl = full[2:3, sl]
        c_h = c_ref[0, h]
        n_h = n_ref[0, h:h + 1, :]
        att = jnp.sum(q * k, axis=-1, keepdims=True) * ws
        q8 = jnp.broadcast_to(q, (SUBLANES, dk)).astype(BF16)
        num = att * v_h + wi * _dot(q8, c_h.astype(BF16))[0:1]
        den = att + wi * jnp.sum(q * n_h, axis=-1, keepdims=True)
        hout = num / jnp.maximum(jnp.abs(den), fl)
        mu = jnp.mean(hout, axis=-1, keepdims=True)
        var = jnp.mean(jnp.square(hout - mu), axis=-1, keepdims=True)
        hn = (hout - mu) * lax.rsqrt(var + LN_EPS) * ng_ref[:, sl]
        y_ref[0, :, sl] = (_sigmoid(o_ref[0][:, sl]) * hn).astype(y_ref.dtype)
        kw = k * ws
        kw_col = _col_bcast(kw, dk)
        wi_full = jnp.broadcast_to(wi, (dk, dk))
        for half in range(dk // LANES):
            hs = slice(half * LANES, (half + 1) * LANES)
            c_out_ref[0, h, :, hs] = wi_full[:, hs] * c_h[:, hs] + kw_col * v_h[:, hs]
        n_out_ref[0, h:h + 1, :] = wi * n_h + kw
    m_out_ref[0] = m_t


def _mlstm_step(u, v, o, gates, buf, c_st, n_st, m_row, p):
    nb, dm = u.shape
    n_heads, dk, _ = p["w_mq"].shape
    row3 = lambda b: (b, 0, 0)
    cst = lambda b: (0, 0)
    cst3 = lambda b: (0, 0, 0)
    r3 = lambda a: a.reshape(nb, 1, a.shape[-1])
    y, c_new, n_new, m_new = pl.pallas_call(
        functools.partial(_mlstm_step_kernel, n_heads=n_heads, dk=dk),
        grid=(nb,),
        in_specs=[pl.BlockSpec((1, 1, dm), row3),
                  pl.BlockSpec((1, 1, dm), row3),
                  pl.BlockSpec((1, 1, dm), row3),
                  pl.BlockSpec((1, 1, 2 * LANES), row3),
                  pl.BlockSpec((1, CONV_K - 1, dm), row3),
                  pl.BlockSpec((1, n_heads, dk, dk), lambda b: (b, 0, 0, 0)),
                  pl.BlockSpec((1, n_heads, dk), row3),
                  pl.BlockSpec((1, 1, LANES), row3),
                  pl.BlockSpec((CONV_K, dm), cst),
                  pl.BlockSpec((1, dm), cst),
                  pl.BlockSpec((n_heads, dk, dk), cst3),
                  pl.BlockSpec((n_heads, dk, dk), cst3),
                  pl.BlockSpec((1, LANES), cst),
                  pl.BlockSpec((1, LANES), cst),
                  pl.BlockSpec((1, dm), cst),
                  pl.BlockSpec((LANES, dm), cst)],
        out_specs=[pl.BlockSpec((1, 1, dm), row3),
                   pl.BlockSpec((1, n_heads, dk, dk), lambda b: (b, 0, 0, 0)),
                   pl.BlockSpec((1, n_heads, dk), row3),
                   pl.BlockSpec((1, 1, LANES), row3)],
        out_shape=[jax.ShapeDtypeStruct((nb, 1, dm), BF16),
                   jax.ShapeDtypeStruct(c_st.shape, F32),
                   jax.ShapeDtypeStruct(n_st.shape, F32),
                   jax.ShapeDtypeStruct((nb, 1, LANES), F32)],
        compiler_params=_cparams("arbitrary"),
        name="mlstm_step",
    )(r3(u), r3(v), r3(o), r3(gates), buf, c_st, n_st, m_row, p["conv_m_w"], p["conv_m_b"], p["w_mq"],
      p["w_mk"], p["b_i_row"], p["b_f_row"], p["mlstm_norm_g"], p["mlstm_expand"])
    return y.reshape(nb, dm), c_new, n_new, m_new


def _outproj_kernel(ys_ref, hm_ref, x_ref, w_ref, g_ref, b_ref, x1t_ref, *, alpha):
    d_ssd = ys_ref.shape[1]
    mix = _dot(ys_ref[...], w_ref[0:d_ssd, :]) + _dot(hm_ref[...], w_ref[d_ssd:, :])
    r = alpha * x_ref[...] + mix
    mu = jnp.mean(r, axis=-1, keepdims=True)
    var = jnp.mean(jnp.square(r - mu), axis=-1, keepdims=True)
    x1 = (r - mu) * lax.rsqrt(var + LN_EPS) * g_ref[...] + b_ref[...]
    x1t_ref[...] = x1.T


def _outproj(ys, hm, x2, p, alpha, tm):
    m, d = x2.shape
    d_ssd, d_m = ys.shape[1], hm.shape[1]
    return pl.pallas_call(
        functools.partial(_outproj_kernel, alpha=alpha),
        grid=(m // tm,),
        in_specs=[pl.BlockSpec((tm, d_ssd), lambda i: (i, 0)),
                  pl.BlockSpec((tm, d_m), lambda i: (i, 0)),
                  pl.BlockSpec((tm, d), lambda i: (i, 0)),
                  pl.BlockSpec((d_ssd + d_m, d), lambda i: (0, 0)),
                  pl.BlockSpec((1, d), lambda i: (0, 0)),
                  pl.BlockSpec((1, d), lambda i: (0, 0))],
        out_specs=pl.BlockSpec((d, tm), lambda i: (0, i)),
        out_shape=jax.ShapeDtypeStruct((d, m), F32),
        compiler_params=_cparams("arbitrary"),
        name="outproj_ln1",
    )(ys, hm, x2, p["w_out"], p["ln1_g"], p["ln1_b"])


def _extract_top(v, k):
    rows = v.shape[0]
    idx = lax.broadcasted_iota(jnp.int32, v.shape, 0)
    out = []
    for _ in range(k):
        m = jnp.max(v, axis=0, keepdims=True)
        first = jnp.min(jnp.where(v == m, idx, rows), axis=0, keepdims=True)
        v = jnp.where(idx == first, -jnp.inf, v)
        out.append(m)
    return out


def _peer_topk_kernel(x1t_ref, wqt_ref, keys_ref, s1_ref, s2_ref, st_ref, q_scr, *, n_heads, n_keys):
    tq = x1t_ref.shape[1]
    q_scr[...] = _dot(wqt_ref[...], x1t_ref[...].astype(BF16))
    half = q_scr.shape[0] // (2 * n_heads)
    sub = lax.broadcasted_iota(jnp.int32, (SUBLANES, tq), 0)

    def head(h, carry):
        base = pl.multiple_of(h * 2 * half, 2 * half)
        tops = []
        for side, s_ref in ((0, s1_ref), (1, s2_ref)):
            qs = q_scr[pl.ds(base + side * half, half), :].astype(BF16)
            s = _dot(keys_ref[h, side], qs)
            s_ref[h] = s
            tops.append(_extract_top(s, PEER_TOPK))
        a, b = tops
        a_lo = jnp.concatenate(a[0:SUBLANES], axis=0)
        a_hi = jnp.concatenate(a[SUBLANES:2 * SUBLANES], axis=0)
        b_lo = jnp.concatenate(b[0:SUBLANES], axis=0)
        b_hi = jnp.concatenate(b[SUBLANES:2 * SUBLANES], axis=0)
        tiles = [a[0] + b_lo, a[0] + b_hi]
        for i in range(1, SUBLANES):
            tiles.append(jnp.where(sub < PEER_TOPK // (i + 1), a[i] + b_lo, -jnp.inf))
        tiles.append(a_hi + b[0])
        cand = jnp.concatenate(tiles, axis=0)
        tau = _extract_top(cand, PEER_TOPK)[PEER_TOPK - 1]
        top = a[0] + b[0]
        zsum = jnp.sum(jnp.where(cand >= tau, jnp.exp(cand - top), 0.0), axis=0, keepdims=True)
        st_ref[h] = jnp.concatenate([tau, a[0], b[0], 1.0 / zsum,
                                     jnp.zeros((SUBLANES - 4, tq), F32)], axis=0)
        return carry

    lax.fori_loop(0, n_heads, head, 0)


def _peer_topk(x1t, p, tq):
    d, m = x1t.shape
    n_heads, _, n_keys, half = p["peer_keys"].shape
    nq = p["peer_wqT"].shape[0]
    tokb = lambda i: (0, 0, i)
    return pl.pallas_call(
        functools.partial(_peer_topk_kernel, n_heads=n_heads, n_keys=n_keys),
        grid=(m // tq,),
        in_specs=[pl.BlockSpec((d, tq), lambda i: (0, i)),
                  pl.BlockSpec((nq, d), lambda i: (0, 0)),
                  pl.BlockSpec((n_heads, 2, n_keys, half), lambda i: (0, 0, 0, 0))],
        out_specs=[pl.BlockSpec((n_heads, n_keys, tq), tokb),
                   pl.BlockSpec((n_heads, n_keys, tq), tokb),
                   pl.BlockSpec((n_heads, SUBLANES, tq), tokb)],
        out_shape=[jax.ShapeDtypeStruct((n_heads, n_keys, m), F32),
                   jax.ShapeDtypeStruct((n_heads, n_keys, m), F32),
                   jax.ShapeDtypeStruct((n_heads, SUBLANES, m), F32)],
        scratch_shapes=[pltpu.VMEM((nq, tq), F32)],
        compiler_params=_cparams("arbitrary"),
        name="peer_topk",
    )(x1t, p["peer_wqT"], p["peer_keys"])


def _peer_dense_kernel(x1t_ref, s1_ref, s2_ref, st_ref, u_ref, vt_ref, g_ref, b_ref, out_ref,
                       xb_scr, e2_scr, c_scr, acc_scr, s_scr, w_scr, s1c_scr, cc_scr, *, alpha, n_heads, nj):
    j = pl.program_id(1)
    ec, tb = s_scr.shape
    n_keys = s1_ref.shape[1]
    na = ec // n_keys

    @pl.when(j == 0)
    def _():
        xb_scr[...] = x1t_ref[...].astype(BF16)
        for h in range(n_heads):
            e2_scr[h] = jnp.exp(s2_ref[h] - st_ref[h, 2:3, :])
            c_scr[h] = jnp.exp(s1_ref[h] - st_ref[h, 1:2, :]) * st_ref[h, 3:4, :]
        acc_scr[...] = jnp.zeros_like(acc_scr)

    s_scr[...] = _dot(u_ref[...], xb_scr[...])

    a0 = pl.multiple_of(j * na, na)
    for h in range(n_heads):
        s1_blk = s1_ref[h, pl.ds(a0, na), :]
        c_blk = c_scr[h, pl.ds(a0, na), :]
        for al in range(na):
            s1c_scr[h * na + al] = s1_blk[al:al + 1, :]
            cc_scr[h * na + al] = c_blk[al:al + 1, :]

    def first_key(al, carry):
        r0 = pl.multiple_of(al * n_keys, n_keys)
        for t in range(tb // LANES):
            ls = slice(t * LANES, (t + 1) * LANES)
            gate = jnp.zeros((n_keys, LANES), F32)
            for h in range(n_heads):
                ssum = s2_ref[h, :, ls] + s1c_scr[h * na + al][:, ls]
                gate = gate + jnp.where(ssum >= st_ref[h, 0:1, ls],
                                        e2_scr[h, :, ls] * cc_scr[h * na + al][:, ls], 0.0)
            hid = _gelu_exact(s_scr[pl.ds(r0, n_keys), ls])
            w_scr[pl.ds(r0, n_keys), ls] = (gate * hid).astype(BF16)
        return carry

    lax.fori_loop(0, na, first_key, 0)
    acc_scr[...] += _dot(vt_ref[...], w_scr[...])

    @pl.when(j == nj - 1)
    def _():
        r = alpha * x1t_ref[...] + acc_scr[...]
        mu = jnp.mean(r, axis=0, keepdims=True)
        var = jnp.mean(jnp.square(r - mu), axis=0, keepdims=True)
        y = (r - mu) * lax.rsqrt(var + LN_EPS) * g_ref[...] + b_ref[...]
        out_ref[...] = y.T


def _peer_dense(x1t, s1, s2, st, p, alpha, tb, ec):
    d, m = x1t.shape
    n_heads, n_keys, _ = s1.shape
    n_exp = p["peer_u"].shape[0]
    nj = n_exp // ec
    tokb = lambda i, j: (0, 0, i)
    return pl.pallas_call(
        functools.partial(_peer_dense_kernel, alpha=alpha, n_heads=n_heads, nj=nj),
        grid=(m // tb, nj),
        in_specs=[pl.BlockSpec((d, tb), lambda i, j: (0, i)),
                  pl.BlockSpec((n_heads, n_keys, tb), tokb),
                  pl.BlockSpec((n_heads, n_keys, tb), tokb),
                  pl.BlockSpec((n_heads, SUBLANES, tb), tokb),
                  pl.BlockSpec((ec, d), lambda i, j: (j, 0)),
                  pl.BlockSpec((d, ec), lambda i, j: (0, j)),
                  pl.BlockSpec((d, 1), lambda i, j: (0, 0)),
                  pl.BlockSpec((d, 1), lambda i, j: (0, 0))],
        out_specs=pl.BlockSpec((tb, d), lambda i, j: (i, 0)),
        out_shape=jax.ShapeDtypeStruct((m, d), F32),
        scratch_shapes=[pltpu.VMEM((d, tb), BF16),
                        pltpu.VMEM((n_heads, n_keys, tb), F32),
                        pltpu.VMEM((n_heads, n_keys, tb), F32),
                        pltpu.VMEM((d, tb), F32),
                        pltpu.VMEM((ec, tb), F32),
                        pltpu.VMEM((ec, tb), BF16),
                        pltpu.VMEM((n_heads * (ec // n_keys), 1, tb), F32),
                        pltpu.VMEM((n_heads * (ec // n_keys), 1, tb), F32)],
        compiler_params=_cparams("arbitrary", "arbitrary"),
        name="peer_dense",
    )(x1t, s1, s2, st, p["peer_u"], p["peer_vT"], p["ln2_g_col"], p["ln2_b_col"])


def _pad_row(vec, offset):
    return jnp.zeros((1, LANES), F32).at[0, offset:offset + vec.shape[0]].set(vec.astype(F32))


def _prep_layer(l, w_in, conv_ssd_w, conv_ssd_b, dt_bias, a_log, d_skip, ssd_norm_g, conv_m_w, conv_m_b,
                w_mq, w_mk, b_igate, b_fgate, mlstm_norm_g, w_out, ln1_g, ln1_b, peer_wq, peer_keys,
                peer_u, peer_v, ln2_g, ln2_b):
    d_model = w_in.shape[1]
    d_ssd = ssd_norm_g.shape[1]
    cd = conv_ssd_w.shape[2]
    n_sh = dt_bias.shape[1]
    d_m = conv_m_w.shape[2]
    n_mh = w_mq.shape[1]
    dk = w_mq.shape[2]
    assert n_sh <= GATE_I0 and GATE_I0 + n_mh <= LANES and d_ssd == n_sh * SSD_HEAD_DIM
    w = w_in[l]
    offs = [0]
    for n in (d_ssd, cd, n_sh, d_m, d_m, d_m, n_mh, n_mh):
        offs.append(offs[-1] + n)
    z_w, xbc_w, dt_w, u_w, v_w, o_w, i_w, f_w = (w[:, offs[k]:offs[k + 1]] for k in range(8))
    gate_a = jnp.zeros((d_model, LANES), F32).at[:, 0:n_sh].set(dt_w).at[:, GATE_I0:GATE_I0 + n_mh].set(i_w)
    gate_b = jnp.zeros((d_model, LANES), F32).at[:, GATE_I0:GATE_I0 + n_mh].set(f_w)
    w_cat = jnp.concatenate([z_w, xbc_w, u_w, v_w, o_w, gate_a, gate_b], axis=1).astype(BF16)
    widths = (d_ssd, cd, d_m, d_m, d_m, 2 * LANES)

    lane = jnp.arange(LANES)[:, None]
    ssd_expand = (lane == (jnp.arange(d_ssd)[None, :] // SSD_HEAD_DIM)).astype(BF16)
    mlstm_expand = (lane == (GATE_I0 + jnp.arange(d_m)[None, :] // dk)).astype(BF16)
    t = jnp.arange(CHUNK)
    tril = (t[:, None] >= t[None, :]).astype(BF16)
    p = {
        "w_cat": w_cat,
        "conv_ssd_w": conv_ssd_w[l], "conv_ssd_b": conv_ssd_b[l][None, :],
        "dt_bias_row": _pad_row(dt_bias[l], 0), "a_log_row": _pad_row(a_log[l], 0),
        "d_skip_row": jnp.repeat(d_skip[l], SSD_HEAD_DIM)[None, :], "ssd_norm_g": ssd_norm_g[l][None, :],
        "conv_m_w": conv_m_w[l], "conv_m_b": conv_m_b[l][None, :],
        "w_mq": w_mq[l].astype(BF16), "w_mk": w_mk[l].astype(BF16),
        "b_i_row": _pad_row(b_igate[l], GATE_I0), "b_f_row": _pad_row(b_fgate[l], GATE_I0),
        "mlstm_norm_g": mlstm_norm_g[l].reshape(1, d_m),
        "w_out": w_out[l].astype(BF16), "ln1_g": ln1_g[l][None, :], "ln1_b": ln1_b[l][None, :],
        "peer_wqT": peer_wq[l].T.astype(BF16), "peer_keys": peer_keys[l].astype(BF16),
        "peer_u": peer_u[l].astype(BF16), "peer_vT": peer_v[l].T.astype(BF16),
        "ln2_g_col": ln2_g[l][:, None], "ln2_b_col": ln2_b[l][:, None],
        "tril": tril, "ssd_expand": ssd_expand, "mlstm_expand": mlstm_expand,
    }
    return p, widths


def _tile_sizes(m):
    tm = 256 if m % 256 == 0 else LANES
    tb = 512 if m % 512 == 0 else LANES
    return tm, tb


def _peer_block(x1t, p, alpha):
    m = x1t.shape[1]
    tm, tb = _tile_sizes(m)
    s1, s2, st = _peer_topk(x1t, p, tm)
    return _peer_dense(x1t, s1, s2, st, p, alpha, tb, ec=1024)


def _layer_prompt(x, p, widths, alpha):
    bsz, seq, d = x.shape
    assert seq % CHUNK == 0 and seq >= CONV_K - 1
    x2 = x.reshape(bsz * seq, d)
    tm, _ = _tile_sizes(bsz * seq)
    z, xbc, u, v, o, gates = _inproj(x2, p["w_cat"], widths, tm)
    y_s, h_t = _ssd_chunk(xbc, z, gates, p, bsz, seq)
    h_m, c_new, n_new, m_new = _mlstm_chunk(u, v, o, gates, p, bsz, seq)
    x1t = _outproj(y_s, h_m, x2, p, alpha, tm)
    y = _peer_block(x1t, p, alpha).reshape(bsz, seq, d)
    n_mh = c_new.shape[1]
    d_ssd = z.shape[1]
    n_sh = d_ssd // SSD_HEAD_DIM
    hg = n_sh // SSD_GROUPS
    new_ssd = h_t.reshape(bsz, SSD_GROUPS, SSD_STATE, hg, SSD_HEAD_DIM).transpose(0, 1, 3, 4, 2)
    new_ssd = new_ssd.reshape(bsz, n_sh, SSD_HEAD_DIM, SSD_STATE)
    states = (new_ssd,
              xbc.reshape(bsz, seq, -1)[:, seq - (CONV_K - 1):, :],
              c_new,
              n_new[:, :, 0, :],
              m_new[:, 0, GATE_I0:GATE_I0 + n_mh],
              u.reshape(bsz, seq, -1)[:, seq - (CONV_K - 1):, :])
    return y, states


def _layer_sample(x, st, p, widths, alpha):
    st_ssd, st_ssd_conv, st_c, st_n, st_m, st_mconv = st
    nb, seq, d = x.shape
    assert seq == 1 and nb % LANES == 0
    x2 = x.reshape(nb, d)
    z, xbc, u, v, o, gates = _inproj(x2, p["w_cat"], widths, LANES)
    y_s, new_ssd = _ssd_step(xbc, z, gates, st_ssd_conv, st_ssd, p)
    n_mh = st_c.shape[1]
    m_row = jnp.zeros((nb, 1, LANES), F32).at[:, 0, GATE_I0:GATE_I0 + n_mh].set(st_m)
    h_m, c_new, n_new, m_new = _mlstm_step(u, v, o, gates, st_mconv, st_c, st_n, m_row, p)
    x1t = _outproj(y_s, h_m, x2, p, alpha, LANES)
    y = _peer_block(x1t, p, alpha).reshape(nb, seq, d)
    states = (new_ssd,
              jnp.concatenate([st_ssd_conv[:, 1:, :], xbc[:, None, :]], axis=1),
              c_new,
              n_new,
              m_new[:, 0, GATE_I0:GATE_I0 + n_mh],
              jnp.concatenate([st_mconv[:, 1:, :], u[:, None, :]], axis=1))
    return y, states


def kernel(x_prompt, x_sample, state_ssd, state_ssd_conv, state_mlstm_c, state_mlstm_n, state_mlstm_m,
           state_mlstm_conv, w_in, conv_ssd_w, conv_ssd_b, dt_bias, a_log, d_skip, ssd_norm_g, conv_m_w,
           conv_m_b, w_mq, w_mk, b_igate, b_fgate, mlstm_norm_g, w_out, ln1_g, ln1_b, peer_wq, peer_keys,
           peer_u, peer_v, ln2_g, ln2_b):
    depth = w_in.shape[0]
    alpha = (2.0 * depth) ** 0.25
    states = (state_ssd, state_ssd_conv, state_mlstm_c, state_mlstm_n, state_mlstm_m, state_mlstm_conv)
    yp, ys = x_prompt, x_sample
    p_list, s_list = [], []
    for l in range(depth):
        p, widths = _prep_layer(l, w_in, conv_ssd_w, conv_ssd_b, dt_bias, a_log, d_skip, ssd_norm_g, conv_m_w,
                                conv_m_b, w_mq, w_mk, b_igate, b_fgate, mlstm_norm_g, w_out, ln1_g, ln1_b,
                                peer_wq, peer_keys, peer_u, peer_v, ln2_g, ln2_b)
        yp, p_new = _layer_prompt(yp, p, widths, alpha)
        ys, s_new = _layer_sample(ys, tuple(s[l] for s in states), p, widths, alpha)
        p_list.append(p_new)
        s_list.append(s_new)
    p_out = tuple(jnp.stack([pn[i] for pn in p_list]) for i in range(6))
    s_out = tuple(jnp.stack([sn[i] for sn in s_list]) for i in range(6))
    return (yp, ys) + p_out + s_out
```

```python
import functools
import math

import jax
import jax.numpy as jnp
from jax import lax
from jax.experimental import pallas as pl
from jax.experimental.pallas import tpu as pltpu

F32 = jnp.float32
BF16 = jnp.bfloat16

LANES = 128
SUBLANES = 8
CHUNK = 128
CONV_K = 4
LN_EPS = 1e-5
VMEM_LIMIT = 52 * 1024 * 1024

SSD_HEAD_DIM = 64
SSD_STATE = 128
SSD_GROUPS = 2
PEER_TOPK = 16
GATE_I0 = 16
N_STATS = 4


def _cparams(*sem):
    return pltpu.CompilerParams(dimension_semantics=sem, vmem_limit_bytes=VMEM_LIMIT)


def _dot(a, b):
    return jnp.dot(a, b, preferred_element_type=F32)


def _dot_nt(a, b):
    return lax.dot_general(a, b, (((1,), (1,)), ((), ())), preferred_element_type=F32)


def _split3(x):
    h = x.astype(BF16)
    r = x - h.astype(F32)
    m = r.astype(BF16)
    lo = (r - m.astype(F32)).astype(BF16)
    return h, m, lo


def _dot_sel_l(sel_bf16, x):
    h, m, lo = _split3(x)
    return _dot(sel_bf16, h) + _dot(sel_bf16, m) + _dot(sel_bf16, lo)


def _dot_sel_r(x, sel_bf16):
    h, m, lo = _split3(x)
    return _dot(h, sel_bf16) + _dot(m, sel_bf16) + _dot(lo, sel_bf16)


def _sigmoid(x):
    return 1.0 / (1.0 + jnp.exp(-x))


def _silu(x):
    return x * _sigmoid(x)


def _softplus(x):
    return jnp.maximum(x, 0.0) + jnp.log1p(jnp.exp(-jnp.abs(x)))


def _log_sigmoid(x):
    return -_softplus(-x)


def _gelu_exact(x):
    return 0.5 * x * (1.0 + lax.erf(x * (1.0 / math.sqrt(2.0))))


def _inproj_kernel(x_ref, w_ref, *out_refs):
    xb = x_ref[...].astype(BF16)
    off = 0
    for ref in out_refs:
        n = ref.shape[-1]
        ref[...] = _dot(xb, w_ref[:, off:off + n])
        off += n


def _inproj(x2, w_cat, widths, tm):
    m, d = x2.shape
    n_all = w_cat.shape[1]
    return pl.pallas_call(
        _inproj_kernel,
        grid=(m // tm,),
        in_specs=[pl.BlockSpec((tm, d), lambda i: (i, 0)),
                  pl.BlockSpec((d, n_all), lambda i: (0, 0))],
        out_specs=[pl.BlockSpec((tm, n), lambda i: (i, 0)) for n in widths],
        out_shape=[jax.ShapeDtypeStruct((m, n), F32) for n in widths],
        compiler_params=_cparams("arbitrary"),
        name="inproj",
    )(x2, w_cat)


def _ssd_chunk_kernel(xbc_ref, z_ref, g_ref, cw_ref, cb_ref, dtb_ref, alog_ref, dsk_ref, ng_ref,
                      tril_ref, exp_ref, y_ref, hT_out_ref, xpad, hT, ybuf, *, nc, d_ssd, n_heads):
    c = pl.program_id(1)
    cd = xpad.shape[1]
    hg = n_heads // SSD_GROUPS
    gw = d_ssd // SSD_GROUPS

    @pl.when(c == 0)
    def _():
        xpad[0:SUBLANES, :] = jnp.zeros((SUBLANES, cd), F32)
        hT[...] = jnp.zeros_like(hT)

    xpad[SUBLANES:SUBLANES + CHUNK, :] = xbc_ref[...]
    conv = cb_ref[...]
    for j in range(CONV_K):
        r0 = SUBLANES - (CONV_K - 1) + j
        conv = conv + cw_ref[j:j + 1, :] * xpad[r0:r0 + CHUNK, :]
    xpad[0:SUBLANES, :] = xpad[CHUNK:CHUNK + SUBLANES, :]
    xc = _silu(conv)
    xs = xc[:, 0:d_ssd]

    lane = lax.broadcasted_iota(jnp.int32, (1, LANES), 1)
    a_row = jnp.where(lane < n_heads, -jnp.exp(alog_ref[...]), 0.0)
    dt = _softplus(g_ref[:, 0:LANES] + dtb_ref[...])
    acs = _dot_sel_l(tril_ref[...], dt * a_row)
    dt_t = dt.T
    acs_t = acs.T
    eacs = jnp.exp(acs)
    dec = jnp.exp(acs[CHUNK - 1:CHUNK, :] - acs) * dt
    both = _dot_sel_r(jnp.concatenate([eacs, dec], axis=0), exp_ref[...])
    eacs_full = both[0:CHUNK]
    dec_full = both[CHUNK:2 * CHUNK]

    row = lax.broadcasted_iota(jnp.int32, (CHUNK, CHUNK), 0)
    col = lax.broadcasted_iota(jnp.int32, (CHUNK, CHUNK), 1)
    causal = row >= col

    for g in range(SSD_GROUPS):
        b_g = xc[:, d_ssd + g * SSD_STATE:d_ssd + (g + 1) * SSD_STATE]
        c_g = xc[:, d_ssd + (SSD_GROUPS + g) * SSD_STATE:d_ssd + (SSD_GROUPS + g + 1) * SSD_STATE]
        b_gb = b_g.astype(BF16)
        c_gb = c_g.astype(BF16)
        cb = _dot_nt(c_gb, b_gb)
        h_g = hT[g]
        y_inter = _dot(c_gb, h_g.astype(BF16)) * eacs_full[:, g * gw:(g + 1) * gw]
        for hl in range(hg):
            h = g * hg + hl
            diff = acs[:, h:h + 1] - acs_t[h:h + 1, :]
            lmat = jnp.exp(jnp.where(causal, diff, -jnp.inf))
            mm = cb * lmat * dt_t[h:h + 1, :]
            x_h = xs[:, h * SSD_HEAD_DIM:(h + 1) * SSD_HEAD_DIM]
            ybuf[:, h * SSD_HEAD_DIM:(h + 1) * SSD_HEAD_DIM] = _dot(mm.astype(BF16), x_h.astype(BF16))
        xd = xs[:, g * gw:(g + 1) * gw] * dec_full[:, g * gw:(g + 1) * gw]
        hT[g] = h_g * eacs_full[CHUNK - 1:CHUNK, g * gw:(g + 1) * gw] + _dot(b_g.T.astype(BF16), xd.astype(BF16))
        ybuf[:, g * gw:(g + 1) * gw] = ybuf[:, g * gw:(g + 1) * gw] + y_inter

    y = ybuf[...] + dsk_ref[...] * xs
    y = y * _silu(z_ref[...])
    y = y * lax.rsqrt(jnp.mean(y * y, axis=-1, keepdims=True) + LN_EPS) * ng_ref[...]
    y_ref[...] = y.astype(y_ref.dtype)

    @pl.when(c == nc - 1)
    def _():
        hT_out_ref[0] = hT[...]


def _ssd_chunk(xbc, z, gates, p, bsz, seq):
    nc = seq // CHUNK
    cd = xbc.shape[1]
    d_ssd = z.shape[1]
    n_heads = d_ssd // SSD_HEAD_DIM
    gw = d_ssd // SSD_GROUPS
    tok = lambda b, c: (b * nc + c, 0)
    cst = lambda b, c: (0, 0)
    return pl.pallas_call(
        functools.partial(_ssd_chunk_kernel, nc=nc, d_ssd=d_ssd, n_heads=n_heads),
        grid=(bsz, nc),
        in_specs=[pl.BlockSpec((CHUNK, cd), tok),
                  pl.BlockSpec((CHUNK, d_ssd), tok),
                  pl.BlockSpec((CHUNK, 2 * LANES), tok),
                  pl.BlockSpec((CONV_K, cd), cst),
                  pl.BlockSpec((1, cd), cst),
                  pl.BlockSpec((1, LANES), cst),
                  pl.BlockSpec((1, LANES), cst),
                  pl.BlockSpec((1, d_ssd), cst),
                  pl.BlockSpec((1, d_ssd), cst),
                  pl.BlockSpec((CHUNK, CHUNK), cst),
                  pl.BlockSpec((LANES, d_ssd), cst)],
        out_specs=[pl.BlockSpec((CHUNK, d_ssd), tok),
                   pl.BlockSpec((1, SSD_GROUPS, SSD_STATE, gw), lambda b, c: (b, 0, 0, 0))],
        out_shape=[jax.ShapeDtypeStruct((bsz * seq, d_ssd), BF16),
                   jax.ShapeDtypeStruct((bsz, SSD_GROUPS, SSD_STATE, gw), F32)],
        scratch_shapes=[pltpu.VMEM((CHUNK + SUBLANES, cd), F32),
                        pltpu.VMEM((SSD_GROUPS, SSD_STATE, gw), F32),
                        pltpu.VMEM((CHUNK, d_ssd), F32)],
        compiler_params=_cparams("arbitrary", "arbitrary"),
        name="ssd_chunk",
    )(xbc, z, gates, p["conv_ssd_w"], p["conv_ssd_b"], p["dt_bias_row"], p["a_log_row"],
      p["d_skip_row"], p["ssd_norm_g"], p["tril"], p["ssd_expand"])


def _mlstm_chunk_kernel(u_ref, v_ref, o_ref, g_ref, cw_ref, cb_ref, wq_ref, wk_ref, bi_ref, bf_ref, ng_ref,
                        tril_ref, y_ref, c_out_ref, n_out_ref, m_out_ref, upad, c_st, n_st, m_st,
                        *, nc, n_heads, dk):
    c = pl.program_id(1)
    dm = upad.shape[1]

    @pl.when(c == 0)
    def _():
        upad[0:SUBLANES, :] = jnp.zeros((SUBLANES, dm), F32)
        c_st[...] = jnp.zeros_like(c_st)
        n_st[...] = jnp.zeros_like(n_st)
        m_st[...] = jnp.zeros_like(m_st)

    upad[SUBLANES:SUBLANES + CHUNK, :] = u_ref[...]
    conv = cb_ref[...]
    for j in range(CONV_K):
        r0 = SUBLANES - (CONV_K - 1) + j
        conv = conv + cw_ref[j:j + 1, :] * upad[r0:r0 + CHUNK, :]
    upad[0:SUBLANES, :] = upad[CHUNK:CHUNK + SUBLANES, :]
    uc = _silu(conv).astype(BF16)

    logi = g_ref[:, 0:LANES] + bi_ref[...]
    logf = _log_sigmoid(g_ref[:, LANES:2 * LANES] + bf_ref[...])
    bcs = _dot_sel_l(tril_ref[...], logf)
    m_prev = m_st[0:1, :]
    b_end = bcs[CHUNK - 1:CHUNK, :]
    g_all = b_end - bcs + logi
    m_new = jnp.maximum(b_end + m_prev, jnp.max(g_all, axis=0, keepdims=True))
    w_old = jnp.exp(b_end + m_prev - m_new)
    w_s_all = jnp.exp(g_all - m_new)
    inter_all = bcs + m_prev
    bcs_t = bcs.T
    logi_t = logi.T

    row = lax.broadcasted_iota(jnp.int32, (CHUNK, CHUNK), 0)
    col = lax.broadcasted_iota(jnp.int32, (CHUNK, CHUNK), 1)
    causal = row >= col
    scale = dk ** -0.5

    for h in range(n_heads):
        gl = GATE_I0 + h
        sl = slice(h * dk, (h + 1) * dk)
        uc_h = uc[:, sl]
        q = _dot(uc_h, wq_ref[h])
        k = _dot(uc_h, wk_ref[h]) * scale
        qb = q.astype(BF16)
        v_h = v_ref[:, sl]
        vb = v_h.astype(BF16)
        dmat = jnp.where(causal, bcs[:, gl:gl + 1] - bcs_t[gl:gl + 1, :] + logi_t[gl:gl + 1, :], -jnp.inf)
        inter = inter_all[:, gl:gl + 1]
        m_t = jnp.maximum(inter, jnp.max(dmat, axis=1, keepdims=True))
        w_inter = jnp.exp(inter - m_t)
        att = _dot_nt(qb, k.astype(BF16)) * jnp.exp(dmat - m_t)
        c_h = c_st[h]
        n_h = n_st[h][0:1, :]
        num = _dot(att.astype(BF16), vb) + w_inter * _dot(qb, c_h.astype(BF16))
        den = jnp.sum(att, axis=1, keepdims=True) + w_inter * jnp.sum(q * n_h, axis=1, keepdims=True)
        hout = num / jnp.maximum(jnp.abs(den), jnp.exp(-m_t))
        mu = jnp.mean(hout, axis=-1, keepdims=True)
        var = jnp.mean(jnp.square(hout - mu), axis=-1, keepdims=True)
        hn = (hout - mu) * lax.rsqrt(var + LN_EPS) * ng_ref[:, sl]
        y_ref[:, sl] = (_sigmoid(o_ref[:, sl]) * hn).astype(y_ref.dtype)
        kw = k * w_s_all[:, gl:gl + 1]
        wo = w_old[:, gl:gl + 1]
        c_st[h] = wo * c_h + _dot(kw.T.astype(BF16), vb)
        n_st[h] = jnp.broadcast_to(wo * n_h + jnp.sum(kw, axis=0, keepdims=True), (SUBLANES, dk))

    m_st[...] = jnp.broadcast_to(m_new, (SUBLANES, LANES))

    @pl.when(c == nc - 1)
    def _():
        c_out_ref[0] = c_st[...]
        n_out_ref[0] = n_st[...]
        m_out_ref[0] = m_st[...]


def _mlstm_chunk(u, v, o, gates, p, bsz, seq):
    nc = seq // CHUNK
    dm = u.shape[1]
    n_heads, dk, _ = p["w_mq"].shape
    tok = lambda b, c: (b * nc + c, 0)
    cst = lambda b, c: (0, 0)
    cst3 = lambda b, c: (0, 0, 0)
    return pl.pallas_call(
        functools.partial(_mlstm_chunk_kernel, nc=nc, n_heads=n_heads, dk=dk),
        grid=(bsz, nc),
        in_specs=[pl.BlockSpec((CHUNK, dm), tok),
                  pl.BlockSpec((CHUNK, dm), tok),
                  pl.BlockSpec((CHUNK, dm), tok),
                  pl.BlockSpec((CHUNK, 2 * LANES), tok),
                  pl.BlockSpec((CONV_K, dm), cst),
                  pl.BlockSpec((1, dm), cst),
                  pl.BlockSpec((n_heads, dk, dk), cst3),
                  pl.BlockSpec((n_heads, dk, dk), cst3),
                  pl.BlockSpec((1, LANES), cst),
                  pl.BlockSpec((1, LANES), cst),
                  pl.BlockSpec((1, dm), cst),
                  pl.BlockSpec((CHUNK, CHUNK), cst)],
        out_specs=[pl.BlockSpec((CHUNK, dm), tok),
                   pl.BlockSpec((1, n_heads, dk, dk), lambda b, c: (b, 0, 0, 0)),
                   pl.BlockSpec((1, n_heads, SUBLANES, dk), lambda b, c: (b, 0, 0, 0)),
                   pl.BlockSpec((1, SUBLANES, LANES), lambda b, c: (b, 0, 0))],
        out_shape=[jax.ShapeDtypeStruct((bsz * seq, dm), BF16),
                   jax.ShapeDtypeStruct((bsz, n_heads, dk, dk), F32),
                   jax.ShapeDtypeStruct((bsz, n_heads, SUBLANES, dk), F32),
                   jax.ShapeDtypeStruct((bsz, SUBLANES, LANES), F32)],
        scratch_shapes=[pltpu.VMEM((CHUNK + SUBLANES, dm), F32),
                        pltpu.VMEM((n_heads, dk, dk), F32),
                        pltpu.VMEM((n_heads, SUBLANES, dk), F32),
                        pltpu.VMEM((SUBLANES, LANES), F32)],
        compiler_params=_cparams("arbitrary", "arbitrary"),
        name="mlstm_chunk",
    )(u, v, o, gates, p["conv_m_w"], p["conv_m_b"], p["w_mq"], p["w_mk"], p["b_i_row"], p["b_f_row"],
      p["mlstm_norm_g"], p["tril"])


def _col_bcast(row, n_rows):
    return jnp.broadcast_to(row, (LANES, row.shape[1])).T[0:n_rows, :]


def _ssd_step_kernel(xbc_ref, z_ref, g_ref, buf_ref, st_ref, cw_ref, cb_ref, dtb_ref, alog_ref, dsk_ref, ng_ref,
                     exp_ref, y_ref, st_out_ref, *, d_ssd, n_heads):
    gw = d_ssd // SSD_GROUPS
    x_row = xbc_ref[0]
    conv = cb_ref[...] + cw_ref[CONV_K - 1:CONV_K, :] * x_row
    for j in range(CONV_K - 1):
        conv = conv + cw_ref[j:j + 1, :] * buf_ref[0, j:j + 1, :]
    xc = _silu(conv)
    xs = xc[:, 0:d_ssd]

    lane = lax.broadcasted_iota(jnp.int32, (1, LANES), 1)
    a_row = jnp.where(lane < n_heads, -jnp.exp(alog_ref[...]), 0.0)
    dt = _softplus(g_ref[0][:, 0:LANES] + dtb_ref[...])
    dec = jnp.exp(dt * a_row)
    both = _dot_sel_r(jnp.concatenate([jnp.broadcast_to(dt, (SUBLANES, LANES)),
                                       jnp.broadcast_to(dec, (SUBLANES, LANES))], axis=0), exp_ref[...])
    dt_full = both[0:1]
    dec_full = both[SUBLANES:SUBLANES + 1]
    dtx = dt_full * xs
    dtx_col = _col_bcast(dtx, d_ssd)
    dec_col = _col_bcast(dec_full, d_ssd)

    y_parts = []
    for g in range(SSD_GROUPS):
        b_g = xc[:, d_ssd + g * SSD_STATE:d_ssd + (g + 1) * SSD_STATE]
        c_g = xc[:, d_ssd + (SSD_GROUPS + g) * SSD_STATE:d_ssd + (SSD_GROUPS + g + 1) * SSD_STATE]
        h_g = st_ref[0, g * gw:(g + 1) * gw, :]
        cb = jnp.sum(c_g * b_g, axis=-1, keepdims=True)
        c8 = jnp.broadcast_to(c_g, (SUBLANES, SSD_STATE)).astype(BF16)
        y_int = _dot_nt(c8, h_g.astype(BF16))[0:1]
        sl = slice(g * gw, (g + 1) * gw)
        y_parts.append(cb * dtx[:, sl] + dec_full[:, sl] * y_int)
        st_out_ref[0, sl, :] = h_g * dec_col[sl, :] + dtx_col[sl, :] * b_g
    y = jnp.concatenate(y_parts, axis=1) + dsk_ref[...] * xs
    y = y * _silu(z_ref[0])
    y = y * lax.rsqrt(jnp.mean(y * y, axis=-1, keepdims=True) + LN_EPS) * ng_ref[...]
    y_ref[0] = y.astype(y_ref.dtype)


def _ssd_step(xbc, z, gates, buf, state, p):
    nb, cd = xbc.shape
    d_ssd = z.shape[1]
    n_heads = d_ssd // SSD_HEAD_DIM
    row3 = lambda b: (b, 0, 0)
    cst = lambda b: (0, 0)
    st2 = state.reshape(nb, d_ssd, SSD_STATE)
    y, st_new = pl.pallas_call(
        functools.partial(_ssd_step_kernel, d_ssd=d_ssd, n_heads=n_heads),
        grid=(nb,),
        in_specs=[pl.BlockSpec((1, 1, cd), row3),
                  pl.BlockSpec((1, 1, d_ssd), row3),
                  pl.BlockSpec((1, 1, 2 * LANES), row3),
                  pl.BlockSpec((1, CONV_K - 1, cd), row3),
                  pl.BlockSpec((1, d_ssd, SSD_STATE), row3),
                  pl.BlockSpec((CONV_K, cd), cst),
                  pl.BlockSpec((1, cd), cst),
                  pl.BlockSpec((1, LANES), cst),
                  pl.BlockSpec((1, LANES), cst),
                  pl.BlockSpec((1, d_ssd), cst),
                  pl.BlockSpec((1, d_ssd), cst),
                  pl.BlockSpec((LANES, d_ssd), cst)],
        out_specs=[pl.BlockSpec((1, 1, d_ssd), row3),
                   pl.BlockSpec((1, d_ssd, SSD_STATE), row3)],
        out_shape=[jax.ShapeDtypeStruct((nb, 1, d_ssd), BF16),
                   jax.ShapeDtypeStruct((nb, d_ssd, SSD_STATE), F32)],
        compiler_params=_cparams("arbitrary"),
        name="ssd_step",
    )(xbc.reshape(nb, 1, cd), z.reshape(nb, 1, d_ssd), gates.reshape(nb, 1, 2 * LANES), buf, st2,
      p["conv_ssd_w"], p["conv_ssd_b"], p["dt_bias_row"], p["a_log_row"], p["d_skip_row"], p["ssd_norm_g"],
      p["ssd_expand"])
    return y.reshape(nb, d_ssd), st_new.reshape(state.shape)


def _mlstm_step_kernel(u_ref, v_ref, o_ref, g_ref, buf_ref, c_ref, n_ref, m_ref, cw_ref, cb_ref, wq_ref, wk_ref,
                       bi_ref, bf_ref, ng_ref, hexp_ref, y_ref, c_out_ref, n_out_ref, m_out_ref, *, n_heads, dk):
    conv = cb_ref[...] + cw_ref[CONV_K - 1:CONV_K, :] * u_ref[0]
    for j in range(CONV_K - 1):
        conv = conv + cw_ref[j:j + 1, :] * buf_ref[0, j:j + 1, :]
    uc = jnp.broadcast_to(_silu(conv), (SUBLANES, conv.shape[1])).astype(BF16)

    gt = g_ref[0]
    logi = gt[:, 0:LANES] + bi_ref[...]
    logf = _log_sigmoid(gt[:, LANES:2 * LANES] + bf_ref[...])
    m_prev = m_ref[0]
    inter = logf + m_prev
    m_t = jnp.maximum(inter, logi)
    w_inter = jnp.exp(inter - m_t)
    w_in = jnp.exp(logi - m_t)
    floor = jnp.exp(-m_t)
    rows = jnp.concatenate([w_inter, w_in, floor, jnp.zeros((SUBLANES - 3, LANES), F32)], axis=0)
    full = _dot_sel_r(rows, hexp_ref[...])
    scale = dk ** -0.5

    for h in range(n_heads):
        sl = slice(h * dk, (h + 1) * dk)
        q = _dot(uc[:, sl], wq_ref[h])[0:1]
        k = _dot(uc[:, sl], wk_ref[h])[0:1] * scale
        v_h = v_ref[0][:, sl]
        wi = full[0:1, sl]
        ws = full[1:2, sl]
        fl = full[2:3, sl]
        c_h = c_ref[0, h]
        n_h = n_ref[0, h:h + 1, :]
        att = jnp.sum(q * k, axis=-1, keepdims=True) * ws
        q8 = jnp.broadcast_to(q, (SUBLANES, dk)).astype(BF16)
        num = att * v_h + wi * _dot(q8, c_h.astype(BF16))[0:1]
        den = att + wi * jnp.sum(q * n_h, axis=-1, keepdims=True)
        hout = num / jnp.maximum(jnp.abs(den), fl)
        mu = jnp.mean(hout, axis=-1, keepdims=True)
        var = jnp.mean(jnp.square(hout - mu), axis=-1, keepdims=True)
        hn = (hout - mu) * lax.rsqrt(var + LN_EPS) * ng_ref[:, sl]
        y_ref[0, :, sl] = (_sigmoid(o_ref[0][:, sl]) * hn).astype(y_ref.dtype)
        kw = k * ws
        kw_col = _col_bcast(kw, dk)
        wi_full = jnp.broadcast_to(wi, (dk, dk))
        for half in range(dk // LANES):
            hs = slice(half * LANES, (half + 1) * LANES)
            c_out_ref[0, h, :, hs] = wi_full[:, hs] * c_h[:, hs] + kw_col * v_h[:, hs]
        n_out_ref[0, h:h + 1, :] = wi * n_h + kw
    m_out_ref[0] = m_t


def _mlstm_step(u, v, o, gates, buf, c_st, n_st, m_row, p):
    nb, dm = u.shape
    n_heads, dk, _ = p["w_mq"].shape
    row3 = lambda b: (b, 0, 0)
    cst = lambda b: (0, 0)
    cst3 = lambda b: (0, 0, 0)
    r3 = lambda a: a.reshape(nb, 1, a.shape[-1])
    y, c_new, n_new, m_new = pl.pallas_call(
        functools.partial(_mlstm_step_kernel, n_heads=n_heads, dk=dk),
        grid=(nb,),
        in_specs=[pl.BlockSpec((1, 1, dm), row3),
                  pl.BlockSpec((1, 1, dm), row3),
                  pl.BlockSpec((1, 1, dm), row3),
                  pl.BlockSpec((1, 1, 2 * LANES), row3),
                  pl.BlockSpec((1, CONV_K - 1, dm), row3),
                  pl.BlockSpec((1, n_heads, dk, dk), lambda b: (b, 0, 0, 0)),
                  pl.BlockSpec((1, n_heads, dk), row3),
                  pl.BlockSpec((1, 1, LANES), row3),
                  pl.BlockSpec((CONV_K, dm), cst),
                  pl.BlockSpec((1, dm), cst),
                  pl.BlockSpec((n_heads, dk, dk), cst3),
                  pl.BlockSpec((n_heads, dk, dk), cst3),
                  pl.BlockSpec((1, LANES), cst),
                  pl.BlockSpec((1, LANES), cst),
                  pl.BlockSpec((1, dm), cst),
                  pl.BlockSpec((LANES, dm), cst)],
        out_specs=[pl.BlockSpec((1, 1, dm), row3),
                   pl.BlockSpec((1, n_heads, dk, dk), lambda b: (b, 0, 0, 0)),
                   pl.BlockSpec((1, n_heads, dk), row3),
                   pl.BlockSpec((1, 1, LANES), row3)],
        out_shape=[jax.ShapeDtypeStruct((nb, 1, dm), BF16),
                   jax.ShapeDtypeStruct(c_st.shape, F32),
                   jax.ShapeDtypeStruct(n_st.shape, F32),
                   jax.ShapeDtypeStruct((nb, 1, LANES), F32)],
        compiler_params=_cparams("arbitrary"),
        name="mlstm_step",
    )(r3(u), r3(v), r3(o), r3(gates), buf, c_st, n_st, m_row, p["conv_m_w"], p["conv_m_b"], p["w_mq"],
      p["w_mk"], p["b_i_row"], p["b_f_row"], p["mlstm_norm_g"], p["mlstm_expand"])
    return y.reshape(nb, dm), c_new, n_new, m_new


def _outproj_kernel(ys_ref, hm_ref, x_ref, w_ref, g_ref, b_ref, x1t_ref, *, alpha):
    d_ssd = ys_ref.shape[1]
    mix = _dot(ys_ref[...], w_ref[0:d_ssd, :]) + _dot(hm_ref[...], w_ref[d_ssd:, :])
    r = alpha * x_ref[...] + mix
    mu = jnp.mean(r, axis=-1, keepdims=True)
    var = jnp.mean(jnp.square(r - mu), axis=-1, keepdims=True)
    x1 = (r - mu) * lax.rsqrt(var + LN_EPS) * g_ref[...] + b_ref[...]
    x1t_ref[...] = x1.T


def _outproj(ys, hm, x2, p, alpha, tm):
    m, d = x2.shape
    d_ssd, d_m = ys.shape[1], hm.shape[1]
    return pl.pallas_call(
        functools.partial(_outproj_kernel, alpha=alpha),
        grid=(m // tm,),
        in_specs=[pl.BlockSpec((tm, d_ssd), lambda i: (i, 0)),
                  pl.BlockSpec((tm, d_m), lambda i: (i, 0)),
                  pl.BlockSpec((tm, d), lambda i: (i, 0)),
                  pl.BlockSpec((d_ssd + d_m, d), lambda i: (0, 0)),
                  pl.BlockSpec((1, d), lambda i: (0, 0)),
                  pl.BlockSpec((1, d), lambda i: (0, 0))],
        out_specs=pl.BlockSpec((d, tm), lambda i: (0, i)),
        out_shape=jax.ShapeDtypeStruct((d, m), F32),
        compiler_params=_cparams("arbitrary"),
        name="outproj_ln1",
    )(ys, hm, x2, p["w_out"], p["ln1_g"], p["ln1_b"])


def _oddeven_merge_sort_pairs(n):
    pairs = []
    p = 1
    while p < n:
        k = p
        while k >= 1:
            for j in range(k % p, n - k, 2 * k):
                for i in range(min(k, n - j - k)):
                    if (i + j) // (2 * p) == (i + j + k) // (2 * p):
                        pairs.append((i + j, i + j + k))
            k //= 2
        p *= 2
    return pairs


_SORT_TOPK = _oddeven_merge_sort_pairs(PEER_TOPK)


def _compare_exchange(lst, i, j):
    hi, lo = jnp.maximum(lst[i], lst[j]), jnp.minimum(lst[i], lst[j])
    lst[i], lst[j] = hi, lo


def _bitonic_to_sorted(t):
    d = PEER_TOPK // 2
    while d >= 1:
        for i in range(PEER_TOPK):
            if i & d == 0:
                _compare_exchange(t, i, i + d)
        d //= 2
    return t


def _merge_top(x, y):
    neg = jnp.full_like(x[0], -jnp.inf)
    x = x + [neg] * (PEER_TOPK - len(x))
    y = y + [neg] * (PEER_TOPK - len(y))
    return _bitonic_to_sorted([jnp.maximum(x[k], y[PEER_TOPK - 1 - k]) for k in range(PEER_TOPK)])


def _top_rows(s):
    lst = [s[SUBLANES * g:SUBLANES * (g + 1), :] for g in range(PEER_TOPK)]
    for i, j in _SORT_TOPK:
        _compare_exchange(lst, i, j)
    shift = SUBLANES // 2
    while shift >= 1:
        lst = _merge_top(lst, [pltpu.roll(a, shift, 0) for a in lst])
        shift //= 2
    return lst


def _peer_topk_kernel(x1t_ref, wqt_ref, keys_ref, s1_ref, s2_ref, st_ref, q_scr, a_scr, b_scr, *, n_heads):
    tq = x1t_ref.shape[1]
    q_scr[...] = _dot(wqt_ref[...], x1t_ref[...].astype(BF16))
    half = q_scr.shape[0] // (2 * n_heads)

    for h in range(n_heads):
        for side, (s_ref, top_scr) in enumerate(((s1_ref, a_scr), (s2_ref, b_scr))):
            r0 = (2 * h + side) * half
            s = _dot(keys_ref[h, side], q_scr[r0:r0 + half, :].astype(BF16))
            for t in range(tq // LANES):
                ls = slice(t * LANES, (t + 1) * LANES)
                s_ref[h, t] = s[:, ls]
                top = _top_rows(s[:, ls])
                for k in range(PEER_TOPK):
                    top_scr[k, h:h + 1, ls] = top[k][0:1, :]

    for t in range(tq // LANES):
        ls = slice(t * LANES, (t + 1) * LANES)
        a = [a_scr[k, :, ls] for k in range(PEER_TOPK)]
        b = [b_scr[k, :, ls] for k in range(PEER_TOPK)]
        single = PEER_TOPK // 2
        lists = [[a[i] + b[j] for j in range(PEER_TOPK // (i + 1))] for i in range(single)]
        lists.append([a[i] + b[0] for i in range(single, PEER_TOPK)])
        merged = lists[0]
        for other in lists[1:]:
            merged = _merge_top(merged, other)
        tau = merged[PEER_TOPK - 1]
        top = a[0] + b[0]
        zsum = jnp.zeros_like(tau)
        for cand in (c for lst in lists for c in lst):
            zsum = zsum + jnp.where(cand >= tau, jnp.exp(cand - top), 0.0)
        st_ref[0, :, ls] = tau
        st_ref[1, :, ls] = a[0]
        st_ref[2, :, ls] = b[0]
        st_ref[3, :, ls] = 1.0 / zsum


def _peer_topk(x1t, p, tq):
    d, m = x1t.shape
    n_heads, _, n_keys, half = p["peer_keys"].shape
    nq = p["peer_wqT"].shape[0]
    assert n_heads == SUBLANES and n_keys == PEER_TOPK * SUBLANES
    keyb = lambda i: (0, i, 0, 0)
    return pl.pallas_call(
        functools.partial(_peer_topk_kernel, n_heads=n_heads),
        grid=(m // tq,),
        in_specs=[pl.BlockSpec((d, tq), lambda i: (0, i)),
                  pl.BlockSpec((nq, d), lambda i: (0, 0)),
                  pl.BlockSpec((n_heads, 2, n_keys, half), lambda i: (0, 0, 0, 0))],
        out_specs=[pl.BlockSpec((n_heads, tq // LANES, n_keys, LANES), keyb),
                   pl.BlockSpec((n_heads, tq // LANES, n_keys, LANES), keyb),
                   pl.BlockSpec((N_STATS, SUBLANES, tq), lambda i: (0, 0, i))],
        out_shape=[jax.ShapeDtypeStruct((n_heads, m // LANES, n_keys, LANES), F32),
                   jax.ShapeDtypeStruct((n_heads, m // LANES, n_keys, LANES), F32),
                   jax.ShapeDtypeStruct((N_STATS, SUBLANES, m), F32)],
        scratch_shapes=[pltpu.VMEM((nq, tq), F32),
                        pltpu.VMEM((PEER_TOPK, SUBLANES, tq), F32),
                        pltpu.VMEM((PEER_TOPK, SUBLANES, tq), F32)],
        compiler_params=_cparams("arbitrary"),
        name="peer_topk",
    )(x1t, p["peer_wqT"], p["peer_keys"])


def _peer_dense_kernel(x1t_ref, s1_ref, s2_ref, st_ref, u_ref, vt_ref, g_ref, b_ref, out_ref,
                       xb_scr, e2_scr, c_scr, acc_scr, s_scr, w_scr, s1c_scr, cc_scr, *, alpha, n_heads, nj):
    j = pl.program_id(1)
    ec = s_scr.shape[0]
    tb = x1t_ref.shape[1]
    nt = tb // LANES
    n_keys = s1_ref.shape[2]
    na = ec // n_keys

    @pl.when(j == 0)
    def _():
        xb_scr[...] = x1t_ref[...].astype(BF16)
        for h in range(n_heads):
            for t in range(nt):
                ls = slice(t * LANES, (t + 1) * LANES)
                e2_scr[h, t] = jnp.exp(s2_ref[h, t] - st_ref[2, h:h + 1, ls])
                c_scr[h, t] = jnp.exp(s1_ref[h, t] - st_ref[1, h:h + 1, ls]) * st_ref[3, h:h + 1, ls]
        acc_scr[...] = jnp.zeros_like(acc_scr)

    s_scr[:, 0:tb] = _dot(u_ref[...], xb_scr[...])

    a0 = pl.multiple_of(j * na, na)
    for h in range(n_heads):
        for t in range(nt):
            ls = slice(t * LANES, (t + 1) * LANES)
            s1_blk = s1_ref[h, t, pl.ds(a0, na), :]
            c_blk = c_scr[h, t, pl.ds(a0, na), :]
            for al in range(na):
                s1c_scr[h * na + al, :, ls] = s1_blk[al:al + 1, :]
                cc_scr[h * na + al, :, ls] = c_blk[al:al + 1, :]

    def first_key(al, carry):
        r0 = pl.multiple_of(al * n_keys, n_keys)
        for t in range(nt):
            ls = slice(t * LANES, (t + 1) * LANES)
            gate = jnp.zeros((n_keys, LANES), F32)
            for h in range(n_heads):
                ssum = s2_ref[h, t] + s1c_scr[h * na + al][:, ls]
                gate = gate + jnp.where(ssum >= st_ref[0, h:h + 1, ls],
                                        e2_scr[h, t] * cc_scr[h * na + al][:, ls], 0.0)
            hid = _gelu_exact(s_scr[pl.ds(r0, n_keys), ls])
            w_scr[pl.ds(r0, n_keys), ls] = (gate * hid).astype(BF16)
        return carry

    lax.fori_loop(0, na, first_key, 0)
    acc_scr[...] += _dot(vt_ref[...], w_scr[:, 0:tb])

    @pl.when(j == nj - 1)
    def _():
        r = alpha * x1t_ref[...] + acc_scr[...]
        mu = jnp.mean(r, axis=0, keepdims=True)
        var = jnp.mean(jnp.square(r - mu), axis=0, keepdims=True)
        y = (r - mu) * lax.rsqrt(var + LN_EPS) * g_ref[...] + b_ref[...]
        out_ref[...] = y.T


def _peer_dense(x1t, s1, s2, st, p, alpha, tb, ec):
    d, m = x1t.shape
    n_heads, _, n_keys, _ = s1.shape
    n_exp = p["peer_u"].shape[0]
    nj = n_exp // ec
    nt = tb // LANES
    keyb = lambda i, j: (0, i, 0, 0)
    pitch = tb + LANES
    return pl.pallas_call(
        functools.partial(_peer_dense_kernel, alpha=alpha, n_heads=n_heads, nj=nj),
        grid=(m // tb, nj),
        in_specs=[pl.BlockSpec((d, tb), lambda i, j: (0, i)),
                  pl.BlockSpec((n_heads, nt, n_keys, LANES), keyb),
                  pl.BlockSpec((n_heads, nt, n_keys, LANES), keyb),
                  pl.BlockSpec((N_STATS, SUBLANES, tb), lambda i, j: (0, 0, i)),
                  pl.BlockSpec((ec, d), lambda i, j: (j, 0)),
                  pl.BlockSpec((d, ec), lambda i, j: (0, j)),
                  pl.BlockSpec((d, 1), lambda i, j: (0, 0)),
                  pl.BlockSpec((d, 1), lambda i, j: (0, 0))],
        out_specs=pl.BlockSpec((tb, d), lambda i, j: (i, 0)),
        out_shape=jax.ShapeDtypeStruct((m, d), F32),
        scratch_shapes=[pltpu.VMEM((d, tb), BF16),
                        pltpu.VMEM((n_heads, nt, n_keys, LANES), F32),
                        pltpu.VMEM((n_heads, nt, n_keys, LANES), F32),
                        pltpu.VMEM((d, tb), F32),
                        pltpu.VMEM((ec, pitch), F32),
                        pltpu.VMEM((ec, pitch), BF16),
                        pltpu.VMEM((n_heads * (ec // n_keys), 1, tb), F32),
                        pltpu.VMEM((n_heads * (ec // n_keys), 1, tb), F32)],
        compiler_params=_cparams("arbitrary", "arbitrary"),
        name="peer_dense",
    )(x1t, s1, s2, st, p["peer_u"], p["peer_vT"], p["ln2_g_col"], p["ln2_b_col"])


def _pad_row(vec, offset):
    return jnp.zeros((1, LANES), F32).at[0, offset:offset + vec.shape[0]].set(vec.astype(F32))


def _prep_layer(l, w_in, conv_ssd_w, conv_ssd_b, dt_bias, a_log, d_skip, ssd_norm_g, conv_m_w, conv_m_b,
                w_mq, w_mk, b_igate, b_fgate, mlstm_norm_g, w_out, ln1_g, ln1_b, peer_wq, peer_keys,
                peer_u, peer_v, ln2_g, ln2_b):
    d_model = w_in.shape[1]
    d_ssd = ssd_norm_g.shape[1]
    cd = conv_ssd_w.shape[2]
    n_sh = dt_bias.shape[1]
    d_m = conv_m_w.shape[2]
    n_mh = w_mq.shape[1]
    dk = w_mq.shape[2]
    assert n_sh <= GATE_I0 and GATE_I0 + n_mh <= LANES and d_ssd == n_sh * SSD_HEAD_DIM
    w = w_in[l]
    offs = [0]
    for n in (d_ssd, cd, n_sh, d_m, d_m, d_m, n_mh, n_mh):
        offs.append(offs[-1] + n)
    z_w, xbc_w, dt_w, u_w, v_w, o_w, i_w, f_w = (w[:, offs[k]:offs[k + 1]] for k in range(8))
    gate_a = jnp.zeros((d_model, LANES), F32).at[:, 0:n_sh].set(dt_w).at[:, GATE_I0:GATE_I0 + n_mh].set(i_w)
    gate_b = jnp.zeros((d_model, LANES), F32).at[:, GATE_I0:GATE_I0 + n_mh].set(f_w)
    w_cat = jnp.concatenate([z_w, xbc_w, u_w, v_w, o_w, gate_a, gate_b], axis=1).astype(BF16)
    widths = (d_ssd, cd, d_m, d_m, d_m, 2 * LANES)

    lane = jnp.arange(LANES)[:, None]
    ssd_expand = (lane == (jnp.arange(d_ssd)[None, :] // SSD_HEAD_DIM)).astype(BF16)
    mlstm_expand = (lane == (GATE_I0 + jnp.arange(d_m)[None, :] // dk)).astype(BF16)
    t = jnp.arange(CHUNK)
    tril = (t[:, None] >= t[None, :]).astype(BF16)
    p = {
        "w_cat": w_cat,
        "conv_ssd_w": conv_ssd_w[l], "conv_ssd_b": conv_ssd_b[l][None, :],
        "dt_bias_row": _pad_row(dt_bias[l], 0), "a_log_row": _pad_row(a_log[l], 0),
        "d_skip_row": jnp.repeat(d_skip[l], SSD_HEAD_DIM)[None, :], "ssd_norm_g": ssd_norm_g[l][None, :],
        "conv_m_w": conv_m_w[l], "conv_m_b": conv_m_b[l][None, :],
        "w_mq": w_mq[l].astype(BF16), "w_mk": w_mk[l].astype(BF16),
        "b_i_row": _pad_row(b_igate[l], GATE_I0), "b_f_row": _pad_row(b_fgate[l], GATE_I0),
        "mlstm_norm_g": mlstm_norm_g[l].reshape(1, d_m),
        "w_out": w_out[l].astype(BF16), "ln1_g": ln1_g[l][None, :], "ln1_b": ln1_b[l][None, :],
        "peer_wqT": peer_wq[l].T.astype(BF16), "peer_keys": peer_keys[l].astype(BF16),
        "peer_u": peer_u[l].astype(BF16), "peer_vT": peer_v[l].T.astype(BF16),
        "ln2_g_col": ln2_g[l][:, None], "ln2_b_col": ln2_b[l][:, None],
        "tril": tril, "ssd_expand": ssd_expand, "mlstm_expand": mlstm_expand,
    }
    return p, widths


def _tile_sizes(m):
    tm = 256 if m % 256 == 0 else LANES
    tb = 512 if m % 512 == 0 else LANES
    return tm, tb


def _peer_block(x1t, p, alpha):
    m = x1t.shape[1]
    tm, tb = _tile_sizes(m)
    s1, s2, st = _peer_topk(x1t, p, tm)
    return _peer_dense(x1t, s1, s2, st, p, alpha, tb, ec=1024)


def _layer_prompt(x, p, widths, alpha):
    bsz, seq, d = x.shape
    assert seq % CHUNK == 0 and seq >= CONV_K - 1
    x2 = x.reshape(bsz * seq, d)
    tm, _ = _tile_sizes(bsz * seq)
    z, xbc, u, v, o, gates = _inproj(x2, p["w_cat"], widths, tm)
    y_s, h_t = _ssd_chunk(xbc, z, gates, p, bsz, seq)
    h_m, c_new, n_new, m_new = _mlstm_chunk(u, v, o, gates, p, bsz, seq)
    x1t = _outproj(y_s, h_m, x2, p, alpha, tm)
    y = _peer_block(x1t, p, alpha).reshape(bsz, seq, d)
    n_mh = c_new.shape[1]
    d_ssd = z.shape[1]
    n_sh = d_ssd // SSD_HEAD_DIM
    hg = n_sh // SSD_GROUPS
    new_ssd = h_t.reshape(bsz, SSD_GROUPS, SSD_STATE, hg, SSD_HEAD_DIM).transpose(0, 1, 3, 4, 2)
    new_ssd = new_ssd.reshape(bsz, n_sh, SSD_HEAD_DIM, SSD_STATE)
    states = (new_ssd,
              xbc.reshape(bsz, seq, -1)[:, seq - (CONV_K - 1):, :],
              c_new,
              n_new[:, :, 0, :],
              m_new[:, 0, GATE_I0:GATE_I0 + n_mh],
              u.reshape(bsz, seq, -1)[:, seq - (CONV_K - 1):, :])
    return y, states


def _layer_sample(x, st, p, widths, alpha):
    st_ssd, st_ssd_conv, st_c, st_n, st_m, st_mconv = st
    nb, seq, d = x.shape
    assert seq == 1 and nb % LANES == 0
    x2 = x.reshape(nb, d)
    z, xbc, u, v, o, gates = _inproj(x2, p["w_cat"], widths, LANES)
    y_s, new_ssd = _ssd_step(xbc, z, gates, st_ssd_conv, st_ssd, p)
    n_mh = st_c.shape[1]
    m_row = jnp.zeros((nb, 1, LANES), F32).at[:, 0, GATE_I0:GATE_I0 + n_mh].set(st_m)
    h_m, c_new, n_new, m_new = _mlstm_step(u, v, o, gates, st_mconv, st_c, st_n, m_row, p)
    x1t = _outproj(y_s, h_m, x2, p, alpha, LANES)
    y = _peer_block(x1t, p, alpha).reshape(nb, seq, d)
    states = (new_ssd,
              jnp.concatenate([st_ssd_conv[:, 1:, :], xbc[:, None, :]], axis=1),
              c_new,
              n_new,
              m_new[:, 0, GATE_I0:GATE_I0 + n_mh],
              jnp.concatenate([st_mconv[:, 1:, :], u[:, None, :]], axis=1))
    return y, states


def kernel(x_prompt, x_sample, state_ssd, state_ssd_conv, state_mlstm_c, state_mlstm_n, state_mlstm_m,
           state_mlstm_conv, w_in, conv_ssd_w, conv_ssd_b, dt_bias, a_log, d_skip, ssd_norm_g, conv_m_w,
           conv_m_b, w_mq, w_mk, b_igate, b_fgate, mlstm_norm_g, w_out, ln1_g, ln1_b, peer_wq, peer_keys,
           peer_u, peer_v, ln2_g, ln2_b):
    depth = w_in.shape[0]
    alpha = (2.0 * depth) ** 0.25
    states = (state_ssd, state_ssd_conv, state_mlstm_c, state_mlstm_n, state_mlstm_m, state_mlstm_conv)
    yp, ys = x_prompt, x_sample
    p_list, s_list = [], []
    for l in range(depth):
        p, widths = _prep_layer(l, w_in, conv_ssd_w, conv_ssd_b, dt_bias, a_log, d_skip, ssd_norm_g, conv_m_w,
                                conv_m_b, w_mq, w_mk, b_igate, b_fgate, mlstm_norm_g, w_out, ln1_g, ln1_b,
                                peer_wq, peer_keys, peer_u, peer_v, ln2_g, ln2_b)
        yp, p_new = _layer_prompt(yp, p, widths, alpha)
        ys, s_new = _layer_sample(ys, tuple(s[l] for s in states), p, widths, alpha)
        p_list.append(p_new)
        s_list.append(s_new)
    p_out = tuple(jnp.stack([pn[i] for pn in p_list]) for i in range(6))
    s_out = tuple(jnp.stack([sn[i] for sn in s_list]) for i in range(6))
    return (yp, ys) + p_out + s_out
```

```python
import functools
import math

import jax
import jax.numpy as jnp
from jax import lax
from jax.experimental import pallas as pl
from jax.experimental.pallas import tpu as pltpu

F32 = jnp.float32
BF16 = jnp.bfloat16

LANES = 128
SUBLANES = 8
CHUNK = 128
CONV_K = 4
LN_EPS = 1e-5
VMEM_LIMIT = 52 * 1024 * 1024

SSD_HEAD_DIM = 64
SSD_STATE = 128
SSD_GROUPS = 2
PEER_TOPK = 16
GATE_I0 = 16
N_STATS = 4
PEER_CHUNK = 512


def _cparams(*sem):
    return pltpu.CompilerParams(dimension_semantics=sem, vmem_limit_bytes=VMEM_LIMIT)


def _dot(a, b):
    return jnp.dot(a, b, preferred_element_type=F32)


def _dot_nt(a, b):
    return lax.dot_general(a, b, (((1,), (1,)), ((), ())), preferred_element_type=F32)


def _split3(x):
    h = x.astype(BF16)
    r = x - h.astype(F32)
    m = r.astype(BF16)
    lo = (r - m.astype(F32)).astype(BF16)
    return h, m, lo


def _dot_sel_l(sel_bf16, x):
    h, m, lo = _split3(x)
    return _dot(sel_bf16, h) + _dot(sel_bf16, m) + _dot(sel_bf16, lo)


def _dot_sel_r(x, sel_bf16):
    h, m, lo = _split3(x)
    return _dot(h, sel_bf16) + _dot(m, sel_bf16) + _dot(lo, sel_bf16)


def _sigmoid(x):
    return 1.0 / (1.0 + jnp.exp(-x))


def _silu(x):
    return x * _sigmoid(x)


def _softplus(x):
    return jnp.maximum(x, 0.0) + jnp.log1p(jnp.exp(-jnp.abs(x)))


def _log_sigmoid(x):
    return -_softplus(-x)


def _gelu_exact(x):
    return 0.5 * x * (1.0 + lax.erf(x * (1.0 / math.sqrt(2.0))))


def _inproj_kernel(x_ref, w_ref, *out_refs):
    xb = x_ref[...].astype(BF16)
    off = 0
    for ref in out_refs:
        n = ref.shape[-1]
        ref[...] = _dot(xb, w_ref[:, off:off + n])
        off += n


def _inproj(x2, w_cat, widths, tm):
    m, d = x2.shape
    n_all = w_cat.shape[1]
    return pl.pallas_call(
        _inproj_kernel,
        grid=(m // tm,),
        in_specs=[pl.BlockSpec((tm, d), lambda i: (i, 0)),
                  pl.BlockSpec((d, n_all), lambda i: (0, 0))],
        out_specs=[pl.BlockSpec((tm, n), lambda i: (i, 0)) for n in widths],
        out_shape=[jax.ShapeDtypeStruct((m, n), F32) for n in widths],
        compiler_params=_cparams("arbitrary"),
        name="inproj",
    )(x2, w_cat)


def _ssd_chunk_kernel(xbc_ref, z_ref, g_ref, cw_ref, cb_ref, dtb_ref, alog_ref, dsk_ref, ng_ref,
                      tril_ref, exp_ref, y_ref, hT_out_ref, xpad, hT, ybuf, *, nc, d_ssd, n_heads):
    c = pl.program_id(1)
    cd = xpad.shape[1]
    hg = n_heads // SSD_GROUPS
    gw = d_ssd // SSD_GROUPS

    @pl.when(c == 0)
    def _():
        xpad[0:SUBLANES, :] = jnp.zeros((SUBLANES, cd), F32)
        hT[...] = jnp.zeros_like(hT)

    xpad[SUBLANES:SUBLANES + CHUNK, :] = xbc_ref[...]
    conv = cb_ref[...]
    for j in range(CONV_K):
        r0 = SUBLANES - (CONV_K - 1) + j
        conv = conv + cw_ref[j:j + 1, :] * xpad[r0:r0 + CHUNK, :]
    xpad[0:SUBLANES, :] = xpad[CHUNK:CHUNK + SUBLANES, :]
    xc = _silu(conv)
    xs = xc[:, 0:d_ssd]

    lane = lax.broadcasted_iota(jnp.int32, (1, LANES), 1)
    a_row = jnp.where(lane < n_heads, -jnp.exp(alog_ref[...]), 0.0)
    dt = _softplus(g_ref[:, 0:LANES] + dtb_ref[...])
    acs = _dot_sel_l(tril_ref[...], dt * a_row)
    dt_t = dt.T
    acs_t = acs.T
    eacs = jnp.exp(acs)
    dec = jnp.exp(acs[CHUNK - 1:CHUNK, :] - acs) * dt
    both = _dot_sel_r(jnp.concatenate([eacs, dec], axis=0), exp_ref[...])
    eacs_full = both[0:CHUNK]
    dec_full = both[CHUNK:2 * CHUNK]

    row = lax.broadcasted_iota(jnp.int32, (CHUNK, CHUNK), 0)
    col = lax.broadcasted_iota(jnp.int32, (CHUNK, CHUNK), 1)
    causal = row >= col

    for g in range(SSD_GROUPS):
        b_g = xc[:, d_ssd + g * SSD_STATE:d_ssd + (g + 1) * SSD_STATE]
        c_g = xc[:, d_ssd + (SSD_GROUPS + g) * SSD_STATE:d_ssd + (SSD_GROUPS + g + 1) * SSD_STATE]
        b_gb = b_g.astype(BF16)
        c_gb = c_g.astype(BF16)
        cb = _dot_nt(c_gb, b_gb)
        h_g = hT[g]
        y_inter = _dot(c_gb, h_g.astype(BF16)) * eacs_full[:, g * gw:(g + 1) * gw]
        for hl in range(hg):
            h = g * hg + hl
            diff = acs[:, h:h + 1] - acs_t[h:h + 1, :]
            lmat = jnp.exp(jnp.where(causal, diff, -jnp.inf))
            mm = cb * lmat * dt_t[h:h + 1, :]
            x_h = xs[:, h * SSD_HEAD_DIM:(h + 1) * SSD_HEAD_DIM]
            ybuf[:, h * SSD_HEAD_DIM:(h + 1) * SSD_HEAD_DIM] = _dot(mm.astype(BF16), x_h.astype(BF16))
        xd = xs[:, g * gw:(g + 1) * gw] * dec_full[:, g * gw:(g + 1) * gw]
        hT[g] = h_g * eacs_full[CHUNK - 1:CHUNK, g * gw:(g + 1) * gw] + _dot(b_g.T.astype(BF16), xd.astype(BF16))
        ybuf[:, g * gw:(g + 1) * gw] = ybuf[:, g * gw:(g + 1) * gw] + y_inter

    y = ybuf[...] + dsk_ref[...] * xs
    y = y * _silu(z_ref[...])
    y = y * lax.rsqrt(jnp.mean(y * y, axis=-1, keepdims=True) + LN_EPS) * ng_ref[...]
    y_ref[...] = y.astype(y_ref.dtype)

    @pl.when(c == nc - 1)
    def _():
        hT_out_ref[0] = hT[...]


def _ssd_chunk(xbc, z, gates, p, bsz, seq):
    nc = seq // CHUNK
    cd = xbc.shape[1]
    d_ssd = z.shape[1]
    n_heads = d_ssd // SSD_HEAD_DIM
    gw = d_ssd // SSD_GROUPS
    tok = lambda b, c: (b * nc + c, 0)
    cst = lambda b, c: (0, 0)
    return pl.pallas_call(
        functools.partial(_ssd_chunk_kernel, nc=nc, d_ssd=d_ssd, n_heads=n_heads),
        grid=(bsz, nc),
        in_specs=[pl.BlockSpec((CHUNK, cd), tok),
                  pl.BlockSpec((CHUNK, d_ssd), tok),
                  pl.BlockSpec((CHUNK, 2 * LANES), tok),
                  pl.BlockSpec((CONV_K, cd), cst),
                  pl.BlockSpec((1, cd), cst),
                  pl.BlockSpec((1, LANES), cst),
                  pl.BlockSpec((1, LANES), cst),
                  pl.BlockSpec((1, d_ssd), cst),
                  pl.BlockSpec((1, d_ssd), cst),
                  pl.BlockSpec((CHUNK, CHUNK), cst),
                  pl.BlockSpec((LANES, d_ssd), cst)],
        out_specs=[pl.BlockSpec((CHUNK, d_ssd), tok),
                   pl.BlockSpec((1, SSD_GROUPS, SSD_STATE, gw), lambda b, c: (b, 0, 0, 0))],
        out_shape=[jax.ShapeDtypeStruct((bsz * seq, d_ssd), BF16),
                   jax.ShapeDtypeStruct((bsz, SSD_GROUPS, SSD_STATE, gw), F32)],
        scratch_shapes=[pltpu.VMEM((CHUNK + SUBLANES, cd), F32),
                        pltpu.VMEM((SSD_GROUPS, SSD_STATE, gw), F32),
                        pltpu.VMEM((CHUNK, d_ssd), F32)],
        compiler_params=_cparams("arbitrary", "arbitrary"),
        name="ssd_chunk",
    )(xbc, z, gates, p["conv_ssd_w"], p["conv_ssd_b"], p["dt_bias_row"], p["a_log_row"],
      p["d_skip_row"], p["ssd_norm_g"], p["tril"], p["ssd_expand"])


def _mlstm_chunk_kernel(u_ref, v_ref, o_ref, g_ref, cw_ref, cb_ref, wq_ref, wk_ref, bi_ref, bf_ref, ng_ref,
                        tril_ref, y_ref, c_out_ref, n_out_ref, m_out_ref, upad, c_st, n_st, m_st,
                        *, nc, n_heads, dk):
    c = pl.program_id(1)
    dm = upad.shape[1]

    @pl.when(c == 0)
    def _():
        upad[0:SUBLANES, :] = jnp.zeros((SUBLANES, dm), F32)
        c_st[...] = jnp.zeros_like(c_st)
        n_st[...] = jnp.zeros_like(n_st)
        m_st[...] = jnp.zeros_like(m_st)

    upad[SUBLANES:SUBLANES + CHUNK, :] = u_ref[...]
    conv = cb_ref[...]
    for j in range(CONV_K):
        r0 = SUBLANES - (CONV_K - 1) + j
        conv = conv + cw_ref[j:j + 1, :] * upad[r0:r0 + CHUNK, :]
    upad[0:SUBLANES, :] = upad[CHUNK:CHUNK + SUBLANES, :]
    uc = _silu(conv).astype(BF16)

    logi = g_ref[:, 0:LANES] + bi_ref[...]
    logf = _log_sigmoid(g_ref[:, LANES:2 * LANES] + bf_ref[...])
    bcs = _dot_sel_l(tril_ref[...], logf)
    m_prev = m_st[0:1, :]
    b_end = bcs[CHUNK - 1:CHUNK, :]
    g_all = b_end - bcs + logi
    m_new = jnp.maximum(b_end + m_prev, jnp.max(g_all, axis=0, keepdims=True))
    w_old = jnp.exp(b_end + m_prev - m_new)
    w_s_all = jnp.exp(g_all - m_new)
    inter_all = bcs + m_prev
    bcs_t = bcs.T
    logi_t = logi.T

    row = lax.broadcasted_iota(jnp.int32, (CHUNK, CHUNK), 0)
    col = lax.broadcasted_iota(jnp.int32, (CHUNK, CHUNK), 1)
    causal = row >= col
    scale = dk ** -0.5

    for h in range(n_heads):
        gl = GATE_I0 + h
        sl = slice(h * dk, (h + 1) * dk)
        uc_h = uc[:, sl]
        q = _dot(uc_h, wq_ref[h])
        k = _dot(uc_h, wk_ref[h]) * scale
        qb = q.astype(BF16)
        v_h = v_ref[:, sl]
        vb = v_h.astype(BF16)
        dmat = jnp.where(causal, bcs[:, gl:gl + 1] - bcs_t[gl:gl + 1, :] + logi_t[gl:gl + 1, :], -jnp.inf)
        inter = inter_all[:, gl:gl + 1]
        m_t = jnp.maximum(inter, jnp.max(dmat, axis=1, keepdims=True))
        w_inter = jnp.exp(inter - m_t)
        att = _dot_nt(qb, k.astype(BF16)) * jnp.exp(dmat - m_t)
        c_h = c_st[h]
        n_h = n_st[h][0:1, :]
        num = _dot(att.astype(BF16), vb) + w_inter * _dot(qb, c_h.astype(BF16))
        den = jnp.sum(att, axis=1, keepdims=True) + w_inter * jnp.sum(q * n_h, axis=1, keepdims=True)
        hout = num / jnp.maximum(jnp.abs(den), jnp.exp(-m_t))
        mu = jnp.mean(hout, axis=-1, keepdims=True)
        var = jnp.mean(jnp.square(hout - mu), axis=-1, keepdims=True)
        hn = (hout - mu) * lax.rsqrt(var + LN_EPS) * ng_ref[:, sl]
        y_ref[:, sl] = (_sigmoid(o_ref[:, sl]) * hn).astype(y_ref.dtype)
        kw = k * w_s_all[:, gl:gl + 1]
        wo = w_old[:, gl:gl + 1]
        c_st[h] = wo * c_h + _dot(kw.T.astype(BF16), vb)
        n_st[h] = jnp.broadcast_to(wo * n_h + jnp.sum(kw, axis=0, keepdims=True), (SUBLANES, dk))

    m_st[...] = jnp.broadcast_to(m_new, (SUBLANES, LANES))

    @pl.when(c == nc - 1)
    def _():
        c_out_ref[0] = c_st[...]
        n_out_ref[0] = n_st[...]
        m_out_ref[0] = m_st[...]


def _mlstm_chunk(u, v, o, gates, p, bsz, seq):
    nc = seq // CHUNK
    dm = u.shape[1]
    n_heads, dk, _ = p["w_mq"].shape
    tok = lambda b, c: (b * nc + c, 0)
    cst = lambda b, c: (0, 0)
    cst3 = lambda b, c: (0, 0, 0)
    return pl.pallas_call(
        functools.partial(_mlstm_chunk_kernel, nc=nc, n_heads=n_heads, dk=dk),
        grid=(bsz, nc),
        in_specs=[pl.BlockSpec((CHUNK, dm), tok),
                  pl.BlockSpec((CHUNK, dm), tok),
                  pl.BlockSpec((CHUNK, dm), tok),
                  pl.BlockSpec((CHUNK, 2 * LANES), tok),
                  pl.BlockSpec((CONV_K, dm), cst),
                  pl.BlockSpec((1, dm), cst),
                  pl.BlockSpec((n_heads, dk, dk), cst3),
                  pl.BlockSpec((n_heads, dk, dk), cst3),
                  pl.BlockSpec((1, LANES), cst),
                  pl.BlockSpec((1, LANES), cst),
                  pl.BlockSpec((1, dm), cst),
                  pl.BlockSpec((CHUNK, CHUNK), cst)],
        out_specs=[pl.BlockSpec((CHUNK, dm), tok),
                   pl.BlockSpec((1, n_heads, dk, dk), lambda b, c: (b, 0, 0, 0)),
                   pl.BlockSpec((1, n_heads, SUBLANES, dk), lambda b, c: (b, 0, 0, 0)),
                   pl.BlockSpec((1, SUBLANES, LANES), lambda b, c: (b, 0, 0))],
        out_shape=[jax.ShapeDtypeStruct((bsz * seq, dm), BF16),
                   jax.ShapeDtypeStruct((bsz, n_heads, dk, dk), F32),
                   jax.ShapeDtypeStruct((bsz, n_heads, SUBLANES, dk), F32),
                   jax.ShapeDtypeStruct((bsz, SUBLANES, LANES), F32)],
        scratch_shapes=[pltpu.VMEM((CHUNK + SUBLANES, dm), F32),
                        pltpu.VMEM((n_heads, dk, dk), F32),
                        pltpu.VMEM((n_heads, SUBLANES, dk), F32),
                        pltpu.VMEM((SUBLANES, LANES), F32)],
        compiler_params=_cparams("arbitrary", "arbitrary"),
        name="mlstm_chunk",
    )(u, v, o, gates, p["conv_m_w"], p["conv_m_b"], p["w_mq"], p["w_mk"], p["b_i_row"], p["b_f_row"],
      p["mlstm_norm_g"], p["tril"])


def _col_bcast(row, n_rows):
    return jnp.broadcast_to(row, (LANES, row.shape[1])).T[0:n_rows, :]


def _ssd_step_kernel(xbc_ref, z_ref, g_ref, buf_ref, st_ref, cw_ref, cb_ref, dtb_ref, alog_ref, dsk_ref, ng_ref,
                     exp_ref, y_ref, st_out_ref, *, d_ssd, n_heads):
    gw = d_ssd // SSD_GROUPS
    x_row = xbc_ref[0]
    conv = cb_ref[...] + cw_ref[CONV_K - 1:CONV_K, :] * x_row
    for j in range(CONV_K - 1):
        conv = conv + cw_ref[j:j + 1, :] * buf_ref[0, j:j + 1, :]
    xc = _silu(conv)
    xs = xc[:, 0:d_ssd]

    lane = lax.broadcasted_iota(jnp.int32, (1, LANES), 1)
    a_row = jnp.where(lane < n_heads, -jnp.exp(alog_ref[...]), 0.0)
    dt = _softplus(g_ref[0][:, 0:LANES] + dtb_ref[...])
    dec = jnp.exp(dt * a_row)
    both = _dot_sel_r(jnp.concatenate([jnp.broadcast_to(dt, (SUBLANES, LANES)),
                                       jnp.broadcast_to(dec, (SUBLANES, LANES))], axis=0), exp_ref[...])
    dt_full = both[0:1]
    dec_full = both[SUBLANES:SUBLANES + 1]
    dtx = dt_full * xs
    dtx_col = _col_bcast(dtx, d_ssd)
    dec_col = _col_bcast(dec_full, d_ssd)

    y_parts = []
    for g in range(SSD_GROUPS):
        b_g = xc[:, d_ssd + g * SSD_STATE:d_ssd + (g + 1) * SSD_STATE]
        c_g = xc[:, d_ssd + (SSD_GROUPS + g) * SSD_STATE:d_ssd + (SSD_GROUPS + g + 1) * SSD_STATE]
        h_g = st_ref[0, g * gw:(g + 1) * gw, :]
        cb = jnp.sum(c_g * b_g, axis=-1, keepdims=True)
        c8 = jnp.broadcast_to(c_g, (SUBLANES, SSD_STATE)).astype(BF16)
        y_int = _dot_nt(c8, h_g.astype(BF16))[0:1]
        sl = slice(g * gw, (g + 1) * gw)
        y_parts.append(cb * dtx[:, sl] + dec_full[:, sl] * y_int)
        st_out_ref[0, sl, :] = h_g * dec_col[sl, :] + dtx_col[sl, :] * b_g
    y = jnp.concatenate(y_parts, axis=1) + dsk_ref[...] * xs
    y = y * _silu(z_ref[0])
    y = y * lax.rsqrt(jnp.mean(y * y, axis=-1, keepdims=True) + LN_EPS) * ng_ref[...]
    y_ref[0] = y.astype(y_ref.dtype)


def _ssd_step(xbc, z, gates, buf, state, p):
    nb, cd = xbc.shape
    d_ssd = z.shape[1]
    n_heads = d_ssd // SSD_HEAD_DIM
    row3 = lambda b: (b, 0, 0)
    cst = lambda b: (0, 0)
    st2 = state.reshape(nb, d_ssd, SSD_STATE)
    y, st_new = pl.pallas_call(
        functools.partial(_ssd_step_kernel, d_ssd=d_ssd, n_heads=n_heads),
        grid=(nb,),
        in_specs=[pl.BlockSpec((1, 1, cd), row3),
                  pl.BlockSpec((1, 1, d_ssd), row3),
                  pl.BlockSpec((1, 1, 2 * LANES), row3),
                  pl.BlockSpec((1, CONV_K - 1, cd), row3),
                  pl.BlockSpec((1, d_ssd, SSD_STATE), row3),
                  pl.BlockSpec((CONV_K, cd), cst),
                  pl.BlockSpec((1, cd), cst),
                  pl.BlockSpec((1, LANES), cst),
                  pl.BlockSpec((1, LANES), cst),
                  pl.BlockSpec((1, d_ssd), cst),
                  pl.BlockSpec((1, d_ssd), cst),
                  pl.BlockSpec((LANES, d_ssd), cst)],
        out_specs=[pl.BlockSpec((1, 1, d_ssd), row3),
                   pl.BlockSpec((1, d_ssd, SSD_STATE), row3)],
        out_shape=[jax.ShapeDtypeStruct((nb, 1, d_ssd), BF16),
                   jax.ShapeDtypeStruct((nb, d_ssd, SSD_STATE), F32)],
        compiler_params=_cparams("arbitrary"),
        name="ssd_step",
    )(xbc.reshape(nb, 1, cd), z.reshape(nb, 1, d_ssd), gates.reshape(nb, 1, 2 * LANES), buf, st2,
      p["conv_ssd_w"], p["conv_ssd_b"], p["dt_bias_row"], p["a_log_row"], p["d_skip_row"], p["ssd_norm_g"],
      p["ssd_expand"])
    return y.reshape(nb, d_ssd), st_new.reshape(state.shape)


def _mlstm_step_kernel(u_ref, v_ref, o_ref, g_ref, buf_ref, c_ref, n_ref, m_ref, cw_ref, cb_ref, wq_ref, wk_ref,
                       bi_ref, bf_ref, ng_ref, hexp_ref, y_ref, c_out_ref, n_out_ref, m_out_ref, *, n_heads, dk):
    conv = cb_ref[...] + cw_ref[CONV_K - 1:CONV_K, :] * u_ref[0]
    for j in range(CONV_K - 1):
        conv = conv + cw_ref[j:j + 1, :] * buf_ref[0, j:j + 1, :]
    uc = jnp.broadcast_to(_silu(conv), (SUBLANES, conv.shape[1])).astype(BF16)

    gt = g_ref[0]
    logi = gt[:, 0:LANES] + bi_ref[...]
    logf = _log_sigmoid(gt[:, LANES:2 * LANES] + bf_ref[...])
    m_prev = m_ref[0]
    inter = logf + m_prev
    m_t = jnp.maximum(inter, logi)
    w_inter = jnp.exp(inter - m_t)
    w_in = jnp.exp(logi - m_t)
    floor = jnp.exp(-m_t)
    rows = jnp.concatenate([w_inter, w_in, floor, jnp.zeros((SUBLANES - 3, LANES), F32)], axis=0)
    full = _dot_sel_r(rows, hexp_ref[...])
    scale = dk ** -0.5

    for h in range(n_heads):
        sl = slice(h * dk, (h + 1) * dk)
        q = _dot(uc[:, sl], wq_ref[h])[0:1]
        k = _dot(uc[:, sl], wk_ref[h])[0:1] * scale
        v_h = v_ref[0][:, sl]
        wi = full[0:1, sl]
        ws = full[1:2, sl]
        fl = full[2:3, sl]
        c_h = c_ref[0, h]
        n_h = n_ref[0, h:h + 1, :]
        att = jnp.sum(q * k, axis=-1, keepdims=True) * ws
        q8 = jnp.broadcast_to(q, (SUBLANES, dk)).astype(BF16)
        num = att * v_h + wi * _dot(q8, c_h.astype(BF16))[0:1]
        den = att + wi * jnp.sum(q * n_h, axis=-1, keepdims=True)
        hout = num / jnp.maximum(jnp.abs(den), fl)
        mu = jnp.mean(hout, axis=-1, keepdims=True)
        var = jnp.mean(jnp.square(hout - mu), axis=-1, keepdims=True)
        hn = (hout - mu) * lax.rsqrt(var + LN_EPS) * ng_ref[:, sl]
        y_ref[0, :, sl] = (_sigmoid(o_ref[0][:, sl]) * hn).astype(y_ref.dtype)
        kw = k * ws
        kw_col = _col_bcast(kw, dk)
        wi_full = jnp.broadcast_to(wi, (dk, dk))
        for half in range(dk // LANES):
            hs = slice(half * LANES, (half + 1) * LANES)
            c_out_ref[0, h, :, hs] = wi_full[:, hs] * c_h[:, hs] + kw_col * v_h[:, hs]
        n_out_ref[0, h:h + 1, :] = wi * n_h + kw
    m_out_ref[0] = m_t


def _mlstm_step(u, v, o, gates, buf, c_st, n_st, m_row, p):
    nb, dm = u.shape
    n_heads, dk, _ = p["w_mq"].shape
    row3 = lambda b: (b, 0, 0)
    cst = lambda b: (0, 0)
    cst3 = lambda b: (0, 0, 0)
    r3 = lambda a: a.reshape(nb, 1, a.shape[-1])
    y, c_new, n_new, m_new = pl.pallas_call(
        functools.partial(_mlstm_step_kernel, n_heads=n_heads, dk=dk),
        grid=(nb,),
        in_specs=[pl.BlockSpec((1, 1, dm), row3),
                  pl.BlockSpec((1, 1, dm), row3),
                  pl.BlockSpec((1, 1, dm), row3),
                  pl.BlockSpec((1, 1, 2 * LANES), row3),
                  pl.BlockSpec((1, CONV_K - 1, dm), row3),
                  pl.BlockSpec((1, n_heads, dk, dk), lambda b: (b, 0, 0, 0)),
                  pl.BlockSpec((1, n_heads, dk), row3),
                  pl.BlockSpec((1, 1, LANES), row3),
                  pl.BlockSpec((CONV_K, dm), cst),
                  pl.BlockSpec((1, dm), cst),
                  pl.BlockSpec((n_heads, dk, dk), cst3),
                  pl.BlockSpec((n_heads, dk, dk), cst3),
                  pl.BlockSpec((1, LANES), cst),
                  pl.BlockSpec((1, LANES), cst),
                  pl.BlockSpec((1, dm), cst),
                  pl.BlockSpec((LANES, dm), cst)],
        out_specs=[pl.BlockSpec((1, 1, dm), row3),
                   pl.BlockSpec((1, n_heads, dk, dk), lambda b: (b, 0, 0, 0)),
                   pl.BlockSpec((1, n_heads, dk), row3),
                   pl.BlockSpec((1, 1, LANES), row3)],
        out_shape=[jax.ShapeDtypeStruct((nb, 1, dm), BF16),
                   jax.ShapeDtypeStruct(c_st.shape, F32),
                   jax.ShapeDtypeStruct(n_st.shape, F32),
                   jax.ShapeDtypeStruct((nb, 1, LANES), F32)],
        compiler_params=_cparams("arbitrary"),
        name="mlstm_step",
    )(r3(u), r3(v), r3(o), r3(gates), buf, c_st, n_st, m_row, p["conv_m_w"], p["conv_m_b"], p["w_mq"],
      p["w_mk"], p["b_i_row"], p["b_f_row"], p["mlstm_norm_g"], p["mlstm_expand"])
    return y.reshape(nb, dm), c_new, n_new, m_new


def _outproj_kernel(ys_ref, hm_ref, x_ref, w_ref, g_ref, b_ref, x1t_ref, *, alpha):
    d_ssd = ys_ref.shape[1]
    mix = _dot(ys_ref[...], w_ref[0:d_ssd, :]) + _dot(hm_ref[...], w_ref[d_ssd:, :])
    r = alpha * x_ref[...] + mix
    mu = jnp.mean(r, axis=-1, keepdims=True)
    var = jnp.mean(jnp.square(r - mu), axis=-1, keepdims=True)
    x1 = (r - mu) * lax.rsqrt(var + LN_EPS) * g_ref[...] + b_ref[...]
    x1t_ref[...] = x1.T


def _outproj(ys, hm, x2, p, alpha, tm):
    m, d = x2.shape
    d_ssd, d_m = ys.shape[1], hm.shape[1]
    return pl.pallas_call(
        functools.partial(_outproj_kernel, alpha=alpha),
        grid=(m // tm,),
        in_specs=[pl.BlockSpec((tm, d_ssd), lambda i: (i, 0)),
                  pl.BlockSpec((tm, d_m), lambda i: (i, 0)),
                  pl.BlockSpec((tm, d), lambda i: (i, 0)),
                  pl.BlockSpec((d_ssd + d_m, d), lambda i: (0, 0)),
                  pl.BlockSpec((1, d), lambda i: (0, 0)),
                  pl.BlockSpec((1, d), lambda i: (0, 0))],
        out_specs=pl.BlockSpec((d, tm), lambda i: (0, i)),
        out_shape=jax.ShapeDtypeStruct((d, m), F32),
        compiler_params=_cparams("arbitrary"),
        name="outproj_ln1",
    )(ys, hm, x2, p["w_out"], p["ln1_g"], p["ln1_b"])


def _oddeven_merge_sort_pairs(n):
    pairs = []
    p = 1
    while p < n:
        k = p
        while k >= 1:
            for j in range(k % p, n - k, 2 * k):
                for i in range(min(k, n - j - k)):
                    if (i + j) // (2 * p) == (i + j + k) // (2 * p):
                        pairs.append((i + j, i + j + k))
            k //= 2
        p *= 2
    return pairs


_SORT_TOPK = _oddeven_merge_sort_pairs(PEER_TOPK)


def _compare_exchange(lst, i, j):
    hi, lo = jnp.maximum(lst[i], lst[j]), jnp.minimum(lst[i], lst[j])
    lst[i], lst[j] = hi, lo


def _bitonic_to_sorted(t):
    d = PEER_TOPK // 2
    while d >= 1:
        for i in range(PEER_TOPK):
            if i & d == 0:
                _compare_exchange(t, i, i + d)
        d //= 2
    return t


def _merge_top(x, y):
    neg = jnp.full_like(x[0], -jnp.inf)
    x = x + [neg] * (PEER_TOPK - len(x))
    y = y + [neg] * (PEER_TOPK - len(y))
    return _bitonic_to_sorted([jnp.maximum(x[k], y[PEER_TOPK - 1 - k]) for k in range(PEER_TOPK)])


def _top_rows(s):
    lst = [s[SUBLANES * g:SUBLANES * (g + 1), :] for g in range(PEER_TOPK)]
    for i, j in _SORT_TOPK:
        _compare_exchange(lst, i, j)
    shift = SUBLANES // 2
    while shift >= 1:
        lst = _merge_top(lst, [pltpu.roll(a, shift, 0) for a in lst])
        shift //= 2
    return lst


def _peer_topk_kernel(x1t_ref, wqt_ref, keys_ref, s1_ref, s2_ref, st_ref, b_scr, q_scr, a_scr, *, n_heads):
    tq = x1t_ref.shape[1]
    q_scr[...] = _dot(wqt_ref[...], x1t_ref[...].astype(BF16))
    half = q_scr.shape[0] // (2 * n_heads)

    for h in range(n_heads):
        for side, (s_ref, top_scr) in enumerate(((s1_ref, a_scr), (s2_ref, b_scr))):
            r0 = (2 * h + side) * half
            s = _dot(keys_ref[h, side], q_scr[r0:r0 + half, :].astype(BF16))
            for t in range(tq // LANES):
                ls = slice(t * LANES, (t + 1) * LANES)
                s_ref[h, t] = s[:, ls]
                top = _top_rows(s[:, ls])
                for k in range(PEER_TOPK):
                    top_scr[k, h:h + 1, ls] = top[k][0:1, :]

    for t in range(tq // LANES):
        ls = slice(t * LANES, (t + 1) * LANES)
        a = [a_scr[k, :, ls] for k in range(PEER_TOPK)]
        b = [b_scr[k, :, ls] for k in range(PEER_TOPK)]
        single = PEER_TOPK // 2
        lists = [[a[i] + b[j] for j in range(PEER_TOPK // (i + 1))] for i in range(single)]
        lists.append([a[i] + b[0] for i in range(single, PEER_TOPK)])
        merged = lists[0]
        for other in lists[1:]:
            merged = _merge_top(merged, other)
        tau = merged[PEER_TOPK - 1]
        top = a[0] + b[0]
        zsum = jnp.zeros_like(tau)
        for cand in (c for lst in lists for c in lst):
            zsum = zsum + jnp.where(cand >= tau, jnp.exp(cand - top), 0.0)
        st_ref[0, :, ls] = tau
        st_ref[1, :, ls] = a[0]
        st_ref[2, :, ls] = b[0]
        st_ref[3, :, ls] = 1.0 / zsum


def _peer_topk(x1t, p, tq):
    d, m = x1t.shape
    n_heads, _, n_keys, half = p["peer_keys"].shape
    nq = p["peer_wqT"].shape[0]
    assert n_heads == SUBLANES and n_keys == PEER_TOPK * SUBLANES
    keyb = lambda i: (0, i, 0, 0)
    return pl.pallas_call(
        functools.partial(_peer_topk_kernel, n_heads=n_heads),
        grid=(m // tq,),
        in_specs=[pl.BlockSpec((d, tq), lambda i: (0, i)),
                  pl.BlockSpec((nq, d), lambda i: (0, 0)),
                  pl.BlockSpec((n_heads, 2, n_keys, half), lambda i: (0, 0, 0, 0))],
        out_specs=[pl.BlockSpec((n_heads, tq // LANES, n_keys, LANES), keyb),
                   pl.BlockSpec((n_heads, tq // LANES, n_keys, LANES), keyb),
                   pl.BlockSpec((N_STATS, SUBLANES, tq), lambda i: (0, 0, i)),
                   pl.BlockSpec((PEER_TOPK, SUBLANES, tq), lambda i: (0, 0, i))],
        out_shape=[jax.ShapeDtypeStruct((n_heads, m // LANES, n_keys, LANES), F32),
                   jax.ShapeDtypeStruct((n_heads, m // LANES, n_keys, LANES), F32),
                   jax.ShapeDtypeStruct((N_STATS, SUBLANES, m), F32),
                   jax.ShapeDtypeStruct((PEER_TOPK, SUBLANES, m), F32)],
        scratch_shapes=[pltpu.VMEM((nq, tq), F32),
                        pltpu.VMEM((PEER_TOPK, SUBLANES, tq), F32)],
        compiler_params=_cparams("arbitrary"),
        name="peer_topk",
    )(x1t, p["peer_wqT"], p["peer_keys"])


def _peer_dense_kernel(x1t_ref, s1_ref, s2_ref, st_ref, bl_ref, u0_ref, u_ref, vt_ref, g_ref, b_ref, out_ref,
                       xb_scr, e2_scr, c_scr, thr_scr, acc_scr, s_a, s_b, w_scr, thr_a, thr_b, cr_a, cr_b,
                       *, alpha, n_heads):
    jj = pl.program_id(1)
    ec = u0_ref.shape[0]
    tb = x1t_ref.shape[1]
    nt = tb // LANES
    n_keys = s1_ref.shape[2]
    na = ec // n_keys

    @pl.when(jj == 0)
    def _():
        xb_scr[...] = x1t_ref[...].astype(BF16)
        for h in range(n_heads):
            for t in range(nt):
                ls = slice(t * LANES, (t + 1) * LANES)
                e2_scr[h, t] = jnp.exp(s2_ref[h, t] - st_ref[2, h:h + 1, ls])
                s1 = s1_ref[h, t]
                c_scr[h, t] = jnp.exp(s1 - st_ref[1, h:h + 1, ls]) * (0.5 * st_ref[3, h:h + 1, ls])
                tau = st_ref[0, h:h + 1, ls]
                thr = jnp.full((n_keys, LANES), jnp.inf, F32)
                for k in range(PEER_TOPK):
                    bk = bl_ref[k, h:h + 1, ls]
                    thr = jnp.where(s1 + bk >= tau, bk, thr)
                thr_scr[h, t] = thr
        acc_scr[...] = jnp.zeros_like(acc_scr)
        s_a[:, 0:tb] = _dot(u0_ref[...], xb_scr[...])

    stages = ((s_a, s_b, thr_a, cr_a), (s_b, s_a, thr_b, cr_b))
    g0 = pl.multiple_of(jj * SUBLANES, SUBLANES)
    for half, (s_in, s_out, thr_r, cr) in enumerate(stages):
        off = half * na
        for h in range(n_heads):
            for t in range(nt):
                ls = slice(t * LANES, (t + 1) * LANES)
                thr_blk = thr_scr[h, t, pl.ds(g0, SUBLANES), :]
                c_blk = c_scr[h, t, pl.ds(g0, SUBLANES), :]
                for al in range(na):
                    thr_r[h * na + al, :, ls] = thr_blk[off + al:off + al + 1, :]
                    cr[h * na + al, :, ls] = c_blk[off + al:off + al + 1, :]

        def first_key(al, carry, half=half, s_in=s_in, s_out=s_out, thr_r=thr_r, cr=cr):
            r0 = pl.multiple_of(al * n_keys, n_keys)
            for t in range(nt):
                ls = slice(t * LANES, (t + 1) * LANES)
                gate = jnp.zeros((n_keys, LANES), F32)
                for h in range(n_heads):
                    gate = gate + jnp.where(s2_ref[h, t] >= thr_r[h * na + al][:, ls],
                                            e2_scr[h, t] * cr[h * na + al][:, ls], 0.0)
                x = s_in[pl.ds(r0, n_keys), ls]
                hid = x + x * lax.erf(x * (1.0 / math.sqrt(2.0)))
                w_scr[pl.ds(r0, n_keys), ls] = (gate * hid).astype(BF16)
            u_rows = pl.multiple_of(half * ec + al * n_keys, n_keys)
            s_out[pl.ds(r0, n_keys), 0:tb] = _dot(u_ref[pl.ds(u_rows, n_keys), :], xb_scr[...])
            return carry

        lax.fori_loop(0, na, first_key, 0)
        acc_scr[...] += _dot(vt_ref[:, half * ec:(half + 1) * ec], w_scr[:, 0:tb])

    @pl.when(jj == pl.num_programs(1) - 1)
    def _():
        r = alpha * x1t_ref[...] + acc_scr[...]
        mu = jnp.mean(r, axis=0, keepdims=True)
        var = jnp.mean(jnp.square(r - mu), axis=0, keepdims=True)
        y = (r - mu) * lax.rsqrt(var + LN_EPS) * g_ref[...] + b_ref[...]
        out_ref[...] = y.T


def _peer_dense(x1t, s1, s2, st, blist, p, alpha, tb):
    d, m = x1t.shape
    n_heads, _, n_keys, _ = s1.shape
    n_exp = p["peer_u"].shape[0]
    ec = PEER_CHUNK
    nj = n_exp // ec
    na = ec // n_keys
    assert nj % 2 == 0 and 2 * na == SUBLANES
    nt = tb // LANES
    keyb = lambda i, j: (0, i, 0, 0)
    tokb = lambda i, j: (0, 0, i)
    pitch = tb + LANES
    return pl.pallas_call(
        functools.partial(_peer_dense_kernel, alpha=alpha, n_heads=n_heads),
        grid=(m // tb, nj // 2),
        in_specs=[pl.BlockSpec((d, tb), lambda i, j: (0, i)),
                  pl.BlockSpec((n_heads, nt, n_keys, LANES), keyb),
                  pl.BlockSpec((n_heads, nt, n_keys, LANES), keyb),
                  pl.BlockSpec((N_STATS, SUBLANES, tb), tokb),
                  pl.BlockSpec((PEER_TOPK, SUBLANES, tb), tokb),
                  pl.BlockSpec((ec, d), lambda i, j: (0, 0)),
                  pl.BlockSpec((2 * ec, d), lambda i, j: (j, 0)),
                  pl.BlockSpec((d, 2 * ec), lambda i, j: (0, j)),
                  pl.BlockSpec((d, 1), lambda i, j: (0, 0)),
                  pl.BlockSpec((d, 1), lambda i, j: (0, 0))],
        out_specs=pl.BlockSpec((tb, d), lambda i, j: (i, 0)),
        out_shape=jax.ShapeDtypeStruct((m, d), F32),
        scratch_shapes=[pltpu.VMEM((d, tb), BF16),
                        pltpu.VMEM((n_heads, nt, n_keys, LANES), F32),
                        pltpu.VMEM((n_heads, nt, n_keys, LANES), F32),
                        pltpu.VMEM((n_heads, nt, n_keys, LANES), F32),
                        pltpu.VMEM((d, tb), F32),
                        pltpu.VMEM((ec, pitch), F32),
                        pltpu.VMEM((ec, pitch), F32),
                        pltpu.VMEM((ec, pitch), BF16),
                        pltpu.VMEM((n_heads * na, 1, tb), F32),
                        pltpu.VMEM((n_heads * na, 1, tb), F32),
                        pltpu.VMEM((n_heads * na, 1, tb), F32),
                        pltpu.VMEM((n_heads * na, 1, tb), F32)],
        compiler_params=_cparams("arbitrary", "arbitrary"),
        name="peer_dense",
    )(x1t, s1, s2, st, blist, p["peer_u"], p["peer_u_next"], p["peer_vT"], p["ln2_g_col"], p["ln2_b_col"])


def _pad_row(vec, offset):
    return jnp.zeros((1, LANES), F32).at[0, offset:offset + vec.shape[0]].set(vec.astype(F32))


def _prep_layer(l, w_in, conv_ssd_w, conv_ssd_b, dt_bias, a_log, d_skip, ssd_norm_g, conv_m_w, conv_m_b,
                w_mq, w_mk, b_igate, b_fgate, mlstm_norm_g, w_out, ln1_g, ln1_b, peer_wq, peer_keys,
                peer_u, peer_v, ln2_g, ln2_b):
    d_model = w_in.shape[1]
    d_ssd = ssd_norm_g.shape[1]
    cd = conv_ssd_w.shape[2]
    n_sh = dt_bias.shape[1]
    d_m = conv_m_w.shape[2]
    n_mh = w_mq.shape[1]
    dk = w_mq.shape[2]
    assert n_sh <= GATE_I0 and GATE_I0 + n_mh <= LANES and d_ssd == n_sh * SSD_HEAD_DIM
    w = w_in[l]
    offs = [0]
    for n in (d_ssd, cd, n_sh, d_m, d_m, d_m, n_mh, n_mh):
        offs.append(offs[-1] + n)
    z_w, xbc_w, dt_w, u_w, v_w, o_w, i_w, f_w = (w[:, offs[k]:offs[k + 1]] for k in range(8))
    gate_a = jnp.zeros((d_model, LANES), F32).at[:, 0:n_sh].set(dt_w).at[:, GATE_I0:GATE_I0 + n_mh].set(i_w)
    gate_b = jnp.zeros((d_model, LANES), F32).at[:, GATE_I0:GATE_I0 + n_mh].set(f_w)
    w_cat = jnp.concatenate([z_w, xbc_w, u_w, v_w, o_w, gate_a, gate_b], axis=1).astype(BF16)
    widths = (d_ssd, cd, d_m, d_m, d_m, 2 * LANES)

    lane = jnp.arange(LANES)[:, None]
    ssd_expand = (lane == (jnp.arange(d_ssd)[None, :] // SSD_HEAD_DIM)).astype(BF16)
    mlstm_expand = (lane == (GATE_I0 + jnp.arange(d_m)[None, :] // dk)).astype(BF16)
    t = jnp.arange(CHUNK)
    tril = (t[:, None] >= t[None, :]).astype(BF16)
    p = {
        "w_cat": w_cat,
        "conv_ssd_w": conv_ssd_w[l], "conv_ssd_b": conv_ssd_b[l][None, :],
        "dt_bias_row": _pad_row(dt_bias[l], 0), "a_log_row": _pad_row(a_log[l], 0),
        "d_skip_row": jnp.repeat(d_skip[l], SSD_HEAD_DIM)[None, :], "ssd_norm_g": ssd_norm_g[l][None, :],
        "conv_m_w": conv_m_w[l], "conv_m_b": conv_m_b[l][None, :],
        "w_mq": w_mq[l].astype(BF16), "w_mk": w_mk[l].astype(BF16),
        "b_i_row": _pad_row(b_igate[l], GATE_I0), "b_f_row": _pad_row(b_fgate[l], GATE_I0),
        "mlstm_norm_g": mlstm_norm_g[l].reshape(1, d_m),
        "w_out": w_out[l].astype(BF16), "ln1_g": ln1_g[l][None, :], "ln1_b": ln1_b[l][None, :],
        "peer_wqT": peer_wq[l].T.astype(BF16), "peer_keys": peer_keys[l].astype(BF16),
        "peer_u": peer_u[l].astype(BF16),
        "peer_u_next": jnp.pad(peer_u[l].astype(BF16)[PEER_CHUNK:], ((0, PEER_CHUNK), (0, 0))),
        "peer_vT": peer_v[l].astype(BF16).T,
        "ln2_g_col": ln2_g[l][:, None], "ln2_b_col": ln2_b[l][:, None],
        "tril": tril, "ssd_expand": ssd_expand, "mlstm_expand": mlstm_expand,
    }
    return p, widths


def _tile_sizes(m):
    tm = 256 if m % 256 == 0 else LANES
    tb = 512 if m % 512 == 0 else LANES
    return tm, tb


def _peer_block(x1t, p, alpha):
    m = x1t.shape[1]
    tm, tb = _tile_sizes(m)
    s1, s2, st, blist = _peer_topk(x1t, p, tm)
    return _peer_dense(x1t, s1, s2, st, blist, p, alpha, tb)


def _layer_prompt(x, p, widths, alpha):
    bsz, seq, d = x.shape
    assert seq % CHUNK == 0 and seq >= CONV_K - 1
    x2 = x.reshape(bsz * seq, d)
    tm, _ = _tile_sizes(bsz * seq)
    z, xbc, u, v, o, gates = _inproj(x2, p["w_cat"], widths, tm)
    y_s, h_t = _ssd_chunk(xbc, z, gates, p, bsz, seq)
    h_m, c_new, n_new, m_new = _mlstm_chunk(u, v, o, gates, p, bsz, seq)
    x1t = _outproj(y_s, h_m, x2, p, alpha, tm)
    y = _peer_block(x1t, p, alpha).reshape(bsz, seq, d)
    n_mh = c_new.shape[1]
    d_ssd = z.shape[1]
    n_sh = d_ssd // SSD_HEAD_DIM
    hg = n_sh // SSD_GROUPS
    new_ssd = h_t.reshape(bsz, SSD_GROUPS, SSD_STATE, hg, SSD_HEAD_DIM).transpose(0, 1, 3, 4, 2)
    new_ssd = new_ssd.reshape(bsz, n_sh, SSD_HEAD_DIM, SSD_STATE)
    states = (new_ssd,
              xbc.reshape(bsz, seq, -1)[:, seq - (CONV_K - 1):, :],
              c_new,
              n_new[:, :, 0, :],
              m_new[:, 0, GATE_I0:GATE_I0 + n_mh],
              u.reshape(bsz, seq, -1)[:, seq - (CONV_K - 1):, :])
    return y, states


def _layer_sample(x, st, p, widths, alpha):
    st_ssd, st_ssd_conv, st_c, st_n, st_m, st_mconv = st
    nb, seq, d = x.shape
    assert seq == 1 and nb % LANES == 0
    x2 = x.reshape(nb, d)
    z, xbc, u, v, o, gates = _inproj(x2, p["w_cat"], widths, LANES)
    y_s, new_ssd = _ssd_step(xbc, z, gates, st_ssd_conv, st_ssd, p)
    n_mh = st_c.shape[1]
    m_row = jnp.zeros((nb, 1, LANES), F32).at[:, 0, GATE_I0:GATE_I0 + n_mh].set(st_m)
    h_m, c_new, n_new, m_new = _mlstm_step(u, v, o, gates, st_mconv, st_c, st_n, m_row, p)
    x1t = _outproj(y_s, h_m, x2, p, alpha, LANES)
    y = _peer_block(x1t, p, alpha).reshape(nb, seq, d)
    states = (new_ssd,
              jnp.concatenate([st_ssd_conv[:, 1:, :], xbc[:, None, :]], axis=1),
              c_new,
              n_new,
              m_new[:, 0, GATE_I0:GATE_I0 + n_mh],
              jnp.concatenate([st_mconv[:, 1:, :], u[:, None, :]], axis=1))
    return y, states


def kernel(x_prompt, x_sample, state_ssd, state_ssd_conv, state_mlstm_c, state_mlstm_n, state_mlstm_m,
           state_mlstm_conv, w_in, conv_ssd_w, conv_ssd_b, dt_bias, a_log, d_skip, ssd_norm_g, conv_m_w,
           conv_m_b, w_mq, w_mk, b_igate, b_fgate, mlstm_norm_g, w_out, ln1_g, ln1_b, peer_wq, peer_keys,
           peer_u, peer_v, ln2_g, ln2_b):
    depth = w_in.shape[0]
    alpha = (2.0 * depth) ** 0.25
    states = (state_ssd, state_ssd_conv, state_mlstm_c, state_mlstm_n, state_mlstm_m, state_mlstm_conv)
    yp, ys = x_prompt, x_sample
    p_list, s_list = [], []
    for l in range(depth):
        p, widths = _prep_layer(l, w_in, conv_ssd_w, conv_ssd_b, dt_bias, a_log, d_skip, ssd_norm_g, conv_m_w,
                                conv_m_b, w_mq, w_mk, b_igate, b_fgate, mlstm_norm_g, w_out, ln1_g, ln1_b,
                                peer_wq, peer_keys, peer_u, peer_v, ln2_g, ln2_b)
        yp, p_new = _layer_prompt(yp, p, widths, alpha)
        ys, s_new = _layer_sample(ys, tuple(s[l] for s in states), p, widths, alpha)
        p_list.append(p_new)
        s_list.append(s_new)
    p_out = tuple(jnp.stack([pn[i] for pn in p_list]) for i in range(6))
    s_out = tuple(jnp.stack([sn[i] for sn in s_list]) for i in range(6))
    return (yp, ys) + p_out + s_out
```

```python
import functools
import math

import jax
import jax.numpy as jnp
from jax import lax
from jax.experimental import pallas as pl
from jax.experimental.pallas import tpu as pltpu

F32 = jnp.float32
BF16 = jnp.bfloat16

LANES = 128
SUBLANES = 8
CHUNK = 128
CONV_K = 4
LN_EPS = 1e-5
VMEM_LIMIT = 52 * 1024 * 1024

SSD_HEAD_DIM = 64
SSD_STATE = 128
SSD_GROUPS = 2
PEER_TOPK = 16
GATE_I0 = 16
N_STATS = 4
PEER_CHUNK = 1024
STEP_SEQS = 4


def _cparams(*sem):
    return pltpu.CompilerParams(dimension_semantics=sem, vmem_limit_bytes=VMEM_LIMIT)


def _dot(a, b):
    return jnp.dot(a, b, preferred_element_type=F32)


def _dot_nt(a, b):
    return lax.dot_general(a, b, (((1,), (1,)), ((), ())), preferred_element_type=F32)


def _split3(x):
    h = x.astype(BF16)
    r = x - h.astype(F32)
    m = r.astype(BF16)
    lo = (r - m.astype(F32)).astype(BF16)
    return h, m, lo


def _dot_sel_l(sel_bf16, x):
    h, m, lo = _split3(x)
    return _dot(sel_bf16, h) + _dot(sel_bf16, m) + _dot(sel_bf16, lo)


def _dot_sel_r(x, sel_bf16):
    h, m, lo = _split3(x)
    return _dot(h, sel_bf16) + _dot(m, sel_bf16) + _dot(lo, sel_bf16)


def _sigmoid(x):
    return 1.0 / (1.0 + jnp.exp(-x))


def _silu(x):
    return x * _sigmoid(x)


def _softplus(x):
    return jnp.maximum(x, 0.0) + jnp.log1p(jnp.exp(-jnp.abs(x)))


def _log_sigmoid(x):
    return -_softplus(-x)


def _gelu_exact(x):
    return 0.5 * x * (1.0 + lax.erf(x * (1.0 / math.sqrt(2.0))))


def _inproj_kernel(x_ref, w_ref, *out_refs):
    xb = x_ref[...].astype(BF16)
    off = 0
    for ref in out_refs:
        n = ref.shape[-1]
        ref[...] = _dot(xb, w_ref[:, off:off + n])
        off += n


def _inproj(x2, w_cat, widths, tm):
    m, d = x2.shape
    n_all = w_cat.shape[1]
    return pl.pallas_call(
        _inproj_kernel,
        grid=(m // tm,),
        in_specs=[pl.BlockSpec((tm, d), lambda i: (i, 0)),
                  pl.BlockSpec((d, n_all), lambda i: (0, 0))],
        out_specs=[pl.BlockSpec((tm, n), lambda i: (i, 0)) for n in widths],
        out_shape=[jax.ShapeDtypeStruct((m, n), F32) for n in widths],
        compiler_params=_cparams("arbitrary"),
        name="inproj",
    )(x2, w_cat)


def _ssd_chunk_kernel(xbc_ref, z_ref, g_ref, cw_ref, cb_ref, dtb_ref, alog_ref, dsk_ref, ng_ref,
                      tril_ref, exp_ref, y_ref, hT_out_ref, xpad, hT, ybuf, *, nc, d_ssd, n_heads):
    c = pl.program_id(1)
    cd = xpad.shape[1]
    hg = n_heads // SSD_GROUPS
    gw = d_ssd // SSD_GROUPS

    @pl.when(c == 0)
    def _():
        xpad[0:SUBLANES, :] = jnp.zeros((SUBLANES, cd), F32)
        hT[...] = jnp.zeros_like(hT)

    xpad[SUBLANES:SUBLANES + CHUNK, :] = xbc_ref[...]
    conv = cb_ref[...]
    for j in range(CONV_K):
        r0 = SUBLANES - (CONV_K - 1) + j
        conv = conv + cw_ref[j:j + 1, :] * xpad[r0:r0 + CHUNK, :]
    xpad[0:SUBLANES, :] = xpad[CHUNK:CHUNK + SUBLANES, :]
    xc = _silu(conv)
    xs = xc[:, 0:d_ssd]

    lane = lax.broadcasted_iota(jnp.int32, (1, LANES), 1)
    a_row = jnp.where(lane < n_heads, -jnp.exp(alog_ref[...]), 0.0)
    dt = _softplus(g_ref[:, 0:LANES] + dtb_ref[...])
    acs = _dot_sel_l(tril_ref[...], dt * a_row)
    dt_t = dt.T
    acs_t = acs.T
    eacs = jnp.exp(acs)
    dec = jnp.exp(acs[CHUNK - 1:CHUNK, :] - acs) * dt
    both = _dot_sel_r(jnp.concatenate([eacs, dec], axis=0), exp_ref[...])
    eacs_full = both[0:CHUNK]
    dec_full = both[CHUNK:2 * CHUNK]

    row = lax.broadcasted_iota(jnp.int32, (CHUNK, CHUNK), 0)
    col = lax.broadcasted_iota(jnp.int32, (CHUNK, CHUNK), 1)
    causal = row >= col

    for g in range(SSD_GROUPS):
        b_g = xc[:, d_ssd + g * SSD_STATE:d_ssd + (g + 1) * SSD_STATE]
        c_g = xc[:, d_ssd + (SSD_GROUPS + g) * SSD_STATE:d_ssd + (SSD_GROUPS + g + 1) * SSD_STATE]
        b_gb = b_g.astype(BF16)
        c_gb = c_g.astype(BF16)
        cb = _dot_nt(c_gb, b_gb)
        h_g = hT[g]
        y_inter = _dot(c_gb, h_g.astype(BF16)) * eacs_full[:, g * gw:(g + 1) * gw]
        for hl in range(hg):
            h = g * hg + hl
            diff = acs[:, h:h + 1] - acs_t[h:h + 1, :]
            lmat = jnp.exp(jnp.where(causal, diff, -jnp.inf))
            mm = cb * lmat * dt_t[h:h + 1, :]
            x_h = xs[:, h * SSD_HEAD_DIM:(h + 1) * SSD_HEAD_DIM]
            ybuf[:, h * SSD_HEAD_DIM:(h + 1) * SSD_HEAD_DIM] = _dot(mm.astype(BF16), x_h.astype(BF16))
        xd = xs[:, g * gw:(g + 1) * gw] * dec_full[:, g * gw:(g + 1) * gw]
        hT[g] = h_g * eacs_full[CHUNK - 1:CHUNK, g * gw:(g + 1) * gw] + _dot(b_g.T.astype(BF16), xd.astype(BF16))
        ybuf[:, g * gw:(g + 1) * gw] = ybuf[:, g * gw:(g + 1) * gw] + y_inter

    y = ybuf[...] + dsk_ref[...] * xs
    y = y * _silu(z_ref[...])
    y = y * lax.rsqrt(jnp.mean(y * y, axis=-1, keepdims=True) + LN_EPS) * ng_ref[...]
    y_ref[...] = y.astype(y_ref.dtype)

    @pl.when(c == nc - 1)
    def _():
        hT_out_ref[0] = hT[...]


def _ssd_chunk(xbc, z, gates, p, bsz, seq):
    nc = seq // CHUNK
    cd = xbc.shape[1]
    d_ssd = z.shape[1]
    n_heads = d_ssd // SSD_HEAD_DIM
    gw = d_ssd // SSD_GROUPS
    tok = lambda b, c: (b * nc + c, 0)
    cst = lambda b, c: (0, 0)
    return pl.pallas_call(
        functools.partial(_ssd_chunk_kernel, nc=nc, d_ssd=d_ssd, n_heads=n_heads),
        grid=(bsz, nc),
        in_specs=[pl.BlockSpec((CHUNK, cd), tok),
                  pl.BlockSpec((CHUNK, d_ssd), tok),
                  pl.BlockSpec((CHUNK, 2 * LANES), tok),
                  pl.BlockSpec((CONV_K, cd), cst),
                  pl.BlockSpec((1, cd), cst),
                  pl.BlockSpec((1, LANES), cst),
                  pl.BlockSpec((1, LANES), cst),
                  pl.BlockSpec((1, d_ssd), cst),
                  pl.BlockSpec((1, d_ssd), cst),
                  pl.BlockSpec((CHUNK, CHUNK), cst),
                  pl.BlockSpec((LANES, d_ssd), cst)],
        out_specs=[pl.BlockSpec((CHUNK, d_ssd), tok),
                   pl.BlockSpec((1, SSD_GROUPS, SSD_STATE, gw), lambda b, c: (b, 0, 0, 0))],
        out_shape=[jax.ShapeDtypeStruct((bsz * seq, d_ssd), BF16),
                   jax.ShapeDtypeStruct((bsz, SSD_GROUPS, SSD_STATE, gw), F32)],
        scratch_shapes=[pltpu.VMEM((CHUNK + SUBLANES, cd), F32),
                        pltpu.VMEM((SSD_GROUPS, SSD_STATE, gw), F32),
                        pltpu.VMEM((CHUNK, d_ssd), F32)],
        compiler_params=_cparams("arbitrary", "arbitrary"),
        name="ssd_chunk",
    )(xbc, z, gates, p["conv_ssd_w"], p["conv_ssd_b"], p["dt_bias_row"], p["a_log_row"],
      p["d_skip_row"], p["ssd_norm_g"], p["tril"], p["ssd_expand"])


def _mlstm_chunk_kernel(u_ref, v_ref, o_ref, g_ref, cw_ref, cb_ref, wq_ref, wk_ref, bi_ref, bf_ref, ng_ref,
                        tril_ref, y_ref, c_out_ref, n_out_ref, m_out_ref, upad, c_st, n_st, m_st,
                        *, nc, n_heads, dk):
    c = pl.program_id(1)
    dm = upad.shape[1]

    @pl.when(c == 0)
    def _():
        upad[0:SUBLANES, :] = jnp.zeros((SUBLANES, dm), F32)
        c_st[...] = jnp.zeros_like(c_st)
        n_st[...] = jnp.zeros_like(n_st)
        m_st[...] = jnp.zeros_like(m_st)

    upad[SUBLANES:SUBLANES + CHUNK, :] = u_ref[...]
    conv = cb_ref[...]
    for j in range(CONV_K):
        r0 = SUBLANES - (CONV_K - 1) + j
        conv = conv + cw_ref[j:j + 1, :] * upad[r0:r0 + CHUNK, :]
    upad[0:SUBLANES, :] = upad[CHUNK:CHUNK + SUBLANES, :]
    uc = _silu(conv).astype(BF16)

    logi = g_ref[:, 0:LANES] + bi_ref[...]
    logf = _log_sigmoid(g_ref[:, LANES:2 * LANES] + bf_ref[...])
    bcs = _dot_sel_l(tril_ref[...], logf)
    m_prev = m_st[0:1, :]
    b_end = bcs[CHUNK - 1:CHUNK, :]
    g_all = b_end - bcs + logi
    m_new = jnp.maximum(b_end + m_prev, jnp.max(g_all, axis=0, keepdims=True))
    w_old = jnp.exp(b_end + m_prev - m_new)
    w_s_all = jnp.exp(g_all - m_new)
    inter_all = bcs + m_prev
    bcs_t = bcs.T
    logi_t = logi.T

    row = lax.broadcasted_iota(jnp.int32, (CHUNK, CHUNK), 0)
    col = lax.broadcasted_iota(jnp.int32, (CHUNK, CHUNK), 1)
    causal = row >= col
    scale = dk ** -0.5

    for h in range(n_heads):
        gl = GATE_I0 + h
        sl = slice(h * dk, (h + 1) * dk)
        uc_h = uc[:, sl]
        q = _dot(uc_h, wq_ref[h])
        k = _dot(uc_h, wk_ref[h]) * scale
        qb = q.astype(BF16)
        v_h = v_ref[:, sl]
        vb = v_h.astype(BF16)
        dmat = jnp.where(causal, bcs[:, gl:gl + 1] - bcs_t[gl:gl + 1, :] + logi_t[gl:gl + 1, :], -jnp.inf)
        inter = inter_all[:, gl:gl + 1]
        m_t = jnp.maximum(inter, jnp.max(dmat, axis=1, keepdims=True))
        w_inter = jnp.exp(inter - m_t)
        att = _dot_nt(qb, k.astype(BF16)) * jnp.exp(dmat - m_t)
        c_h = c_st[h]
        n_h = n_st[h][0:1, :]
        num = _dot(att.astype(BF16), vb) + w_inter * _dot(qb, c_h.astype(BF16))
        den = jnp.sum(att, axis=1, keepdims=True) + w_inter * jnp.sum(q * n_h, axis=1, keepdims=True)
        hout = num / jnp.maximum(jnp.abs(den), jnp.exp(-m_t))
        mu = jnp.mean(hout, axis=-1, keepdims=True)
        var = jnp.mean(jnp.square(hout - mu), axis=-1, keepdims=True)
        hn = (hout - mu) * lax.rsqrt(var + LN_EPS) * ng_ref[:, sl]
        y_ref[:, sl] = (_sigmoid(o_ref[:, sl]) * hn).astype(y_ref.dtype)
        kw = k * w_s_all[:, gl:gl + 1]
        wo = w_old[:, gl:gl + 1]
        c_st[h] = wo * c_h + _dot(kw.T.astype(BF16), vb)
        n_st[h] = jnp.broadcast_to(wo * n_h + jnp.sum(kw, axis=0, keepdims=True), (SUBLANES, dk))

    m_st[...] = jnp.broadcast_to(m_new, (SUBLANES, LANES))

    @pl.when(c == nc - 1)
    def _():
        c_out_ref[0] = c_st[...]
        n_out_ref[0] = n_st[...]
        m_out_ref[0] = m_st[...]


def _mlstm_chunk(u, v, o, gates, p, bsz, seq):
    nc = seq // CHUNK
    dm = u.shape[1]
    n_heads, dk, _ = p["w_mq"].shape
    tok = lambda b, c: (b * nc + c, 0)
    cst = lambda b, c: (0, 0)
    cst3 = lambda b, c: (0, 0, 0)
    return pl.pallas_call(
        functools.partial(_mlstm_chunk_kernel, nc=nc, n_heads=n_heads, dk=dk),
        grid=(bsz, nc),
        in_specs=[pl.BlockSpec((CHUNK, dm), tok),
                  pl.BlockSpec((CHUNK, dm), tok),
                  pl.BlockSpec((CHUNK, dm), tok),
                  pl.BlockSpec((CHUNK, 2 * LANES), tok),
                  pl.BlockSpec((CONV_K, dm), cst),
                  pl.BlockSpec((1, dm), cst),
                  pl.BlockSpec((n_heads, dk, dk), cst3),
                  pl.BlockSpec((n_heads, dk, dk), cst3),
                  pl.BlockSpec((1, LANES), cst),
                  pl.BlockSpec((1, LANES), cst),
                  pl.BlockSpec((1, dm), cst),
                  pl.BlockSpec((CHUNK, CHUNK), cst)],
        out_specs=[pl.BlockSpec((CHUNK, dm), tok),
                   pl.BlockSpec((1, n_heads, dk, dk), lambda b, c: (b, 0, 0, 0)),
                   pl.BlockSpec((1, n_heads, SUBLANES, dk), lambda b, c: (b, 0, 0, 0)),
                   pl.BlockSpec((1, SUBLANES, LANES), lambda b, c: (b, 0, 0))],
        out_shape=[jax.ShapeDtypeStruct((bsz * seq, dm), BF16),
                   jax.ShapeDtypeStruct((bsz, n_heads, dk, dk), F32),
                   jax.ShapeDtypeStruct((bsz, n_heads, SUBLANES, dk), F32),
                   jax.ShapeDtypeStruct((bsz, SUBLANES, LANES), F32)],
        scratch_shapes=[pltpu.VMEM((CHUNK + SUBLANES, dm), F32),
                        pltpu.VMEM((n_heads, dk, dk), F32),
                        pltpu.VMEM((n_heads, SUBLANES, dk), F32),
                        pltpu.VMEM((SUBLANES, LANES), F32)],
        compiler_params=_cparams("arbitrary", "arbitrary"),
        name="mlstm_chunk",
    )(u, v, o, gates, p["conv_m_w"], p["conv_m_b"], p["w_mq"], p["w_mk"], p["b_i_row"], p["b_f_row"],
      p["mlstm_norm_g"], p["tril"])


def _col_bcast(row, n_rows):
    return jnp.broadcast_to(row, (LANES, row.shape[1])).T[0:n_rows, :]


def _per_sequence(one_sequence, n_seq_in, n_seq_out):
    def kernel(*refs, **static):
        seq_in, shared, seq_out = refs[:n_seq_in], refs[n_seq_in:len(refs) - n_seq_out], refs[len(refs) - n_seq_out:]
        for s in range(seq_in[0].shape[0]):
            one = slice(s, s + 1)
            one_sequence(*(r.at[one] for r in seq_in), *shared, *(r.at[one] for r in seq_out), **static)
    return kernel


def _ssd_step_one(xbc_ref, z_ref, g_ref, buf_ref, st_ref, cw_ref, cb_ref, dtb_ref, alog_ref, dsk_ref, ng_ref,
                  exp_ref, y_ref, st_out_ref, *, d_ssd, n_heads):
    gw = d_ssd // SSD_GROUPS
    x_row = xbc_ref[0]
    conv = cb_ref[...] + cw_ref[CONV_K - 1:CONV_K, :] * x_row
    for j in range(CONV_K - 1):
        conv = conv + cw_ref[j:j + 1, :] * buf_ref[0, j:j + 1, :]
    xc = _silu(conv)
    xs = xc[:, 0:d_ssd]

    lane = lax.broadcasted_iota(jnp.int32, (1, LANES), 1)
    a_row = jnp.where(lane < n_heads, -jnp.exp(alog_ref[...]), 0.0)
    dt = _softplus(g_ref[0][:, 0:LANES] + dtb_ref[...])
    dec = jnp.exp(dt * a_row)
    both = _dot_sel_r(jnp.concatenate([jnp.broadcast_to(dt, (SUBLANES, LANES)),
                                       jnp.broadcast_to(dec, (SUBLANES, LANES))], axis=0), exp_ref[...])
    dt_full = both[0:1]
    dec_full = both[SUBLANES:SUBLANES + 1]
    dtx = dt_full * xs
    dtx_col = _col_bcast(dtx, d_ssd)
    dec_col = _col_bcast(dec_full, d_ssd)

    y_parts = []
    for g in range(SSD_GROUPS):
        b_g = xc[:, d_ssd + g * SSD_STATE:d_ssd + (g + 1) * SSD_STATE]
        c_g = xc[:, d_ssd + (SSD_GROUPS + g) * SSD_STATE:d_ssd + (SSD_GROUPS + g + 1) * SSD_STATE]
        h_g = st_ref[0, g * gw:(g + 1) * gw, :]
        cb = jnp.sum(c_g * b_g, axis=-1, keepdims=True)
        c8 = jnp.broadcast_to(c_g, (SUBLANES, SSD_STATE)).astype(BF16)
        y_int = _dot_nt(c8, h_g.astype(BF16))[0:1]
        sl = slice(g * gw, (g + 1) * gw)
        y_parts.append(cb * dtx[:, sl] + dec_full[:, sl] * y_int)
        st_out_ref[0, sl, :] = h_g * dec_col[sl, :] + dtx_col[sl, :] * b_g
    y = jnp.concatenate(y_parts, axis=1) + dsk_ref[...] * xs
    y = y * _silu(z_ref[0])
    y = y * lax.rsqrt(jnp.mean(y * y, axis=-1, keepdims=True) + LN_EPS) * ng_ref[...]
    y_ref[0] = y.astype(y_ref.dtype)


def _ssd_step(xbc, z, gates, buf, state, p):
    nb, cd = xbc.shape
    d_ssd = z.shape[1]
    n_heads = d_ssd // SSD_HEAD_DIM
    row3 = lambda b: (b, 0, 0)
    cst = lambda b: (0, 0)
    st2 = state.reshape(nb, d_ssd, SSD_STATE)
    sb = STEP_SEQS
    assert nb % sb == 0
    y, st_new = pl.pallas_call(
        functools.partial(_per_sequence(_ssd_step_one, 5, 2), d_ssd=d_ssd, n_heads=n_heads),
        grid=(nb // sb,),
        in_specs=[pl.BlockSpec((sb, 1, cd), row3),
                  pl.BlockSpec((sb, 1, d_ssd), row3),
                  pl.BlockSpec((sb, 1, 2 * LANES), row3),
                  pl.BlockSpec((sb, CONV_K - 1, cd), row3),
                  pl.BlockSpec((sb, d_ssd, SSD_STATE), row3),
                  pl.BlockSpec((CONV_K, cd), cst),
                  pl.BlockSpec((1, cd), cst),
                  pl.BlockSpec((1, LANES), cst),
                  pl.BlockSpec((1, LANES), cst),
                  pl.BlockSpec((1, d_ssd), cst),
                  pl.BlockSpec((1, d_ssd), cst),
                  pl.BlockSpec((LANES, d_ssd), cst)],
        out_specs=[pl.BlockSpec((sb, 1, d_ssd), row3),
                   pl.BlockSpec((sb, d_ssd, SSD_STATE), row3)],
        out_shape=[jax.ShapeDtypeStruct((nb, 1, d_ssd), BF16),
                   jax.ShapeDtypeStruct((nb, d_ssd, SSD_STATE), F32)],
        compiler_params=_cparams("arbitrary"),
        name="ssd_step",
    )(xbc.reshape(nb, 1, cd), z.reshape(nb, 1, d_ssd), gates.reshape(nb, 1, 2 * LANES), buf, st2,
      p["conv_ssd_w"], p["conv_ssd_b"], p["dt_bias_row"], p["a_log_row"], p["d_skip_row"], p["ssd_norm_g"],
      p["ssd_expand"])
    return y.reshape(nb, d_ssd), st_new.reshape(state.shape)


def _mlstm_step_one(u_ref, v_ref, o_ref, g_ref, buf_ref, c_ref, n_ref, m_ref, cw_ref, cb_ref, wq_ref, wk_ref,
                    bi_ref, bf_ref, ng_ref, hexp_ref, y_ref, c_out_ref, n_out_ref, m_out_ref, *, n_heads, dk):
    conv = cb_ref[...] + cw_ref[CONV_K - 1:CONV_K, :] * u_ref[0]
    for j in range(CONV_K - 1):
        conv = conv + cw_ref[j:j + 1, :] * buf_ref[0, j:j + 1, :]
    uc = jnp.broadcast_to(_silu(conv), (SUBLANES, conv.shape[1])).astype(BF16)

    gt = g_ref[0]
    logi = gt[:, 0:LANES] + bi_ref[...]
    logf = _log_sigmoid(gt[:, LANES:2 * LANES] + bf_ref[...])
    m_prev = m_ref[0]
    inter = logf + m_prev
    m_t = jnp.maximum(inter, logi)
    w_inter = jnp.exp(inter - m_t)
    w_in = jnp.exp(logi - m_t)
    floor = jnp.exp(-m_t)
    rows = jnp.concatenate([w_inter, w_in, floor, jnp.zeros((SUBLANES - 3, LANES), F32)], axis=0)
    full = _dot_sel_r(rows, hexp_ref[...])
    scale = dk ** -0.5

    for h in range(n_heads):
        sl = slice(h * dk, (h + 1) * dk)
        q = _dot(uc[:, sl], wq_ref[h])[0:1]
        k = _dot(uc[:, sl], wk_ref[h])[0:1] * scale
        v_h = v_ref[0][:, sl]
        wi = full[0:1, sl]
        ws = full[1:2, sl]
        fl = full[2:3, sl]
        c_h = c_ref[0, h]
        n_h = n_ref[0, h:h + 1, :]
        att = jnp.sum(q * k, axis=-1, keepdims=True) * ws
        q8 = jnp.broadcast_to(q, (SUBLANES, dk)).astype(BF16)
        num = att * v_h + wi * _dot(q8, c_h.astype(BF16))[0:1]
        den = att + wi * jnp.sum(q * n_h, axis=-1, keepdims=True)
        hout = num / jnp.maximum(jnp.abs(den), fl)
        mu = jnp.mean(hout, axis=-1, keepdims=True)
        var = jnp.mean(jnp.square(hout - mu), axis=-1, keepdims=True)
        hn = (hout - mu) * lax.rsqrt(var + LN_EPS) * ng_ref[:, sl]
        y_ref[0, :, sl] = (_sigmoid(o_ref[0][:, sl]) * hn).astype(y_ref.dtype)
        kw = k * ws
        kw_col = _col_bcast(kw, dk)
        wi_full = jnp.broadcast_to(wi, (dk, dk))
        for half in range(dk // LANES):
            hs = slice(half * LANES, (half + 1) * LANES)
            c_out_ref[0, h, :, hs] = wi_full[:, hs] * c_h[:, hs] + kw_col * v_h[:, hs]
        n_out_ref[0, h:h + 1, :] = wi * n_h + kw
    m_out_ref[0] = m_t


def _mlstm_step(u, v, o, gates, buf, c_st, n_st, m_row, p):
    nb, dm = u.shape
    n_heads, dk, _ = p["w_mq"].shape
    row3 = lambda b: (b, 0, 0)
    cst = lambda b: (0, 0)
    cst3 = lambda b: (0, 0, 0)
    r3 = lambda a: a.reshape(nb, 1, a.shape[-1])
    sb = STEP_SEQS
    assert nb % sb == 0
    y, c_new, n_new, m_new = pl.pallas_call(
        functools.partial(_per_sequence(_mlstm_step_one, 8, 4), n_heads=n_heads, dk=dk),
        grid=(nb // sb,),
        in_specs=[pl.BlockSpec((sb, 1, dm), row3),
                  pl.BlockSpec((sb, 1, dm), row3),
                  pl.BlockSpec((sb, 1, dm), row3),
                  pl.BlockSpec((sb, 1, 2 * LANES), row3),
                  pl.BlockSpec((sb, CONV_K - 1, dm), row3),
                  pl.BlockSpec((sb, n_heads, dk, dk), lambda b: (b, 0, 0, 0)),
                  pl.BlockSpec((sb, n_heads, dk), row3),
                  pl.BlockSpec((sb, 1, LANES), row3),
                  pl.BlockSpec((CONV_K, dm), cst),
                  pl.BlockSpec((1, dm), cst),
                  pl.BlockSpec((n_heads, dk, dk), cst3),
                  pl.BlockSpec((n_heads, dk, dk), cst3),
                  pl.BlockSpec((1, LANES), cst),
                  pl.BlockSpec((1, LANES), cst),
                  pl.BlockSpec((1, dm), cst),
                  pl.BlockSpec((LANES, dm), cst)],
        out_specs=[pl.BlockSpec((sb, 1, dm), row3),
                   pl.BlockSpec((sb, n_heads, dk, dk), lambda b: (b, 0, 0, 0)),
                   pl.BlockSpec((sb, n_heads, dk), row3),
                   pl.BlockSpec((sb, 1, LANES), row3)],
        out_shape=[jax.ShapeDtypeStruct((nb, 1, dm), BF16),
                   jax.ShapeDtypeStruct(c_st.shape, F32),
                   jax.ShapeDtypeStruct(n_st.shape, F32),
                   jax.ShapeDtypeStruct((nb, 1, LANES), F32)],
        compiler_params=_cparams("arbitrary"),
        name="mlstm_step",
    )(r3(u), r3(v), r3(o), r3(gates), buf, c_st, n_st, m_row, p["conv_m_w"], p["conv_m_b"], p["w_mq"],
      p["w_mk"], p["b_i_row"], p["b_f_row"], p["mlstm_norm_g"], p["mlstm_expand"])
    return y.reshape(nb, dm), c_new, n_new, m_new


def _outproj_kernel(ys_ref, hm_ref, x_ref, w_ref, g_ref, b_ref, x1t_ref, *, alpha):
    d_ssd = ys_ref.shape[1]
    mix = _dot(ys_ref[...], w_ref[0:d_ssd, :]) + _dot(hm_ref[...], w_ref[d_ssd:, :])
    r = alpha * x_ref[...] + mix
    mu = jnp.mean(r, axis=-1, keepdims=True)
    var = jnp.mean(jnp.square(r - mu), axis=-1, keepdims=True)
    x1 = (r - mu) * lax.rsqrt(var + LN_EPS) * g_ref[...] + b_ref[...]
    x1t_ref[...] = x1.T


def _outproj(ys, hm, x2, p, alpha, tm):
    m, d = x2.shape
    d_ssd, d_m = ys.shape[1], hm.shape[1]
    return pl.pallas_call(
        functools.partial(_outproj_kernel, alpha=alpha),
        grid=(m // tm,),
        in_specs=[pl.BlockSpec((tm, d_ssd), lambda i: (i, 0)),
                  pl.BlockSpec((tm, d_m), lambda i: (i, 0)),
                  pl.BlockSpec((tm, d), lambda i: (i, 0)),
                  pl.BlockSpec((d_ssd + d_m, d), lambda i: (0, 0)),
                  pl.BlockSpec((1, d), lambda i: (0, 0)),
                  pl.BlockSpec((1, d), lambda i: (0, 0))],
        out_specs=pl.BlockSpec((d, tm), lambda i: (0, i)),
        out_shape=jax.ShapeDtypeStruct((d, m), F32),
        compiler_params=_cparams("arbitrary"),
        name="outproj_ln1",
    )(ys, hm, x2, p["w_out"], p["ln1_g"], p["ln1_b"])


def _oddeven_merge_sort_pairs(n):
    pairs = []
    p = 1
    while p < n:
        k = p
        while k >= 1:
            for j in range(k % p, n - k, 2 * k):
                for i in range(min(k, n - j - k)):
                    if (i + j) // (2 * p) == (i + j + k) // (2 * p):
                        pairs.append((i + j, i + j + k))
            k //= 2
        p *= 2
    return pairs


_SORT_TOPK = _oddeven_merge_sort_pairs(PEER_TOPK)


def _compare_exchange(lst, i, j):
    hi, lo = jnp.maximum(lst[i], lst[j]), jnp.minimum(lst[i], lst[j])
    lst[i], lst[j] = hi, lo


def _bitonic_to_sorted(t):
    d = PEER_TOPK // 2
    while d >= 1:
        for i in range(PEER_TOPK):
            if i & d == 0:
                _compare_exchange(t, i, i + d)
        d //= 2
    return t


def _merge_top(x, y):
    neg = jnp.full_like(x[0], -jnp.inf)
    x = x + [neg] * (PEER_TOPK - len(x))
    y = y + [neg] * (PEER_TOPK - len(y))
    return _bitonic_to_sorted([jnp.maximum(x[k], y[PEER_TOPK - 1 - k]) for k in range(PEER_TOPK)])


def _top_rows(s):
    lst = [s[SUBLANES * g:SUBLANES * (g + 1), :] for g in range(PEER_TOPK)]
    for i, j in _SORT_TOPK:
        _compare_exchange(lst, i, j)
    shift = SUBLANES // 2
    while shift >= 1:
        lst = _merge_top(lst, [pltpu.roll(a, shift, 0) for a in lst])
        shift //= 2
    return lst


def _peer_topk_kernel(x1t_ref, wqt_ref, keys_ref, s1_ref, s2_ref, st_ref, b_scr, q_scr, a_scr, *, n_heads):
    tq = x1t_ref.shape[1]
    q_scr[...] = _dot(wqt_ref[...], x1t_ref[...].astype(BF16))
    half = q_scr.shape[0] // (2 * n_heads)

    for h in range(n_heads):
        for side, (s_ref, top_scr) in enumerate(((s1_ref, a_scr), (s2_ref, b_scr))):
            r0 = (2 * h + side) * half
            s = _dot(keys_ref[h, side], q_scr[r0:r0 + half, :].astype(BF16))
            for t in range(tq // LANES):
                ls = slice(t * LANES, (t + 1) * LANES)
                s_ref[h, t] = s[:, ls]
                top = _top_rows(s[:, ls])
                for k in range(PEER_TOPK):
                    top_scr[k, h:h + 1, ls] = top[k][0:1, :]

    for t in range(tq // LANES):
        ls = slice(t * LANES, (t + 1) * LANES)
        a = [a_scr[k, :, ls] for k in range(PEER_TOPK)]
        b = [b_scr[k, :, ls] for k in range(PEER_TOPK)]
        single = PEER_TOPK // 2
        lists = [[a[i] + b[j] for j in range(PEER_TOPK // (i + 1))] for i in range(single)]
        lists.append([a[i] + b[0] for i in range(single, PEER_TOPK)])
        merged = lists[0]
        for other in lists[1:]:
            merged = _merge_top(merged, other)
        tau = merged[PEER_TOPK - 1]
        top = a[0] + b[0]
        zsum = jnp.zeros_like(tau)
        for cand in (c for lst in lists for c in lst):
            zsum = zsum + jnp.where(cand >= tau, jnp.exp(cand - top), 0.0)
        st_ref[0, :, ls] = tau
        st_ref[1, :, ls] = a[0]
        st_ref[2, :, ls] = b[0]
        st_ref[3, :, ls] = 1.0 / zsum


def _peer_topk(x1t, p, tq):
    d, m = x1t.shape
    n_heads, _, n_keys, half = p["peer_keys"].shape
    nq = p["peer_wqT"].shape[0]
    assert n_heads == SUBLANES and n_keys == PEER_TOPK * SUBLANES
    keyb = lambda i: (0, i, 0, 0)
    return pl.pallas_call(
        functools.partial(_peer_topk_kernel, n_heads=n_heads),
        grid=(m // tq,),
        in_specs=[pl.BlockSpec((d, tq), lambda i: (0, i)),
                  pl.BlockSpec((nq, d), lambda i: (0, 0)),
                  pl.BlockSpec((n_heads, 2, n_keys, half), lambda i: (0, 0, 0, 0))],
        out_specs=[pl.BlockSpec((n_heads, tq // LANES, n_keys, LANES), keyb),
                   pl.BlockSpec((n_heads, tq // LANES, n_keys, LANES), keyb),
                   pl.BlockSpec((N_STATS, SUBLANES, tq), lambda i: (0, 0, i)),
                   pl.BlockSpec((PEER_TOPK, SUBLANES, tq), lambda i: (0, 0, i))],
        out_shape=[jax.ShapeDtypeStruct((n_heads, m // LANES, n_keys, LANES), F32),
                   jax.ShapeDtypeStruct((n_heads, m // LANES, n_keys, LANES), F32),
                   jax.ShapeDtypeStruct((N_STATS, SUBLANES, m), F32),
                   jax.ShapeDtypeStruct((PEER_TOPK, SUBLANES, m), F32)],
        scratch_shapes=[pltpu.VMEM((nq, tq), F32),
                        pltpu.VMEM((PEER_TOPK, SUBLANES, tq), F32)],
        compiler_params=_cparams("arbitrary"),
        name="peer_topk",
    )(x1t, p["peer_wqT"], p["peer_keys"])


def _peer_dense_kernel(x1t_ref, s1_ref, s2_ref, st_ref, bl_ref, u_ref, vt_ref, g_ref, b_ref, out_ref,
                       xb_scr, e2_scr, c_scr, thr_scr, acc_scr, s_scr, w_scr, thr_r, cr, *, alpha, n_heads):
    j = pl.program_id(1)
    ec = u_ref.shape[0]
    tb = x1t_ref.shape[1]
    nt = tb // LANES
    n_keys = s1_ref.shape[2]
    na = ec // n_keys

    @pl.when(j == 0)
    def _():
        xb_scr[...] = x1t_ref[...].astype(BF16)
        for h in range(n_heads):
            for t in range(nt):
                ls = slice(t * LANES, (t + 1) * LANES)
                e2_scr[h, t] = jnp.exp(s2_ref[h, t] - st_ref[2, h:h + 1, ls])
                s1 = s1_ref[h, t]
                c_scr[h, t] = jnp.exp(s1 - st_ref[1, h:h + 1, ls]) * (0.5 * st_ref[3, h:h + 1, ls])
                tau = st_ref[0, h:h + 1, ls]
                thr = jnp.full((n_keys, LANES), jnp.inf, F32)
                for k in range(PEER_TOPK):
                    bk = bl_ref[k, h:h + 1, ls]
                    thr = jnp.where(s1 + bk >= tau, bk, thr)
                thr_scr[h, t] = thr
        acc_scr[...] = jnp.zeros_like(acc_scr)

    s_scr[:, 0:tb] = _dot(u_ref[...], xb_scr[...])

    g0 = pl.multiple_of(j * na, na)
    for h in range(n_heads):
        for t in range(nt):
            ls = slice(t * LANES, (t + 1) * LANES)
            thr_blk = thr_scr[h, t, pl.ds(g0, na), :]
            c_blk = c_scr[h, t, pl.ds(g0, na), :]
            for al in range(na):
                thr_r[h * na + al, :, ls] = thr_blk[al:al + 1, :]
                cr[h * na + al, :, ls] = c_blk[al:al + 1, :]

    def first_key(al, carry):
        r0 = pl.multiple_of(al * n_keys, n_keys)
        for t in range(nt):
            ls = slice(t * LANES, (t + 1) * LANES)
            gate = jnp.zeros((n_keys, LANES), F32)
            for h in range(n_heads):
                gate = gate + jnp.where(s2_ref[h, t] >= thr_r[h * na + al][:, ls],
                                        e2_scr[h, t] * cr[h * na + al][:, ls], 0.0)
            x = s_scr[pl.ds(r0, n_keys), ls]
            hid = x + x * lax.erf(x * (1.0 / math.sqrt(2.0)))
            w_scr[pl.ds(r0, n_keys), ls] = (gate * hid).astype(BF16)
        return carry

    lax.fori_loop(0, na, first_key, 0)
    acc_scr[...] += _dot(vt_ref[...], w_scr[:, 0:tb])

    @pl.when(j == pl.num_programs(1) - 1)
    def _():
        r = alpha * x1t_ref[...] + acc_scr[...]
        mu = jnp.mean(r, axis=0, keepdims=True)
        var = jnp.mean(jnp.square(r - mu), axis=0, keepdims=True)
        y = (r - mu) * lax.rsqrt(var + LN_EPS) * g_ref[...] + b_ref[...]
        out_ref[...] = y.T


def _peer_dense(x1t, s1, s2, st, blist, p, alpha, tb):
    d, m = x1t.shape
    n_heads, _, n_keys, _ = s1.shape
    n_exp = p["peer_u"].shape[0]
    ec = PEER_CHUNK
    nj = n_exp // ec
    na = ec // n_keys
    assert na == SUBLANES
    nt = tb // LANES
    keyb = lambda i, j: (0, i, 0, 0)
    tokb = lambda i, j: (0, 0, i)
    pitch = tb + LANES
    return pl.pallas_call(
        functools.partial(_peer_dense_kernel, alpha=alpha, n_heads=n_heads),
        grid=(m // tb, nj),
        in_specs=[pl.BlockSpec((d, tb), lambda i, j: (0, i)),
                  pl.BlockSpec((n_heads, nt, n_keys, LANES), keyb),
                  pl.BlockSpec((n_heads, nt, n_keys, LANES), keyb),
                  pl.BlockSpec((N_STATS, SUBLANES, tb), tokb),
                  pl.BlockSpec((PEER_TOPK, SUBLANES, tb), tokb),
                  pl.BlockSpec((ec, d), lambda i, j: (j, 0)),
                  pl.BlockSpec((d, ec), lambda i, j: (0, j)),
                  pl.BlockSpec((d, 1), lambda i, j: (0, 0)),
                  pl.BlockSpec((d, 1), lambda i, j: (0, 0))],
        out_specs=pl.BlockSpec((tb, d), lambda i, j: (i, 0)),
        out_shape=jax.ShapeDtypeStruct((m, d), F32),
        scratch_shapes=[pltpu.VMEM((d, tb), BF16),
                        pltpu.VMEM((n_heads, nt, n_keys, LANES), F32),
                        pltpu.VMEM((n_heads, nt, n_keys, LANES), F32),
                        pltpu.VMEM((n_heads, nt, n_keys, LANES), F32),
                        pltpu.VMEM((d, tb), F32),
                        pltpu.VMEM((ec, pitch), F32),
                        pltpu.VMEM((ec, pitch), BF16),
                        pltpu.VMEM((n_heads * na, 1, tb), F32),
                        pltpu.VMEM((n_heads * na, 1, tb), F32)],
        compiler_params=_cparams("arbitrary", "arbitrary"),
        name="peer_dense",
    )(x1t, s1, s2, st, blist, p["peer_u"], p["peer_vT"], p["ln2_g_col"], p["ln2_b_col"])


def _pad_row(vec, offset):
    return jnp.zeros((1, LANES), F32).at[0, offset:offset + vec.shape[0]].set(vec.astype(F32))


def _prep_layer(l, w_in, conv_ssd_w, conv_ssd_b, dt_bias, a_log, d_skip, ssd_norm_g, conv_m_w, conv_m_b,
                w_mq, w_mk, b_igate, b_fgate, mlstm_norm_g, w_out, ln1_g, ln1_b, peer_wq, peer_keys,
                peer_u, peer_v, ln2_g, ln2_b):
    d_model = w_in.shape[1]
    d_ssd = ssd_norm_g.shape[1]
    cd = conv_ssd_w.shape[2]
    n_sh = dt_bias.shape[1]
    d_m = conv_m_w.shape[2]
    n_mh = w_mq.shape[1]
    dk = w_mq.shape[2]
    assert n_sh <= GATE_I0 and GATE_I0 + n_mh <= LANES and d_ssd == n_sh * SSD_HEAD_DIM
    w = w_in[l]
    offs = [0]
    for n in (d_ssd, cd, n_sh, d_m, d_m, d_m, n_mh, n_mh):
        offs.append(offs[-1] + n)
    z_w, xbc_w, dt_w, u_w, v_w, o_w, i_w, f_w = (w[:, offs[k]:offs[k + 1]] for k in range(8))
    gate_a = jnp.zeros((d_model, LANES), F32).at[:, 0:n_sh].set(dt_w).at[:, GATE_I0:GATE_I0 + n_mh].set(i_w)
    gate_b = jnp.zeros((d_model, LANES), F32).at[:, GATE_I0:GATE_I0 + n_mh].set(f_w)
    w_cat = jnp.concatenate([z_w, xbc_w, u_w, v_w, o_w, gate_a, gate_b], axis=1).astype(BF16)
    widths = (d_ssd, cd, d_m, d_m, d_m, 2 * LANES)

    lane = jnp.arange(LANES)[:, None]
    ssd_expand = (lane == (jnp.arange(d_ssd)[None, :] // SSD_HEAD_DIM)).astype(BF16)
    mlstm_expand = (lane == (GATE_I0 + jnp.arange(d_m)[None, :] // dk)).astype(BF16)
    t = jnp.arange(CHUNK)
    tril = (t[:, None] >= t[None, :]).astype(BF16)
    p = {
        "w_cat": w_cat,
        "conv_ssd_w": conv_ssd_w[l], "conv_ssd_b": conv_ssd_b[l][None, :],
        "dt_bias_row": _pad_row(dt_bias[l], 0), "a_log_row": _pad_row(a_log[l], 0),
        "d_skip_row": jnp.repeat(d_skip[l], SSD_HEAD_DIM)[None, :], "ssd_norm_g": ssd_norm_g[l][None, :],
        "conv_m_w": conv_m_w[l], "conv_m_b": conv_m_b[l][None, :],
        "w_mq": w_mq[l].astype(BF16), "w_mk": w_mk[l].astype(BF16),
        "b_i_row": _pad_row(b_igate[l], GATE_I0), "b_f_row": _pad_row(b_fgate[l], GATE_I0),
        "mlstm_norm_g": mlstm_norm_g[l].reshape(1, d_m),
        "w_out": w_out[l].astype(BF16), "ln1_g": ln1_g[l][None, :], "ln1_b": ln1_b[l][None, :],
        "peer_wqT": peer_wq[l].T.astype(BF16), "peer_keys": peer_keys[l].astype(BF16),
        "peer_u": peer_u[l].astype(BF16),
        "peer_vT": peer_v[l].astype(BF16).T,
        "ln2_g_col": ln2_g[l][:, None], "ln2_b_col": ln2_b[l][:, None],
        "tril": tril, "ssd_expand": ssd_expand, "mlstm_expand": mlstm_expand,
    }
    return p, widths


def _tile_sizes(m):
    tm = 256 if m % 256 == 0 else LANES
    tb = 512 if m % 512 == 0 else LANES
    return tm, tb


def _peer_block(x1t, p, alpha):
    m = x1t.shape[1]
    tm, tb = _tile_sizes(m)
    s1, s2, st, blist = _peer_topk(x1t, p, tm)
    return _peer_dense(x1t, s1, s2, st, blist, p, alpha, tb)


def _layer_prompt(x, p, widths, alpha):
    bsz, seq, d = x.shape
    assert seq % CHUNK == 0 and seq >= CONV_K - 1
    x2 = x.reshape(bsz * seq, d)
    tm, _ = _tile_sizes(bsz * seq)
    z, xbc, u, v, o, gates = _inproj(x2, p["w_cat"], widths, tm)
    y_s, h_t = _ssd_chunk(xbc, z, gates, p, bsz, seq)
    h_m, c_new, n_new, m_new = _mlstm_chunk(u, v, o, gates, p, bsz, seq)
    x1t = _outproj(y_s, h_m, x2, p, alpha, tm)
    y = _peer_block(x1t, p, alpha).reshape(bsz, seq, d)
    n_mh = c_new.shape[1]
    d_ssd = z.shape[1]
    n_sh = d_ssd // SSD_HEAD_DIM
    hg = n_sh // SSD_GROUPS
    new_ssd = h_t.reshape(bsz, SSD_GROUPS, SSD_STATE, hg, SSD_HEAD_DIM).transpose(0, 1, 3, 4, 2)
    new_ssd = new_ssd.reshape(bsz, n_sh, SSD_HEAD_DIM, SSD_STATE)
    states = (new_ssd,
              xbc.reshape(bsz, seq, -1)[:, seq - (CONV_K - 1):, :],
              c_new,
              n_new[:, :, 0, :],
              m_new[:, 0, GATE_I0:GATE_I0 + n_mh],
              u.reshape(bsz, seq, -1)[:, seq - (CONV_K - 1):, :])
    return y, states


def _layer_sample(x, st, p, widths, alpha):
    st_ssd, st_ssd_conv, st_c, st_n, st_m, st_mconv = st
    nb, seq, d = x.shape
    assert seq == 1 and nb % LANES == 0
    x2 = x.reshape(nb, d)
    z, xbc, u, v, o, gates = _inproj(x2, p["w_cat"], widths, LANES)
    y_s, new_ssd = _ssd_step(xbc, z, gates, st_ssd_conv, st_ssd, p)
    n_mh = st_c.shape[1]
    m_row = jnp.zeros((nb, 1, LANES), F32).at[:, 0, GATE_I0:GATE_I0 + n_mh].set(st_m)
    h_m, c_new, n_new, m_new = _mlstm_step(u, v, o, gates, st_mconv, st_c, st_n, m_row, p)
    x1t = _outproj(y_s, h_m, x2, p, alpha, LANES)
    y = _peer_block(x1t, p, alpha).reshape(nb, seq, d)
    states = (new_ssd,
              jnp.concatenate([st_ssd_conv[:, 1:, :], xbc[:, None, :]], axis=1),
              c_new,
              n_new,
              m_new[:, 0, GATE_I0:GATE_I0 + n_mh],
              jnp.concatenate([st_mconv[:, 1:, :], u[:, None, :]], axis=1))
    return y, states


def kernel(x_prompt, x_sample, state_ssd, state_ssd_conv, state_mlstm_c, state_mlstm_n, state_mlstm_m,
           state_mlstm_conv, w_in, conv_ssd_w, conv_ssd_b, dt_bias, a_log, d_skip, ssd_norm_g, conv_m_w,
           conv_m_b, w_mq, w_mk, b_igate, b_fgate, mlstm_norm_g, w_out, ln1_g, ln1_b, peer_wq, peer_keys,
           peer_u, peer_v, ln2_g, ln2_b):
    depth = w_in.shape[0]
    alpha = (2.0 * depth) ** 0.25
    states = (state_ssd, state_ssd_conv, state_mlstm_c, state_mlstm_n, state_mlstm_m, state_mlstm_conv)
    yp, ys = x_prompt, x_sample
    p_list, s_list = [], []
    for l in range(depth):
        p, widths = _prep_layer(l, w_in, conv_ssd_w, conv_ssd_b, dt_bias, a_log, d_skip, ssd_norm_g, conv_m_w,
                                conv_m_b, w_mq, w_mk, b_igate, b_fgate, mlstm_norm_g, w_out, ln1_g, ln1_b,
                                peer_wq, peer_keys, peer_u, peer_v, ln2_g, ln2_b)
        yp, p_new = _layer_prompt(yp, p, widths, alpha)
        ys, s_new = _layer_sample(ys, tuple(s[l] for s in states), p, widths, alpha)
        p_list.append(p_new)
        s_list.append(s_new)
    p_out = tuple(jnp.stack([pn[i] for pn in p_list]) for i in range(6))
    s_out = tuple(jnp.stack([sn[i] for sn in s_list]) for i in range(6))
    return (yp, ys) + p_out + s_out
```

```python
import functools
import math

import jax
import jax.numpy as jnp
from jax import lax
from jax.experimental import pallas as pl
from jax.experimental.pallas import tpu as pltpu

F32 = jnp.float32
BF16 = jnp.bfloat16

LANES = 128
SUBLANES = 8
CHUNK = 128
CONV_K = 4
LN_EPS = 1e-5
VMEM_LIMIT = 52 * 1024 * 1024

SSD_HEAD_DIM = 64
SSD_STATE = 128
SSD_GROUPS = 2
PEER_TOPK = 16
GATE_I0 = 16
N_STATS = 4
PEER_CHUNK = 1024
STEP_SEQS = 4
OUTPROJ_ROWS = 256


def _cparams(*sem):
    return pltpu.CompilerParams(dimension_semantics=sem, vmem_limit_bytes=VMEM_LIMIT)


def _dot(a, b):
    return jnp.dot(a, b, preferred_element_type=F32)


def _dot_nt(a, b):
    return lax.dot_general(a, b, (((1,), (1,)), ((), ())), preferred_element_type=F32)


def _split3(x):
    h = x.astype(BF16)
    r = x - h.astype(F32)
    m = r.astype(BF16)
    lo = (r - m.astype(F32)).astype(BF16)
    return h, m, lo


def _dot_sel_l(sel_bf16, x):
    h, m, lo = _split3(x)
    return _dot(sel_bf16, h) + _dot(sel_bf16, m) + _dot(sel_bf16, lo)


def _dot_sel_r(x, sel_bf16):
    h, m, lo = _split3(x)
    return _dot(h, sel_bf16) + _dot(m, sel_bf16) + _dot(lo, sel_bf16)


def _sigmoid(x):
    return 1.0 / (1.0 + jnp.exp(-x))


def _silu(x):
    return x * _sigmoid(x)


def _softplus(x):
    return jnp.maximum(x, 0.0) + jnp.log1p(jnp.exp(-jnp.abs(x)))


def _log_sigmoid(x):
    return -_softplus(-x)


def _gelu_exact(x):
    return 0.5 * x * (1.0 + lax.erf(x * (1.0 / math.sqrt(2.0))))


def _inproj_kernel(x_ref, w_ref, *out_refs):
    xb = x_ref[...].astype(BF16)
    off = 0
    for ref in out_refs:
        n = ref.shape[-1]
        ref[...] = _dot(xb, w_ref[:, off:off + n])
        off += n


def _inproj(x2, w_cat, widths, tm):
    m, d = x2.shape
    n_all = w_cat.shape[1]
    return pl.pallas_call(
        _inproj_kernel,
        grid=(m // tm,),
        in_specs=[pl.BlockSpec((tm, d), lambda i: (i, 0)),
                  pl.BlockSpec((d, n_all), lambda i: (0, 0))],
        out_specs=[pl.BlockSpec((tm, n), lambda i: (i, 0)) for n in widths],
        out_shape=[jax.ShapeDtypeStruct((m, n), F32) for n in widths],
        compiler_params=_cparams("arbitrary"),
        name="inproj",
    )(x2, w_cat)


def _ssd_chunk_kernel(xbc_ref, z_ref, g_ref, cw_ref, cb_ref, dtb_ref, alog_ref, dsk_ref, ng_ref,
                      tril_ref, exp_ref, y_ref, hT_out_ref, xpad, hT, ybuf, *, nc, d_ssd, n_heads):
    c = pl.program_id(1)
    cd = xpad.shape[1]
    hg = n_heads // SSD_GROUPS
    gw = d_ssd // SSD_GROUPS

    @pl.when(c == 0)
    def _():
        xpad[0:SUBLANES, :] = jnp.zeros((SUBLANES, cd), F32)
        hT[...] = jnp.zeros_like(hT)

    xpad[SUBLANES:SUBLANES + CHUNK, :] = xbc_ref[...]
    conv = cb_ref[...]
    for j in range(CONV_K):
        r0 = SUBLANES - (CONV_K - 1) + j
        conv = conv + cw_ref[j:j + 1, :] * xpad[r0:r0 + CHUNK, :]
    xpad[0:SUBLANES, :] = xpad[CHUNK:CHUNK + SUBLANES, :]
    xc = _silu(conv)
    xs = xc[:, 0:d_ssd]

    lane = lax.broadcasted_iota(jnp.int32, (1, LANES), 1)
    a_row = jnp.where(lane < n_heads, -jnp.exp(alog_ref[...]), 0.0)
    dt = _softplus(g_ref[:, 0:LANES] + dtb_ref[...])
    acs = _dot_sel_l(tril_ref[...], dt * a_row)
    dt_t = dt.T
    acs_t = acs.T
    eacs = jnp.exp(acs)
    dec = jnp.exp(acs[CHUNK - 1:CHUNK, :] - acs) * dt
    both = _dot_sel_r(jnp.concatenate([eacs, dec], axis=0), exp_ref[...])
    eacs_full = both[0:CHUNK]
    dec_full = both[CHUNK:2 * CHUNK]

    row = lax.broadcasted_iota(jnp.int32, (CHUNK, CHUNK), 0)
    col = lax.broadcasted_iota(jnp.int32, (CHUNK, CHUNK), 1)
    causal = row >= col

    for g in range(SSD_GROUPS):
        b_g = xc[:, d_ssd + g * SSD_STATE:d_ssd + (g + 1) * SSD_STATE]
        c_g = xc[:, d_ssd + (SSD_GROUPS + g) * SSD_STATE:d_ssd + (SSD_GROUPS + g + 1) * SSD_STATE]
        b_gb = b_g.astype(BF16)
        c_gb = c_g.astype(BF16)
        cb = _dot_nt(c_gb, b_gb)
        h_g = hT[g]
        y_inter = _dot(c_gb, h_g.astype(BF16)) * eacs_full[:, g * gw:(g + 1) * gw]
        for hl in range(hg):
            h = g * hg + hl
            diff = acs[:, h:h + 1] - acs_t[h:h + 1, :]
            lmat = jnp.exp(jnp.where(causal, diff, -jnp.inf))
            mm = cb * lmat * dt_t[h:h + 1, :]
            x_h = xs[:, h * SSD_HEAD_DIM:(h + 1) * SSD_HEAD_DIM]
            ybuf[:, h * SSD_HEAD_DIM:(h + 1) * SSD_HEAD_DIM] = _dot(mm.astype(BF16), x_h.astype(BF16))
        xd = xs[:, g * gw:(g + 1) * gw] * dec_full[:, g * gw:(g + 1) * gw]
        hT[g] = h_g * eacs_full[CHUNK - 1:CHUNK, g * gw:(g + 1) * gw] + _dot(b_g.T.astype(BF16), xd.astype(BF16))
        ybuf[:, g * gw:(g + 1) * gw] = ybuf[:, g * gw:(g + 1) * gw] + y_inter

    y = ybuf[...] + dsk_ref[...] * xs
    y = y * _silu(z_ref[...])
    y = y * lax.rsqrt(jnp.mean(y * y, axis=-1, keepdims=True) + LN_EPS) * ng_ref[...]
    y_ref[...] = y.astype(y_ref.dtype)

    @pl.when(c == nc - 1)
    def _():
        hT_out_ref[0] = hT[...]


def _ssd_chunk(xbc, z, gates, p, bsz, seq):
    nc = seq // CHUNK
    cd = xbc.shape[1]
    d_ssd = z.shape[1]
    n_heads = d_ssd // SSD_HEAD_DIM
    gw = d_ssd // SSD_GROUPS
    tok = lambda b, c: (b * nc + c, 0)
    cst = lambda b, c: (0, 0)
    return pl.pallas_call(
        functools.partial(_ssd_chunk_kernel, nc=nc, d_ssd=d_ssd, n_heads=n_heads),
        grid=(bsz, nc),
        in_specs=[pl.BlockSpec((CHUNK, cd), tok),
                  pl.BlockSpec((CHUNK, d_ssd), tok),
                  pl.BlockSpec((CHUNK, 2 * LANES), tok),
                  pl.BlockSpec((CONV_K, cd), cst),
                  pl.BlockSpec((1, cd), cst),
                  pl.BlockSpec((1, LANES), cst),
                  pl.BlockSpec((1, LANES), cst),
                  pl.BlockSpec((1, d_ssd), cst),
                  pl.BlockSpec((1, d_ssd), cst),
                  pl.BlockSpec((CHUNK, CHUNK), cst),
                  pl.BlockSpec((LANES, d_ssd), cst)],
        out_specs=[pl.BlockSpec((CHUNK, d_ssd), tok),
                   pl.BlockSpec((1, SSD_GROUPS, SSD_STATE, gw), lambda b, c: (b, 0, 0, 0))],
        out_shape=[jax.ShapeDtypeStruct((bsz * seq, d_ssd), BF16),
                   jax.ShapeDtypeStruct((bsz, SSD_GROUPS, SSD_STATE, gw), F32)],
        scratch_shapes=[pltpu.VMEM((CHUNK + SUBLANES, cd), F32),
                        pltpu.VMEM((SSD_GROUPS, SSD_STATE, gw), F32),
                        pltpu.VMEM((CHUNK, d_ssd), F32)],
        compiler_params=_cparams("arbitrary", "arbitrary"),
        name="ssd_chunk",
    )(xbc, z, gates, p["conv_ssd_w"], p["conv_ssd_b"], p["dt_bias_row"], p["a_log_row"],
      p["d_skip_row"], p["ssd_norm_g"], p["tril"], p["ssd_expand"])


def _mlstm_chunk_kernel(u_ref, v_ref, o_ref, g_ref, cw_ref, cb_ref, wq_ref, wk_ref, bi_ref, bf_ref, ng_ref,
                        tril_ref, y_ref, c_out_ref, n_out_ref, m_out_ref, upad, c_st, n_st, m_st,
                        *, nc, n_heads, dk):
    c = pl.program_id(1)
    dm = upad.shape[1]

    @pl.when(c == 0)
    def _():
        upad[0:SUBLANES, :] = jnp.zeros((SUBLANES, dm), F32)
        c_st[...] = jnp.zeros_like(c_st)
        n_st[...] = jnp.zeros_like(n_st)
        m_st[...] = jnp.zeros_like(m_st)

    upad[SUBLANES:SUBLANES + CHUNK, :] = u_ref[...]
    conv = cb_ref[...]
    for j in range(CONV_K):
        r0 = SUBLANES - (CONV_K - 1) + j
        conv = conv + cw_ref[j:j + 1, :] * upad[r0:r0 + CHUNK, :]
    upad[0:SUBLANES, :] = upad[CHUNK:CHUNK + SUBLANES, :]
    uc = _silu(conv).astype(BF16)

    logi = g_ref[:, 0:LANES] + bi_ref[...]
    logf = _log_sigmoid(g_ref[:, LANES:2 * LANES] + bf_ref[...])
    bcs = _dot_sel_l(tril_ref[...], logf)
    m_prev = m_st[0:1, :]
    b_end = bcs[CHUNK - 1:CHUNK, :]
    g_all = b_end - bcs + logi
    m_new = jnp.maximum(b_end + m_prev, jnp.max(g_all, axis=0, keepdims=True))
    w_old = jnp.exp(b_end + m_prev - m_new)
    w_s_all = jnp.exp(g_all - m_new)
    inter_all = bcs + m_prev
    bcs_t = bcs.T
    logi_t = logi.T

    row = lax.broadcasted_iota(jnp.int32, (CHUNK, CHUNK), 0)
    col = lax.broadcasted_iota(jnp.int32, (CHUNK, CHUNK), 1)
    causal = row >= col
    scale = dk ** -0.5

    hs = range(n_heads)
    sls = [slice(h * dk, (h + 1) * dk) for h in hs]
    gls = [GATE_I0 + h for h in hs]
    q = [_dot(uc[:, sls[h]], wq_ref[h]) for h in hs]
    k = [_dot(uc[:, sls[h]], wk_ref[h]) * scale for h in hs]
    qb = [x.astype(BF16) for x in q]
    kb = [x.astype(BF16) for x in k]
    vb = [v_ref[:, sls[h]].astype(BF16) for h in hs]
    dmat = [jnp.where(causal, bcs[:, g:g + 1] - bcs_t[g:g + 1, :] + logi_t[g:g + 1, :], -jnp.inf) for g in gls]
    inter = [inter_all[:, g:g + 1] for g in gls]
    m_t = [jnp.maximum(inter[h], jnp.max(dmat[h], axis=1, keepdims=True)) for h in hs]
    w_inter = [jnp.exp(inter[h] - m_t[h]) for h in hs]
    qk = [_dot_nt(qb[h], kb[h]) for h in hs]
    c_old = [c_st[h] for h in hs]
    n_old = [n_st[h][0:1, :] for h in hs]
    qc = [_dot(qb[h], c_old[h].astype(BF16)) for h in hs]
    att = [qk[h] * jnp.exp(dmat[h] - m_t[h]) for h in hs]
    av = [_dot(att[h].astype(BF16), vb[h]) for h in hs]
    kw = [k[h] * w_s_all[:, gls[h]:gls[h] + 1] for h in hs]
    kv = [_dot(kw[h].T.astype(BF16), vb[h]) for h in hs]
    num = [av[h] + w_inter[h] * qc[h] for h in hs]
    den = [jnp.sum(att[h], axis=1, keepdims=True) + w_inter[h] * jnp.sum(q[h] * n_old[h], axis=1, keepdims=True)
           for h in hs]
    hout = [num[h] / jnp.maximum(jnp.abs(den[h]), jnp.exp(-m_t[h])) for h in hs]
    mu = [jnp.mean(x, axis=-1, keepdims=True) for x in hout]
    cen = [hout[h] - mu[h] for h in hs]
    var = [jnp.mean(jnp.square(x), axis=-1, keepdims=True) for x in cen]
    for h in hs:
        sl = sls[h]
        hn = cen[h] * lax.rsqrt(var[h] + LN_EPS) * ng_ref[:, sl]
        y_ref[:, sl] = (_sigmoid(o_ref[:, sl]) * hn).astype(y_ref.dtype)
        wo = w_old[:, gls[h]:gls[h] + 1]
        c_st[h] = wo * c_old[h] + kv[h]
        n_st[h] = jnp.broadcast_to(wo * n_old[h] + jnp.sum(kw[h], axis=0, keepdims=True), (SUBLANES, dk))

    m_st[...] = jnp.broadcast_to(m_new, (SUBLANES, LANES))

    @pl.when(c == nc - 1)
    def _():
        c_out_ref[0] = c_st[...]
        n_out_ref[0] = n_st[...]
        m_out_ref[0] = m_st[...]


def _mlstm_chunk(u, v, o, gates, p, bsz, seq):
    nc = seq // CHUNK
    dm = u.shape[1]
    n_heads, dk, _ = p["w_mq"].shape
    tok = lambda b, c: (b * nc + c, 0)
    cst = lambda b, c: (0, 0)
    cst3 = lambda b, c: (0, 0, 0)
    return pl.pallas_call(
        functools.partial(_mlstm_chunk_kernel, nc=nc, n_heads=n_heads, dk=dk),
        grid=(bsz, nc),
        in_specs=[pl.BlockSpec((CHUNK, dm), tok),
                  pl.BlockSpec((CHUNK, dm), tok),
                  pl.BlockSpec((CHUNK, dm), tok),
                  pl.BlockSpec((CHUNK, 2 * LANES), tok),
                  pl.BlockSpec((CONV_K, dm), cst),
                  pl.BlockSpec((1, dm), cst),
                  pl.BlockSpec((n_heads, dk, dk), cst3),
                  pl.BlockSpec((n_heads, dk, dk), cst3),
                  pl.BlockSpec((1, LANES), cst),
                  pl.BlockSpec((1, LANES), cst),
                  pl.BlockSpec((1, dm), cst),
                  pl.BlockSpec((CHUNK, CHUNK), cst)],
        out_specs=[pl.BlockSpec((CHUNK, dm), tok),
                   pl.BlockSpec((1, n_heads, dk, dk), lambda b, c: (b, 0, 0, 0)),
                   pl.BlockSpec((1, n_heads, SUBLANES, dk), lambda b, c: (b, 0, 0, 0)),
                   pl.BlockSpec((1, SUBLANES, LANES), lambda b, c: (b, 0, 0))],
        out_shape=[jax.ShapeDtypeStruct((bsz * seq, dm), BF16),
                   jax.ShapeDtypeStruct((bsz, n_heads, dk, dk), F32),
                   jax.ShapeDtypeStruct((bsz, n_heads, SUBLANES, dk), F32),
                   jax.ShapeDtypeStruct((bsz, SUBLANES, LANES), F32)],
        scratch_shapes=[pltpu.VMEM((CHUNK + SUBLANES, dm), F32),
                        pltpu.VMEM((n_heads, dk, dk), F32),
                        pltpu.VMEM((n_heads, SUBLANES, dk), F32),
                        pltpu.VMEM((SUBLANES, LANES), F32)],
        compiler_params=_cparams("arbitrary", "arbitrary"),
        name="mlstm_chunk",
    )(u, v, o, gates, p["conv_m_w"], p["conv_m_b"], p["w_mq"], p["w_mk"], p["b_i_row"], p["b_f_row"],
      p["mlstm_norm_g"], p["tril"])


def _col_bcast(row, n_rows):
    return jnp.broadcast_to(row, (LANES, row.shape[1])).T[0:n_rows, :]


def _per_sequence(one_sequence, n_seq_in, n_seq_out):
    def kernel(*refs, **static):
        seq_in, shared, seq_out = refs[:n_seq_in], refs[n_seq_in:len(refs) - n_seq_out], refs[len(refs) - n_seq_out:]
        for s in range(seq_in[0].shape[0]):
            one = slice(s, s + 1)
            one_sequence(*(r.at[one] for r in seq_in), *shared, *(r.at[one] for r in seq_out), **static)
    return kernel


def _ssd_step_one(xbc_ref, z_ref, g_ref, buf_ref, st_ref, cw_ref, cb_ref, dtb_ref, alog_ref, dsk_ref, ng_ref,
                  exp_ref, y_ref, st_out_ref, *, d_ssd, n_heads):
    gw = d_ssd // SSD_GROUPS
    x_row = xbc_ref[0]
    conv = cb_ref[...] + cw_ref[CONV_K - 1:CONV_K, :] * x_row
    for j in range(CONV_K - 1):
        conv = conv + cw_ref[j:j + 1, :] * buf_ref[0, j:j + 1, :]
    xc = _silu(conv)
    xs = xc[:, 0:d_ssd]

    lane = lax.broadcasted_iota(jnp.int32, (1, LANES), 1)
    a_row = jnp.where(lane < n_heads, -jnp.exp(alog_ref[...]), 0.0)
    dt = _softplus(g_ref[0][:, 0:LANES] + dtb_ref[...])
    dec = jnp.exp(dt * a_row)
    both = _dot_sel_r(jnp.concatenate([jnp.broadcast_to(dt, (SUBLANES, LANES)),
                                       jnp.broadcast_to(dec, (SUBLANES, LANES))], axis=0), exp_ref[...])
    dt_full = both[0:1]
    dec_full = both[SUBLANES:SUBLANES + 1]
    dtx = dt_full * xs
    dtx_col = _col_bcast(dtx, d_ssd)
    dec_col = _col_bcast(dec_full, d_ssd)

    y_parts = []
    for g in range(SSD_GROUPS):
        b_g = xc[:, d_ssd + g * SSD_STATE:d_ssd + (g + 1) * SSD_STATE]
        c_g = xc[:, d_ssd + (SSD_GROUPS + g) * SSD_STATE:d_ssd + (SSD_GROUPS + g + 1) * SSD_STATE]
        h_g = st_ref[0, g * gw:(g + 1) * gw, :]
        cb = jnp.sum(c_g * b_g, axis=-1, keepdims=True)
        c8 = jnp.broadcast_to(c_g, (SUBLANES, SSD_STATE)).astype(BF16)
        y_int = _dot_nt(c8, h_g.astype(BF16))[0:1]
        sl = slice(g * gw, (g + 1) * gw)
        y_parts.append(cb * dtx[:, sl] + dec_full[:, sl] * y_int)
        st_out_ref[0, sl, :] = h_g * dec_col[sl, :] + dtx_col[sl, :] * b_g
    y = jnp.concatenate(y_parts, axis=1) + dsk_ref[...] * xs
    y = y * _silu(z_ref[0])
    y = y * lax.rsqrt(jnp.mean(y * y, axis=-1, keepdims=True) + LN_EPS) * ng_ref[...]
    y_ref[0] = y.astype(y_ref.dtype)


def _ssd_step(xbc, z, gates, buf, state, p):
    nb, cd = xbc.shape
    d_ssd = z.shape[1]
    n_heads = d_ssd // SSD_HEAD_DIM
    row3 = lambda b: (b, 0, 0)
    cst = lambda b: (0, 0)
    st2 = state.reshape(nb, d_ssd, SSD_STATE)
    sb = STEP_SEQS
    assert nb % sb == 0
    y, st_new = pl.pallas_call(
        functools.partial(_per_sequence(_ssd_step_one, 5, 2), d_ssd=d_ssd, n_heads=n_heads),
        grid=(nb // sb,),
        in_specs=[pl.BlockSpec((sb, 1, cd), row3),
                  pl.BlockSpec((sb, 1, d_ssd), row3),
                  pl.BlockSpec((sb, 1, 2 * LANES), row3),
                  pl.BlockSpec((sb, CONV_K - 1, cd), row3),
                  pl.BlockSpec((sb, d_ssd, SSD_STATE), row3),
                  pl.BlockSpec((CONV_K, cd), cst),
                  pl.BlockSpec((1, cd), cst),
                  pl.BlockSpec((1, LANES), cst),
                  pl.BlockSpec((1, LANES), cst),
                  pl.BlockSpec((1, d_ssd), cst),
                  pl.BlockSpec((1, d_ssd), cst),
                  pl.BlockSpec((LANES, d_ssd), cst)],
        out_specs=[pl.BlockSpec((sb, 1, d_ssd), row3),
                   pl.BlockSpec((sb, d_ssd, SSD_STATE), row3)],
        out_shape=[jax.ShapeDtypeStruct((nb, 1, d_ssd), BF16),
                   jax.ShapeDtypeStruct((nb, d_ssd, SSD_STATE), F32)],
        compiler_params=_cparams("arbitrary"),
        name="ssd_step",
    )(xbc.reshape(nb, 1, cd), z.reshape(nb, 1, d_ssd), gates.reshape(nb, 1, 2 * LANES), buf, st2,
      p["conv_ssd_w"], p["conv_ssd_b"], p["dt_bias_row"], p["a_log_row"], p["d_skip_row"], p["ssd_norm_g"],
      p["ssd_expand"])
    return y.reshape(nb, d_ssd), st_new.reshape(state.shape)


def _mlstm_step_one(u_ref, v_ref, o_ref, g_ref, buf_ref, c_ref, n_ref, m_ref, cw_ref, cb_ref, wq_ref, wk_ref,
                    bi_ref, bf_ref, ng_ref, hexp_ref, y_ref, c_out_ref, n_out_ref, m_out_ref, *, n_heads, dk):
    conv = cb_ref[...] + cw_ref[CONV_K - 1:CONV_K, :] * u_ref[0]
    for j in range(CONV_K - 1):
        conv = conv + cw_ref[j:j + 1, :] * buf_ref[0, j:j + 1, :]
    uc = jnp.broadcast_to(_silu(conv), (SUBLANES, conv.shape[1])).astype(BF16)

    gt = g_ref[0]
    logi = gt[:, 0:LANES] + bi_ref[...]
    logf = _log_sigmoid(gt[:, LANES:2 * LANES] + bf_ref[...])
    m_prev = m_ref[0]
    inter = logf + m_prev
    m_t = jnp.maximum(inter, logi)
    w_inter = jnp.exp(inter - m_t)
    w_in = jnp.exp(logi - m_t)
    floor = jnp.exp(-m_t)
    rows = jnp.concatenate([w_inter, w_in, floor, jnp.zeros((SUBLANES - 3, LANES), F32)], axis=0)
    full = _dot_sel_r(rows, hexp_ref[...])
    scale = dk ** -0.5

    for h in range(n_heads):
        sl = slice(h * dk, (h + 1) * dk)
        q = _dot(uc[:, sl], wq_ref[h])[0:1]
        k = _dot(uc[:, sl], wk_ref[h])[0:1] * scale
        v_h = v_ref[0][:, sl]
        wi = full[0:1, sl]
        ws = full[1:2, sl]
        fl = full[2:3, sl]
        c_h = c_ref[0, h]
        n_h = n_ref[0, h:h + 1, :]
        att = jnp.sum(q * k, axis=-1, keepdims=True) * ws
        q8 = jnp.broadcast_to(q, (SUBLANES, dk)).astype(BF16)
        num = att * v_h + wi * _dot(q8, c_h.astype(BF16))[0:1]
        den = att + wi * jnp.sum(q * n_h, axis=-1, keepdims=True)
        hout = num / jnp.maximum(jnp.abs(den), fl)
        mu = jnp.mean(hout, axis=-1, keepdims=True)
        var = jnp.mean(jnp.square(hout - mu), axis=-1, keepdims=True)
        hn = (hout - mu) * lax.rsqrt(var + LN_EPS) * ng_ref[:, sl]
        y_ref[0, :, sl] = (_sigmoid(o_ref[0][:, sl]) * hn).astype(y_ref.dtype)
        kw = k * ws
        kw_col = _col_bcast(kw, dk)
        wi_full = jnp.broadcast_to(wi, (dk, dk))
        for half in range(dk // LANES):
            hs = slice(half * LANES, (half + 1) * LANES)
            c_out_ref[0, h, :, hs] = wi_full[:, hs] * c_h[:, hs] + kw_col * v_h[:, hs]
        n_out_ref[0, h:h + 1, :] = wi * n_h + kw
    m_out_ref[0] = m_t


def _mlstm_step(u, v, o, gates, buf, c_st, n_st, m_row, p):
    nb, dm = u.shape
    n_heads, dk, _ = p["w_mq"].shape
    row3 = lambda b: (b, 0, 0)
    cst = lambda b: (0, 0)
    cst3 = lambda b: (0, 0, 0)
    r3 = lambda a: a.reshape(nb, 1, a.shape[-1])
    sb = STEP_SEQS
    assert nb % sb == 0
    y, c_new, n_new, m_new = pl.pallas_call(
        functools.partial(_per_sequence(_mlstm_step_one, 8, 4), n_heads=n_heads, dk=dk),
        grid=(nb // sb,),
        in_specs=[pl.BlockSpec((sb, 1, dm), row3),
                  pl.BlockSpec((sb, 1, dm), row3),
                  pl.BlockSpec((sb, 1, dm), row3),
                  pl.BlockSpec((sb, 1, 2 * LANES), row3),
                  pl.BlockSpec((sb, CONV_K - 1, dm), row3),
                  pl.BlockSpec((sb, n_heads, dk, dk), lambda b: (b, 0, 0, 0)),
                  pl.BlockSpec((sb, n_heads, dk), row3),
                  pl.BlockSpec((sb, 1, LANES), row3),
                  pl.BlockSpec((CONV_K, dm), cst),
                  pl.BlockSpec((1, dm), cst),
                  pl.BlockSpec((n_heads, dk, dk), cst3),
                  pl.BlockSpec((n_heads, dk, dk), cst3),
                  pl.BlockSpec((1, LANES), cst),
                  pl.BlockSpec((1, LANES), cst),
                  pl.BlockSpec((1, dm), cst),
                  pl.BlockSpec((LANES, dm), cst)],
        out_specs=[pl.BlockSpec((sb, 1, dm), row3),
                   pl.BlockSpec((sb, n_heads, dk, dk), lambda b: (b, 0, 0, 0)),
                   pl.BlockSpec((sb, n_heads, dk), row3),
                   pl.BlockSpec((sb, 1, LANES), row3)],
        out_shape=[jax.ShapeDtypeStruct((nb, 1, dm), BF16),
                   jax.ShapeDtypeStruct(c_st.shape, F32),
                   jax.ShapeDtypeStruct(n_st.shape, F32),
                   jax.ShapeDtypeStruct((nb, 1, LANES), F32)],
        compiler_params=_cparams("arbitrary"),
        name="mlstm_step",
    )(r3(u), r3(v), r3(o), r3(gates), buf, c_st, n_st, m_row, p["conv_m_w"], p["conv_m_b"], p["w_mq"],
      p["w_mk"], p["b_i_row"], p["b_f_row"], p["mlstm_norm_g"], p["mlstm_expand"])
    return y.reshape(nb, dm), c_new, n_new, m_new


def _outproj_kernel(ys_ref, hm_ref, x_ref, w_ref, g_ref, b_ref, x1t_ref, *, alpha):
    d_ssd = ys_ref.shape[1]
    tm = ys_ref.shape[0]
    subs = [slice(r0, min(r0 + OUTPROJ_ROWS, tm)) for r0 in range(0, tm, OUTPROJ_ROWS)]
    mix = [_dot(ys_ref[rs, :], w_ref[0:d_ssd, :]) + _dot(hm_ref[rs, :], w_ref[d_ssd:, :]) for rs in subs]
    for rs, mx in zip(subs, mix):
        r = alpha * x_ref[rs, :] + mx
        mu = jnp.mean(r, axis=-1, keepdims=True)
        var = jnp.mean(jnp.square(r - mu), axis=-1, keepdims=True)
        x1 = (r - mu) * lax.rsqrt(var + LN_EPS) * g_ref[...] + b_ref[...]
        x1t_ref[:, rs] = x1.T


def _outproj(ys, hm, x2, p, alpha, tm):
    m, d = x2.shape
    d_ssd, d_m = ys.shape[1], hm.shape[1]
    return pl.pallas_call(
        functools.partial(_outproj_kernel, alpha=alpha),
        grid=(m // tm,),
        in_specs=[pl.BlockSpec((tm, d_ssd), lambda i: (i, 0)),
                  pl.BlockSpec((tm, d_m), lambda i: (i, 0)),
                  pl.BlockSpec((tm, d), lambda i: (i, 0)),
                  pl.BlockSpec((d_ssd + d_m, d), lambda i: (0, 0)),
                  pl.BlockSpec((1, d), lambda i: (0, 0)),
                  pl.BlockSpec((1, d), lambda i: (0, 0))],
        out_specs=pl.BlockSpec((d, tm), lambda i: (0, i)),
        out_shape=jax.ShapeDtypeStruct((d, m), F32),
        compiler_params=_cparams("arbitrary"),
        name="outproj_ln1",
    )(ys, hm, x2, p["w_out"], p["ln1_g"], p["ln1_b"])


def _oddeven_merge_sort_pairs(n):
    pairs = []
    p = 1
    while p < n:
        k = p
        while k >= 1:
            for j in range(k % p, n - k, 2 * k):
                for i in range(min(k, n - j - k)):
                    if (i + j) // (2 * p) == (i + j + k) // (2 * p):
                        pairs.append((i + j, i + j + k))
            k //= 2
        p *= 2
    return pairs


_SORT_TOPK = _oddeven_merge_sort_pairs(PEER_TOPK)


def _compare_exchange(lst, i, j):
    hi, lo = jnp.maximum(lst[i], lst[j]), jnp.minimum(lst[i], lst[j])
    lst[i], lst[j] = hi, lo


def _bitonic_to_sorted(t):
    d = PEER_TOPK // 2
    while d >= 1:
        for i in range(PEER_TOPK):
            if i & d == 0:
                _compare_exchange(t, i, i + d)
        d //= 2
    return t


def _merge_top(x, y):
    neg = jnp.full_like(x[0], -jnp.inf)
    x = x + [neg] * (PEER_TOPK - len(x))
    y = y + [neg] * (PEER_TOPK - len(y))
    return _bitonic_to_sorted([jnp.maximum(x[k], y[PEER_TOPK - 1 - k]) for k in range(PEER_TOPK)])


def _top_rows(s):
    lst = [s[SUBLANES * g:SUBLANES * (g + 1), :] for g in range(PEER_TOPK)]
    for i, j in _SORT_TOPK:
        _compare_exchange(lst, i, j)
    shift = SUBLANES // 2
    while shift >= 1:
        lst = _merge_top(lst, [pltpu.roll(a, shift, 0) for a in lst])
        shift //= 2
    return lst


def _peer_topk_kernel(x1t_ref, wqt_ref, keys_ref, s1_ref, s2_ref, st_ref, b_scr, q_scr, a_scr, *, n_heads):
    tq = x1t_ref.shape[1]
    q_scr[...] = _dot(wqt_ref[...], x1t_ref[...].astype(BF16))
    half = q_scr.shape[0] // (2 * n_heads)

    for h in range(n_heads):
        for side, (s_ref, top_scr) in enumerate(((s1_ref, a_scr), (s2_ref, b_scr))):
            r0 = (2 * h + side) * half
            s = _dot(keys_ref[h, side], q_scr[r0:r0 + half, :].astype(BF16))
            for t in range(tq // LANES):
                ls = slice(t * LANES, (t + 1) * LANES)
                s_ref[h, t] = s[:, ls]
                top = _top_rows(s[:, ls])
                for k in range(PEER_TOPK):
                    top_scr[k, h:h + 1, ls] = top[k][0:1, :]

    for t in range(tq // LANES):
        ls = slice(t * LANES, (t + 1) * LANES)
        a = [a_scr[k, :, ls] for k in range(PEER_TOPK)]
        b = [b_scr[k, :, ls] for k in range(PEER_TOPK)]
        single = PEER_TOPK // 2
        lists = [[a[i] + b[j] for j in range(PEER_TOPK // (i + 1))] for i in range(single)]
        lists.append([a[i] + b[0] for i in range(single, PEER_TOPK)])
        merged = lists[0]
        for other in lists[1:]:
            merged = _merge_top(merged, other)
        tau = merged[PEER_TOPK - 1]
        top = a[0] + b[0]
        zsum = jnp.zeros_like(tau)
        for cand in (c for lst in lists for c in lst):
            zsum = zsum + jnp.where(cand >= tau, jnp.exp(cand - top), 0.0)
        st_ref[0, :, ls] = tau
        st_ref[1, :, ls] = a[0]
        st_ref[2, :, ls] = b[0]
        st_ref[3, :, ls] = 1.0 / zsum


def _peer_topk(x1t, p, tq):
    d, m = x1t.shape
    n_heads, _, n_keys, half = p["peer_keys"].shape
    nq = p["peer_wqT"].shape[0]
    assert n_heads == SUBLANES and n_keys == PEER_TOPK * SUBLANES
    keyb = lambda i: (0, i, 0, 0)
    return pl.pallas_call(
        functools.partial(_peer_topk_kernel, n_heads=n_heads),
        grid=(m // tq,),
        in_specs=[pl.BlockSpec((d, tq), lambda i: (0, i)),
                  pl.BlockSpec((nq, d), lambda i: (0, 0)),
                  pl.BlockSpec((n_heads, 2, n_keys, half), lambda i: (0, 0, 0, 0))],
        out_specs=[pl.BlockSpec((n_heads, tq // LANES, n_keys, LANES), keyb),
                   pl.BlockSpec((n_heads, tq // LANES, n_keys, LANES), keyb),
                   pl.BlockSpec((N_STATS, SUBLANES, tq), lambda i: (0, 0, i)),
                   pl.BlockSpec((PEER_TOPK, SUBLANES, tq), lambda i: (0, 0, i))],
        out_shape=[jax.ShapeDtypeStruct((n_heads, m // LANES, n_keys, LANES), F32),
                   jax.ShapeDtypeStruct((n_heads, m // LANES, n_keys, LANES), F32),
                   jax.ShapeDtypeStruct((N_STATS, SUBLANES, m), F32),
                   jax.ShapeDtypeStruct((PEER_TOPK, SUBLANES, m), F32)],
        scratch_shapes=[pltpu.VMEM((nq, tq), F32),
                        pltpu.VMEM((PEER_TOPK, SUBLANES, tq), F32)],
        compiler_params=_cparams("arbitrary"),
        name="peer_topk",
    )(x1t, p["peer_wqT"], p["peer_keys"])


def _peer_dense_kernel(x1t_ref, s1_ref, s2_ref, st_ref, bl_ref, u_ref, vt_ref, g_ref, b_ref, out_ref,
                       xb_scr, e2_scr, c_scr, thr_scr, acc_scr, s_scr, w_scr, thr_r, cr, *, alpha, n_heads):
    j = pl.program_id(1)
    ec = u_ref.shape[0]
    tb = x1t_ref.shape[1]
    nt = tb // LANES
    n_keys = s1_ref.shape[2]
    na = ec // n_keys

    @pl.when(j == 0)
    def _():
        xb_scr[...] = x1t_ref[...].astype(BF16)
        for h in range(n_heads):
            for t in range(nt):
                ls = slice(t * LANES, (t + 1) * LANES)
                e2_scr[h, t] = jnp.exp(s2_ref[h, t] - st_ref[2, h:h + 1, ls])
                s1 = s1_ref[h, t]
                c_scr[h, t] = jnp.exp(s1 - st_ref[1, h:h + 1, ls]) * (0.5 * st_ref[3, h:h + 1, ls])
                tau = st_ref[0, h:h + 1, ls]
                thr = jnp.full((n_keys, LANES), jnp.inf, F32)
                for k in range(PEER_TOPK):
                    bk = bl_ref[k, h:h + 1, ls]
                    thr = jnp.where(s1 + bk >= tau, bk, thr)
                thr_scr[h, t] = thr
        acc_scr[...] = jnp.zeros_like(acc_scr)

    s_scr[:, 0:tb] = _dot(u_ref[...], xb_scr[...])

    g0 = pl.multiple_of(j * na, na)
    for h in range(n_heads):
        for t in range(nt):
            ls = slice(t * LANES, (t + 1) * LANES)
            thr_blk = thr_scr[h, t, pl.ds(g0, na), :]
            c_blk = c_scr[h, t, pl.ds(g0, na), :]
            for al in range(na):
                thr_r[h * na + al, :, ls] = thr_blk[al:al + 1, :]
                cr[h * na + al, :, ls] = c_blk[al:al + 1, :]

    def first_key(al, carry):
        r0 = pl.multiple_of(al * n_keys, n_keys)
        for t in range(nt):
            ls = slice(t * LANES, (t + 1) * LANES)
            gate = jnp.zeros((n_keys, LANES), F32)
            for h in range(n_heads):
                gate = gate + jnp.where(s2_ref[h, t] >= thr_r[h * na + al][:, ls],
                                        e2_scr[h, t] * cr[h * na + al][:, ls], 0.0)
            x = s_scr[pl.ds(r0, n_keys), ls]
            hid = x + x * lax.erf(x * (1.0 / math.sqrt(2.0)))
            w_scr[pl.ds(r0, n_keys), ls] = (gate * hid).astype(BF16)
        return carry

    lax.fori_loop(0, na, first_key, 0)
    acc_scr[...] += _dot(vt_ref[...], w_scr[:, 0:tb])

    @pl.when(j == pl.num_programs(1) - 1)
    def _():
        r = alpha * x1t_ref[...] + acc_scr[...]
        mu = jnp.mean(r, axis=0, keepdims=True)
        var = jnp.mean(jnp.square(r - mu), axis=0, keepdims=True)
        y = (r - mu) * lax.rsqrt(var + LN_EPS) * g_ref[...] + b_ref[...]
        out_ref[...] = y.T


def _peer_dense(x1t, s1, s2, st, blist, p, alpha, tb):
    d, m = x1t.shape
    n_heads, _, n_keys, _ = s1.shape
    n_exp = p["peer_u"].shape[0]
    ec = PEER_CHUNK
    nj = n_exp // ec
    na = ec // n_keys
    assert na == SUBLANES
    nt = tb // LANES
    keyb = lambda i, j: (0, i, 0, 0)
    tokb = lambda i, j: (0, 0, i)
    pitch = tb + LANES
    return pl.pallas_call(
        functools.partial(_peer_dense_kernel, alpha=alpha, n_heads=n_heads),
        grid=(m // tb, nj),
        in_specs=[pl.BlockSpec((d, tb), lambda i, j: (0, i)),
                  pl.BlockSpec((n_heads, nt, n_keys, LANES), keyb),
                  pl.BlockSpec((n_heads, nt, n_keys, LANES), keyb),
                  pl.BlockSpec((N_STATS, SUBLANES, tb), tokb),
                  pl.BlockSpec((PEER_TOPK, SUBLANES, tb), tokb),
                  pl.BlockSpec((ec, d), lambda i, j: (j, 0)),
                  pl.BlockSpec((d, ec), lambda i, j: (0, j)),
                  pl.BlockSpec((d, 1), lambda i, j: (0, 0)),
                  pl.BlockSpec((d, 1), lambda i, j: (0, 0))],
        out_specs=pl.BlockSpec((tb, d), lambda i, j: (i, 0)),
        out_shape=jax.ShapeDtypeStruct((m, d), F32),
        scratch_shapes=[pltpu.VMEM((d, tb), BF16),
                        pltpu.VMEM((n_heads, nt, n_keys, LANES), F32),
                        pltpu.VMEM((n_heads, nt, n_keys, LANES), F32),
                        pltpu.VMEM((n_heads, nt, n_keys, LANES), F32),
                        pltpu.VMEM((d, tb), F32),
                        pltpu.VMEM((ec, pitch), F32),
                        pltpu.VMEM((ec, pitch), BF16),
                        pltpu.VMEM((n_heads * na, 1, tb), F32),
                        pltpu.VMEM((n_heads * na, 1, tb), F32)],
        compiler_params=_cparams("arbitrary", "arbitrary"),
        name="peer_dense",
    )(x1t, s1, s2, st, blist, p["peer_u"], p["peer_vT"], p["ln2_g_col"], p["ln2_b_col"])


def _pad_row(vec, offset):
    return jnp.zeros((1, LANES), F32).at[0, offset:offset + vec.shape[0]].set(vec.astype(F32))


def _prep_layer(l, w_in, conv_ssd_w, conv_ssd_b, dt_bias, a_log, d_skip, ssd_norm_g, conv_m_w, conv_m_b,
                w_mq, w_mk, b_igate, b_fgate, mlstm_norm_g, w_out, ln1_g, ln1_b, peer_wq, peer_keys,
                peer_u, peer_v, ln2_g, ln2_b):
    d_model = w_in.shape[1]
    d_ssd = ssd_norm_g.shape[1]
    cd = conv_ssd_w.shape[2]
    n_sh = dt_bias.shape[1]
    d_m = conv_m_w.shape[2]
    n_mh = w_mq.shape[1]
    dk = w_mq.shape[2]
    assert n_sh <= GATE_I0 and GATE_I0 + n_mh <= LANES and d_ssd == n_sh * SSD_HEAD_DIM
    w = w_in[l]
    offs = [0]
    for n in (d_ssd, cd, n_sh, d_m, d_m, d_m, n_mh, n_mh):
        offs.append(offs[-1] + n)
    z_w, xbc_w, dt_w, u_w, v_w, o_w, i_w, f_w = (w[:, offs[k]:offs[k + 1]] for k in range(8))
    gate_a = jnp.zeros((d_model, LANES), F32).at[:, 0:n_sh].set(dt_w).at[:, GATE_I0:GATE_I0 + n_mh].set(i_w)
    gate_b = jnp.zeros((d_model, LANES), F32).at[:, GATE_I0:GATE_I0 + n_mh].set(f_w)
    w_cat = jnp.concatenate([z_w, xbc_w, u_w, v_w, o_w, gate_a, gate_b], axis=1).astype(BF16)
    widths = (d_ssd, cd, d_m, d_m, d_m, 2 * LANES)

    lane = jnp.arange(LANES)[:, None]
    ssd_expand = (lane == (jnp.arange(d_ssd)[None, :] // SSD_HEAD_DIM)).astype(BF16)
    mlstm_expand = (lane == (GATE_I0 + jnp.arange(d_m)[None, :] // dk)).astype(BF16)
    t = jnp.arange(CHUNK)
    tril = (t[:, None] >= t[None, :]).astype(BF16)
    p = {
        "w_cat": w_cat,
        "conv_ssd_w": conv_ssd_w[l], "conv_ssd_b": conv_ssd_b[l][None, :],
        "dt_bias_row": _pad_row(dt_bias[l], 0), "a_log_row": _pad_row(a_log[l], 0),
        "d_skip_row": jnp.repeat(d_skip[l], SSD_HEAD_DIM)[None, :], "ssd_norm_g": ssd_norm_g[l][None, :],
        "conv_m_w": conv_m_w[l], "conv_m_b": conv_m_b[l][None, :],
        "w_mq": w_mq[l].astype(BF16), "w_mk": w_mk[l].astype(BF16),
        "b_i_row": _pad_row(b_igate[l], GATE_I0), "b_f_row": _pad_row(b_fgate[l], GATE_I0),
        "mlstm_norm_g": mlstm_norm_g[l].reshape(1, d_m),
        "w_out": w_out[l].astype(BF16), "ln1_g": ln1_g[l][None, :], "ln1_b": ln1_b[l][None, :],
        "peer_wqT": peer_wq[l].T.astype(BF16), "peer_keys": peer_keys[l].astype(BF16),
        "peer_u": peer_u[l].astype(BF16),
        "peer_vT": peer_v[l].astype(BF16).T,
        "ln2_g_col": ln2_g[l][:, None], "ln2_b_col": ln2_b[l][:, None],
        "tril": tril, "ssd_expand": ssd_expand, "mlstm_expand": mlstm_expand,
    }
    return p, widths


def _tile_sizes(m):
    tm = 256 if m % 256 == 0 else LANES
    tb = 512 if m % 512 == 0 else LANES
    return tm, tb


def _peer_block(x1t, p, alpha):
    m = x1t.shape[1]
    tm, tb = _tile_sizes(m)
    s1, s2, st, blist = _peer_topk(x1t, p, tm)
    return _peer_dense(x1t, s1, s2, st, blist, p, alpha, tb)


def _layer_prompt(x, p, widths, alpha):
    bsz, seq, d = x.shape
    assert seq % CHUNK == 0 and seq >= CONV_K - 1
    x2 = x.reshape(bsz * seq, d)
    tm, _ = _tile_sizes(bsz * seq)
    z, xbc, u, v, o, gates = _inproj(x2, p["w_cat"], widths, tm)
    y_s, h_t = _ssd_chunk(xbc, z, gates, p, bsz, seq)
    h_m, c_new, n_new, m_new = _mlstm_chunk(u, v, o, gates, p, bsz, seq)
    x1t = _outproj(y_s, h_m, x2, p, alpha, 2 * tm if (bsz * seq) % (2 * tm) == 0 else tm)
    y = _peer_block(x1t, p, alpha).reshape(bsz, seq, d)
    n_mh = c_new.shape[1]
    d_ssd = z.shape[1]
    n_sh = d_ssd // SSD_HEAD_DIM
    hg = n_sh // SSD_GROUPS
    new_ssd = h_t.reshape(bsz, SSD_GROUPS, SSD_STATE, hg, SSD_HEAD_DIM).transpose(0, 1, 3, 4, 2)
    new_ssd = new_ssd.reshape(bsz, n_sh, SSD_HEAD_DIM, SSD_STATE)
    states = (new_ssd,
              xbc.reshape(bsz, seq, -1)[:, seq - (CONV_K - 1):, :],
              c_new,
              n_new[:, :, 0, :],
              m_new[:, 0, GATE_I0:GATE_I0 + n_mh],
              u.reshape(bsz, seq, -1)[:, seq - (CONV_K - 1):, :])
    return y, states


def _layer_sample(x, st, p, widths, alpha):
    st_ssd, st_ssd_conv, st_c, st_n, st_m, st_mconv = st
    nb, seq, d = x.shape
    assert seq == 1 and nb % LANES == 0
    x2 = x.reshape(nb, d)
    z, xbc, u, v, o, gates = _inproj(x2, p["w_cat"], widths, LANES)
    y_s, new_ssd = _ssd_step(xbc, z, gates, st_ssd_conv, st_ssd, p)
    n_mh = st_c.shape[1]
    m_row = jnp.zeros((nb, 1, LANES), F32).at[:, 0, GATE_I0:GATE_I0 + n_mh].set(st_m)
    h_m, c_new, n_new, m_new = _mlstm_step(u, v, o, gates, st_mconv, st_c, st_n, m_row, p)
    x1t = _outproj(y_s, h_m, x2, p, alpha, LANES)
    y = _peer_block(x1t, p, alpha).reshape(nb, seq, d)
    states = (new_ssd,
              jnp.concatenate([st_ssd_conv[:, 1:, :], xbc[:, None, :]], axis=1),
              c_new,
              n_new,
              m_new[:, 0, GATE_I0:GATE_I0 + n_mh],
              jnp.concatenate([st_mconv[:, 1:, :], u[:, None, :]], axis=1))
    return y, states


def kernel(x_prompt, x_sample, state_ssd, state_ssd_conv, state_mlstm_c, state_mlstm_n, state_mlstm_m,
           state_mlstm_conv, w_in, conv_ssd_w, conv_ssd_b, dt_bias, a_log, d_skip, ssd_norm_g, conv_m_w,
           conv_m_b, w_mq, w_mk, b_igate, b_fgate, mlstm_norm_g, w_out, ln1_g, ln1_b, peer_wq, peer_keys,
           peer_u, peer_v, ln2_g, ln2_b):
    depth = w_in.shape[0]
    alpha = (2.0 * depth) ** 0.25
    states = (state_ssd, state_ssd_conv, state_mlstm_c, state_mlstm_n, state_mlstm_m, state_mlstm_conv)
    yp, ys = x_prompt, x_sample
    p_list, s_list = [], []
    for l in range(depth):
        p, widths = _prep_layer(l, w_in, conv_ssd_w, conv_ssd_b, dt_bias, a_log, d_skip, ssd_norm_g, conv_m_w,
                                conv_m_b, w_mq, w_mk, b_igate, b_fgate, mlstm_norm_g, w_out, ln1_g, ln1_b,
                                peer_wq, peer_keys, peer_u, peer_v, ln2_g, ln2_b)
        yp, p_new = _layer_prompt(yp, p, widths, alpha)
        ys, s_new = _layer_sample(ys, tuple(s[l] for s in states), p, widths, alpha)
        p_list.append(p_new)
        s_list.append(s_new)
    p_out = tuple(jnp.stack([pn[i] for pn in p_list]) for i in range(6))
    s_out = tuple(jnp.stack([sn[i] for sn in s_list]) for i in range(6))
    return (yp, ys) + p_out + s_out
```

```python
import functools
import math

import jax
import jax.numpy as jnp
from jax import lax
from jax.experimental import pallas as pl
from jax.experimental.pallas import tpu as pltpu

F32 = jnp.float32
BF16 = jnp.bfloat16

LANES = 128
SUBLANES = 8
CHUNK = 128
CONV_K = 4
LN_EPS = 1e-5
VMEM_LIMIT = 58 * 1024 * 1024

SSD_HEAD_DIM = 64
SSD_STATE = 128
SSD_GROUPS = 2
PEER_TOPK = 16
GATE_I0 = 16
N_STATS = 4
PEER_CHUNK = 2048
STEP_SEQS = 4
OUTPROJ_ROWS = 256


def _cparams(*sem):
    return pltpu.CompilerParams(dimension_semantics=sem, vmem_limit_bytes=VMEM_LIMIT)


def _dot(a, b):
    return jnp.dot(a, b, preferred_element_type=F32)


def _dot_nt(a, b):
    return lax.dot_general(a, b, (((1,), (1,)), ((), ())), preferred_element_type=F32)


def _split3(x):
    h = x.astype(BF16)
    r = x - h.astype(F32)
    m = r.astype(BF16)
    lo = (r - m.astype(F32)).astype(BF16)
    return h, m, lo


def _dot_sel_l(sel_bf16, x):
    h, m, lo = _split3(x)
    return _dot(sel_bf16, h) + _dot(sel_bf16, m) + _dot(sel_bf16, lo)


def _dot_sel_r(x, sel_bf16):
    h, m, lo = _split3(x)
    return _dot(h, sel_bf16) + _dot(m, sel_bf16) + _dot(lo, sel_bf16)


def _sigmoid(x):
    return 1.0 / (1.0 + jnp.exp(-x))


def _silu(x):
    return x * _sigmoid(x)


def _softplus(x):
    return jnp.maximum(x, 0.0) + jnp.log1p(jnp.exp(-jnp.abs(x)))


def _log_sigmoid(x):
    return -_softplus(-x)


def _gelu_exact(x):
    return 0.5 * x * (1.0 + lax.erf(x * (1.0 / math.sqrt(2.0))))


def _inproj_kernel(x_ref, w_ref, *out_refs):
    xb = x_ref[...].astype(BF16)
    off = 0
    for ref in out_refs:
        n = ref.shape[-1]
        ref[...] = _dot(xb, w_ref[:, off:off + n])
        off += n


def _inproj(x2, w_cat, widths, tm):
    m, d = x2.shape
    n_all = w_cat.shape[1]
    return pl.pallas_call(
        _inproj_kernel,
        grid=(m // tm,),
        in_specs=[pl.BlockSpec((tm, d), lambda i: (i, 0)),
                  pl.BlockSpec((d, n_all), lambda i: (0, 0))],
        out_specs=[pl.BlockSpec((tm, n), lambda i: (i, 0)) for n in widths],
        out_shape=[jax.ShapeDtypeStruct((m, n), F32) for n in widths],
        compiler_params=_cparams("arbitrary"),
        name="inproj",
    )(x2, w_cat)


def _ssd_chunk_kernel(xbc_ref, z_ref, g_ref, cw_ref, cb_ref, dtb_ref, alog_ref, dsk_ref, ng_ref,
                      tril_ref, exp_ref, y_ref, hT_out_ref, xpad, hT, ybuf, *, nc, d_ssd, n_heads):
    c = pl.program_id(1)
    cd = xpad.shape[1]
    hg = n_heads // SSD_GROUPS
    gw = d_ssd // SSD_GROUPS

    @pl.when(c == 0)
    def _():
        xpad[0:SUBLANES, :] = jnp.zeros((SUBLANES, cd), F32)
        hT[...] = jnp.zeros_like(hT)

    xpad[SUBLANES:SUBLANES + CHUNK, :] = xbc_ref[...]
    conv = cb_ref[...]
    for j in range(CONV_K):
        r0 = SUBLANES - (CONV_K - 1) + j
        conv = conv + cw_ref[j:j + 1, :] * xpad[r0:r0 + CHUNK, :]
    xpad[0:SUBLANES, :] = xpad[CHUNK:CHUNK + SUBLANES, :]
    xc = _silu(conv)
    xs = xc[:, 0:d_ssd]

    lane = lax.broadcasted_iota(jnp.int32, (1, LANES), 1)
    a_row = jnp.where(lane < n_heads, -jnp.exp(alog_ref[...]), 0.0)
    dt = _softplus(g_ref[:, 0:LANES] + dtb_ref[...])
    acs = _dot_sel_l(tril_ref[...], dt * a_row)
    dt_t = dt.T
    acs_t = acs.T
    eacs = jnp.exp(acs)
    dec = jnp.exp(acs[CHUNK - 1:CHUNK, :] - acs) * dt
    both = _dot_sel_r(jnp.concatenate([eacs, dec], axis=0), exp_ref[...])
    eacs_full = both[0:CHUNK]
    dec_full = both[CHUNK:2 * CHUNK]

    row = lax.broadcasted_iota(jnp.int32, (CHUNK, CHUNK), 0)
    col = lax.broadcasted_iota(jnp.int32, (CHUNK, CHUNK), 1)
    causal = row >= col

    for g in range(SSD_GROUPS):
        b_g = xc[:, d_ssd + g * SSD_STATE:d_ssd + (g + 1) * SSD_STATE]
        c_g = xc[:, d_ssd + (SSD_GROUPS + g) * SSD_STATE:d_ssd + (SSD_GROUPS + g + 1) * SSD_STATE]
        b_gb = b_g.astype(BF16)
        c_gb = c_g.astype(BF16)
        cb = _dot_nt(c_gb, b_gb)
        h_g = hT[g]
        y_inter = _dot(c_gb, h_g.astype(BF16)) * eacs_full[:, g * gw:(g + 1) * gw]
        for hl in range(hg):
            h = g * hg + hl
            diff = acs[:, h:h + 1] - acs_t[h:h + 1, :]
            lmat = jnp.exp(jnp.where(causal, diff, -jnp.inf))
            mm = cb * lmat * dt_t[h:h + 1, :]
            x_h = xs[:, h * SSD_HEAD_DIM:(h + 1) * SSD_HEAD_DIM]
            ybuf[:, h * SSD_HEAD_DIM:(h + 1) * SSD_HEAD_DIM] = _dot(mm.astype(BF16), x_h.astype(BF16))
        xd = xs[:, g * gw:(g + 1) * gw] * dec_full[:, g * gw:(g + 1) * gw]
        hT[g] = h_g * eacs_full[CHUNK - 1:CHUNK, g * gw:(g + 1) * gw] + _dot(b_g.T.astype(BF16), xd.astype(BF16))
        ybuf[:, g * gw:(g + 1) * gw] = ybuf[:, g * gw:(g + 1) * gw] + y_inter

    y = ybuf[...] + dsk_ref[...] * xs
    y = y * _silu(z_ref[...])
    y = y * lax.rsqrt(jnp.mean(y * y, axis=-1, keepdims=True) + LN_EPS) * ng_ref[...]
    y_ref[...] = y.astype(y_ref.dtype)

    @pl.when(c == nc - 1)
    def _():
        hT_out_ref[0] = hT[...]


def _ssd_chunk(xbc, z, gates, p, bsz, seq):
    nc = seq // CHUNK
    cd = xbc.shape[1]
    d_ssd = z.shape[1]
    n_heads = d_ssd // SSD_HEAD_DIM
    gw = d_ssd // SSD_GROUPS
    tok = lambda b, c: (b * nc + c, 0)
    cst = lambda b, c: (0, 0)
    return pl.pallas_call(
        functools.partial(_ssd_chunk_kernel, nc=nc, d_ssd=d_ssd, n_heads=n_heads),
        grid=(bsz, nc),
        in_specs=[pl.BlockSpec((CHUNK, cd), tok),
                  pl.BlockSpec((CHUNK, d_ssd), tok),
                  pl.BlockSpec((CHUNK, 2 * LANES), tok),
                  pl.BlockSpec((CONV_K, cd), cst),
                  pl.BlockSpec((1, cd), cst),
                  pl.BlockSpec((1, LANES), cst),
                  pl.BlockSpec((1, LANES), cst),
                  pl.BlockSpec((1, d_ssd), cst),
                  pl.BlockSpec((1, d_ssd), cst),
                  pl.BlockSpec((CHUNK, CHUNK), cst),
                  pl.BlockSpec((LANES, d_ssd), cst)],
        out_specs=[pl.BlockSpec((CHUNK, d_ssd), tok),
                   pl.BlockSpec((1, SSD_GROUPS, SSD_STATE, gw), lambda b, c: (b, 0, 0, 0))],
        out_shape=[jax.ShapeDtypeStruct((bsz * seq, d_ssd), BF16),
                   jax.ShapeDtypeStruct((bsz, SSD_GROUPS, SSD_STATE, gw), F32)],
        scratch_shapes=[pltpu.VMEM((CHUNK + SUBLANES, cd), F32),
                        pltpu.VMEM((SSD_GROUPS, SSD_STATE, gw), F32),
                        pltpu.VMEM((CHUNK, d_ssd), F32)],
        compiler_params=_cparams("arbitrary", "arbitrary"),
        name="ssd_chunk",
    )(xbc, z, gates, p["conv_ssd_w"], p["conv_ssd_b"], p["dt_bias_row"], p["a_log_row"],
      p["d_skip_row"], p["ssd_norm_g"], p["tril"], p["ssd_expand"])


def _mlstm_chunk_kernel(u_ref, v_ref, o_ref, g_ref, cw_ref, cb_ref, wq_ref, wk_ref, bi_ref, bf_ref, ng_ref,
                        tril_ref, y_ref, c_out_ref, n_out_ref, m_out_ref, upad, c_st, n_st, m_st,
                        *, nc, n_heads, dk):
    c = pl.program_id(1)
    dm = upad.shape[1]

    @pl.when(c == 0)
    def _():
        upad[0:SUBLANES, :] = jnp.zeros((SUBLANES, dm), F32)
        c_st[...] = jnp.zeros_like(c_st)
        n_st[...] = jnp.zeros_like(n_st)
        m_st[...] = jnp.zeros_like(m_st)

    upad[SUBLANES:SUBLANES + CHUNK, :] = u_ref[...]
    conv = cb_ref[...]
    for j in range(CONV_K):
        r0 = SUBLANES - (CONV_K - 1) + j
        conv = conv + cw_ref[j:j + 1, :] * upad[r0:r0 + CHUNK, :]
    upad[0:SUBLANES, :] = upad[CHUNK:CHUNK + SUBLANES, :]
    uc = _silu(conv).astype(BF16)

    logi = g_ref[:, 0:LANES] + bi_ref[...]
    logf = _log_sigmoid(g_ref[:, LANES:2 * LANES] + bf_ref[...])
    bcs = _dot_sel_l(tril_ref[...], logf)
    m_prev = m_st[0:1, :]
    b_end = bcs[CHUNK - 1:CHUNK, :]
    g_all = b_end - bcs + logi
    m_new = jnp.maximum(b_end + m_prev, jnp.max(g_all, axis=0, keepdims=True))
    w_old = jnp.exp(b_end + m_prev - m_new)
    w_s_all = jnp.exp(g_all - m_new)
    inter_all = bcs + m_prev
    bcs_t = bcs.T
    logi_t = logi.T

    row = lax.broadcasted_iota(jnp.int32, (CHUNK, CHUNK), 0)
    col = lax.broadcasted_iota(jnp.int32, (CHUNK, CHUNK), 1)
    causal = row >= col
    scale = dk ** -0.5

    hs = range(n_heads)
    sls = [slice(h * dk, (h + 1) * dk) for h in hs]
    gls = [GATE_I0 + h for h in hs]
    q = [_dot(uc[:, sls[h]], wq_ref[h]) for h in hs]
    k = [_dot(uc[:, sls[h]], wk_ref[h]) * scale for h in hs]
    qb = [x.astype(BF16) for x in q]
    kb = [x.astype(BF16) for x in k]
    vb = [v_ref[:, sls[h]].astype(BF16) for h in hs]
    dmat = [jnp.where(causal, bcs[:, g:g + 1] - bcs_t[g:g + 1, :] + logi_t[g:g + 1, :], -jnp.inf) for g in gls]
    inter = [inter_all[:, g:g + 1] for g in gls]
    m_t = [jnp.maximum(inter[h], jnp.max(dmat[h], axis=1, keepdims=True)) for h in hs]
    w_inter = [jnp.exp(inter[h] - m_t[h]) for h in hs]
    qk = [_dot_nt(qb[h], kb[h]) for h in hs]
    c_old = [c_st[h] for h in hs]
    n_old = [n_st[h][0:1, :] for h in hs]
    qc = [_dot(qb[h], c_old[h].astype(BF16)) for h in hs]
    att = [qk[h] * jnp.exp(dmat[h] - m_t[h]) for h in hs]
    av = [_dot(att[h].astype(BF16), vb[h]) for h in hs]
    kw = [k[h] * w_s_all[:, gls[h]:gls[h] + 1] for h in hs]
    kv = [_dot(kw[h].T.astype(BF16), vb[h]) for h in hs]
    num = [av[h] + w_inter[h] * qc[h] for h in hs]
    den = [jnp.sum(att[h], axis=1, keepdims=True) + w_inter[h] * jnp.sum(q[h] * n_old[h], axis=1, keepdims=True)
           for h in hs]
    hout = [num[h] / jnp.maximum(jnp.abs(den[h]), jnp.exp(-m_t[h])) for h in hs]
    mu = [jnp.mean(x, axis=-1, keepdims=True) for x in hout]
    cen = [hout[h] - mu[h] for h in hs]
    var = [jnp.mean(jnp.square(x), axis=-1, keepdims=True) for x in cen]
    for h in hs:
        sl = sls[h]
        hn = cen[h] * lax.rsqrt(var[h] + LN_EPS) * ng_ref[:, sl]
        y_ref[:, sl] = (_sigmoid(o_ref[:, sl]) * hn).astype(y_ref.dtype)
        wo = w_old[:, gls[h]:gls[h] + 1]
        c_st[h] = wo * c_old[h] + kv[h]
        n_st[h] = jnp.broadcast_to(wo * n_old[h] + jnp.sum(kw[h], axis=0, keepdims=True), (SUBLANES, dk))

    m_st[...] = jnp.broadcast_to(m_new, (SUBLANES, LANES))

    @pl.when(c == nc - 1)
    def _():
        c_out_ref[0] = c_st[...]
        n_out_ref[0] = n_st[...]
        m_out_ref[0] = m_st[...]


def _mlstm_chunk(u, v, o, gates, p, bsz, seq):
    nc = seq // CHUNK
    dm = u.shape[1]
    n_heads, dk, _ = p["w_mq"].shape
    tok = lambda b, c: (b * nc + c, 0)
    cst = lambda b, c: (0, 0)
    cst3 = lambda b, c: (0, 0, 0)
    return pl.pallas_call(
        functools.partial(_mlstm_chunk_kernel, nc=nc, n_heads=n_heads, dk=dk),
        grid=(bsz, nc),
        in_specs=[pl.BlockSpec((CHUNK, dm), tok),
                  pl.BlockSpec((CHUNK, dm), tok),
                  pl.BlockSpec((CHUNK, dm), tok),
                  pl.BlockSpec((CHUNK, 2 * LANES), tok),
                  pl.BlockSpec((CONV_K, dm), cst),
                  pl.BlockSpec((1, dm), cst),
                  pl.BlockSpec((n_heads, dk, dk), cst3),
                  pl.BlockSpec((n_heads, dk, dk), cst3),
                  pl.BlockSpec((1, LANES), cst),
                  pl.BlockSpec((1, LANES), cst),
                  pl.BlockSpec((1, dm), cst),
                  pl.BlockSpec((CHUNK, CHUNK), cst)],
        out_specs=[pl.BlockSpec((CHUNK, dm), tok),
                   pl.BlockSpec((1, n_heads, dk, dk), lambda b, c: (b, 0, 0, 0)),
                   pl.BlockSpec((1, n_heads, SUBLANES, dk), lambda b, c: (b, 0, 0, 0)),
                   pl.BlockSpec((1, SUBLANES, LANES), lambda b, c: (b, 0, 0))],
        out_shape=[jax.ShapeDtypeStruct((bsz * seq, dm), BF16),
                   jax.ShapeDtypeStruct((bsz, n_heads, dk, dk), F32),
                   jax.ShapeDtypeStruct((bsz, n_heads, SUBLANES, dk), F32),
                   jax.ShapeDtypeStruct((bsz, SUBLANES, LANES), F32)],
        scratch_shapes=[pltpu.VMEM((CHUNK + SUBLANES, dm), F32),
                        pltpu.VMEM((n_heads, dk, dk), F32),
                        pltpu.VMEM((n_heads, SUBLANES, dk), F32),
                        pltpu.VMEM((SUBLANES, LANES), F32)],
        compiler_params=_cparams("arbitrary", "arbitrary"),
        name="mlstm_chunk",
    )(u, v, o, gates, p["conv_m_w"], p["conv_m_b"], p["w_mq"], p["w_mk"], p["b_i_row"], p["b_f_row"],
      p["mlstm_norm_g"], p["tril"])


def _col_bcast(row, n_rows):
    return jnp.broadcast_to(row, (LANES, row.shape[1])).T[0:n_rows, :]


def _per_sequence(one_sequence, n_seq_in, n_seq_out):
    def kernel(*refs, **static):
        seq_in, shared, seq_out = refs[:n_seq_in], refs[n_seq_in:len(refs) - n_seq_out], refs[len(refs) - n_seq_out:]
        for s in range(seq_in[0].shape[0]):
            one = slice(s, s + 1)
            one_sequence(*(r.at[one] for r in seq_in), *shared, *(r.at[one] for r in seq_out), **static)
    return kernel


def _ssd_step_one(xbc_ref, z_ref, g_ref, buf_ref, st_ref, cw_ref, cb_ref, dtb_ref, alog_ref, dsk_ref, ng_ref,
                  exp_ref, y_ref, st_out_ref, *, d_ssd, n_heads):
    gw = d_ssd // SSD_GROUPS
    x_row = xbc_ref[0]
    conv = cb_ref[...] + cw_ref[CONV_K - 1:CONV_K, :] * x_row
    for j in range(CONV_K - 1):
        conv = conv + cw_ref[j:j + 1, :] * buf_ref[0, j:j + 1, :]
    xc = _silu(conv)
    xs = xc[:, 0:d_ssd]

    lane = lax.broadcasted_iota(jnp.int32, (1, LANES), 1)
    a_row = jnp.where(lane < n_heads, -jnp.exp(alog_ref[...]), 0.0)
    dt = _softplus(g_ref[0][:, 0:LANES] + dtb_ref[...])
    dec = jnp.exp(dt * a_row)
    both = _dot_sel_r(jnp.concatenate([jnp.broadcast_to(dt, (SUBLANES, LANES)),
                                       jnp.broadcast_to(dec, (SUBLANES, LANES))], axis=0), exp_ref[...])
    dt_full = both[0:1]
    dec_full = both[SUBLANES:SUBLANES + 1]
    dtx = dt_full * xs
    dtx_col = _col_bcast(dtx, d_ssd)
    dec_col = _col_bcast(dec_full, d_ssd)

    y_parts = []
    for g in range(SSD_GROUPS):
        b_g = xc[:, d_ssd + g * SSD_STATE:d_ssd + (g + 1) * SSD_STATE]
        c_g = xc[:, d_ssd + (SSD_GROUPS + g) * SSD_STATE:d_ssd + (SSD_GROUPS + g + 1) * SSD_STATE]
        h_g = st_ref[0, g * gw:(g + 1) * gw, :]
        cb = jnp.sum(c_g * b_g, axis=-1, keepdims=True)
        c8 = jnp.broadcast_to(c_g, (SUBLANES, SSD_STATE)).astype(BF16)
        y_int = _dot_nt(c8, h_g.astype(BF16))[0:1]
        sl = slice(g * gw, (g + 1) * gw)
        y_parts.append(cb * dtx[:, sl] + dec_full[:, sl] * y_int)
        st_out_ref[0, sl, :] = h_g * dec_col[sl, :] + dtx_col[sl, :] * b_g
    y = jnp.concatenate(y_parts, axis=1) + dsk_ref[...] * xs
    y = y * _silu(z_ref[0])
    y = y * lax.rsqrt(jnp.mean(y * y, axis=-1, keepdims=True) + LN_EPS) * ng_ref[...]
    y_ref[0] = y.astype(y_ref.dtype)


def _ssd_step(xbc, z, gates, buf, state, p):
    nb, cd = xbc.shape
    d_ssd = z.shape[1]
    n_heads = d_ssd // SSD_HEAD_DIM
    row3 = lambda b: (b, 0, 0)
    cst = lambda b: (0, 0)
    st2 = state.reshape(nb, d_ssd, SSD_STATE)
    sb = STEP_SEQS
    assert nb % sb == 0
    y, st_new = pl.pallas_call(
        functools.partial(_per_sequence(_ssd_step_one, 5, 2), d_ssd=d_ssd, n_heads=n_heads),
        grid=(nb // sb,),
        in_specs=[pl.BlockSpec((sb, 1, cd), row3),
                  pl.BlockSpec((sb, 1, d_ssd), row3),
                  pl.BlockSpec((sb, 1, 2 * LANES), row3),
                  pl.BlockSpec((sb, CONV_K - 1, cd), row3),
                  pl.BlockSpec((sb, d_ssd, SSD_STATE), row3),
                  pl.BlockSpec((CONV_K, cd), cst),
                  pl.BlockSpec((1, cd), cst),
                  pl.BlockSpec((1, LANES), cst),
                  pl.BlockSpec((1, LANES), cst),
                  pl.BlockSpec((1, d_ssd), cst),
                  pl.BlockSpec((1, d_ssd), cst),
                  pl.BlockSpec((LANES, d_ssd), cst)],
        out_specs=[pl.BlockSpec((sb, 1, d_ssd), row3),
                   pl.BlockSpec((sb, d_ssd, SSD_STATE), row3)],
        out_shape=[jax.ShapeDtypeStruct((nb, 1, d_ssd), BF16),
                   jax.ShapeDtypeStruct((nb, d_ssd, SSD_STATE), F32)],
        compiler_params=_cparams("arbitrary"),
        name="ssd_step",
    )(xbc.reshape(nb, 1, cd), z.reshape(nb, 1, d_ssd), gates.reshape(nb, 1, 2 * LANES), buf, st2,
      p["conv_ssd_w"], p["conv_ssd_b"], p["dt_bias_row"], p["a_log_row"], p["d_skip_row"], p["ssd_norm_g"],
      p["ssd_expand"])
    return y.reshape(nb, d_ssd), st_new.reshape(state.shape)


def _mlstm_step_one(u_ref, v_ref, o_ref, g_ref, buf_ref, c_ref, n_ref, m_ref, cw_ref, cb_ref, wq_ref, wk_ref,
                    bi_ref, bf_ref, ng_ref, hexp_ref, y_ref, c_out_ref, n_out_ref, m_out_ref, *, n_heads, dk):
    conv = cb_ref[...] + cw_ref[CONV_K - 1:CONV_K, :] * u_ref[0]
    for j in range(CONV_K - 1):
        conv = conv + cw_ref[j:j + 1, :] * buf_ref[0, j:j + 1, :]
    uc = jnp.broadcast_to(_silu(conv), (SUBLANES, conv.shape[1])).astype(BF16)

    gt = g_ref[0]
    logi = gt[:, 0:LANES] + bi_ref[...]
    logf = _log_sigmoid(gt[:, LANES:2 * LANES] + bf_ref[...])
    m_prev = m_ref[0]
    inter = logf + m_prev
    m_t = jnp.maximum(inter, logi)
    w_inter = jnp.exp(inter - m_t)
    w_in = jnp.exp(logi - m_t)
    floor = jnp.exp(-m_t)
    rows = jnp.concatenate([w_inter, w_in, floor, jnp.zeros((SUBLANES - 3, LANES), F32)], axis=0)
    full = _dot_sel_r(rows, hexp_ref[...])
    scale = dk ** -0.5

    for h in range(n_heads):
        sl = slice(h * dk, (h + 1) * dk)
        q = _dot(uc[:, sl], wq_ref[h])[0:1]
        k = _dot(uc[:, sl], wk_ref[h])[0:1] * scale
        v_h = v_ref[0][:, sl]
        wi = full[0:1, sl]
        ws = full[1:2, sl]
        fl = full[2:3, sl]
        c_h = c_ref[0, h]
        n_h = n_ref[0, h:h + 1, :]
        att = jnp.sum(q * k, axis=-1, keepdims=True) * ws
        q8 = jnp.broadcast_to(q, (SUBLANES, dk)).astype(BF16)
        num = att * v_h + wi * _dot(q8, c_h.astype(BF16))[0:1]
        den = att + wi * jnp.sum(q * n_h, axis=-1, keepdims=True)
        hout = num / jnp.maximum(jnp.abs(den), fl)
        mu = jnp.mean(hout, axis=-1, keepdims=True)
        var = jnp.mean(jnp.square(hout - mu), axis=-1, keepdims=True)
        hn = (hout - mu) * lax.rsqrt(var + LN_EPS) * ng_ref[:, sl]
        y_ref[0, :, sl] = (_sigmoid(o_ref[0][:, sl]) * hn).astype(y_ref.dtype)
        kw = k * ws
        kw_col = _col_bcast(kw, dk)
        wi_full = jnp.broadcast_to(wi, (dk, dk))
        for half in range(dk // LANES):
            hs = slice(half * LANES, (half + 1) * LANES)
            c_out_ref[0, h, :, hs] = wi_full[:, hs] * c_h[:, hs] + kw_col * v_h[:, hs]
        n_out_ref[0, h:h + 1, :] = wi * n_h + kw
    m_out_ref[0] = m_t


def _mlstm_step(u, v, o, gates, buf, c_st, n_st, m_row, p):
    nb, dm = u.shape
    n_heads, dk, _ = p["w_mq"].shape
    row3 = lambda b: (b, 0, 0)
    cst = lambda b: (0, 0)
    cst3 = lambda b: (0, 0, 0)
    r3 = lambda a: a.reshape(nb, 1, a.shape[-1])
    sb = STEP_SEQS
    assert nb % sb == 0
    y, c_new, n_new, m_new = pl.pallas_call(
        functools.partial(_per_sequence(_mlstm_step_one, 8, 4), n_heads=n_heads, dk=dk),
        grid=(nb // sb,),
        in_specs=[pl.BlockSpec((sb, 1, dm), row3),
                  pl.BlockSpec((sb, 1, dm), row3),
                  pl.BlockSpec((sb, 1, dm), row3),
                  pl.BlockSpec((sb, 1, 2 * LANES), row3),
                  pl.BlockSpec((sb, CONV_K - 1, dm), row3),
                  pl.BlockSpec((sb, n_heads, dk, dk), lambda b: (b, 0, 0, 0)),
                  pl.BlockSpec((sb, n_heads, dk), row3),
                  pl.BlockSpec((sb, 1, LANES), row3),
                  pl.BlockSpec((CONV_K, dm), cst),
                  pl.BlockSpec((1, dm), cst),
                  pl.BlockSpec((n_heads, dk, dk), cst3),
                  pl.BlockSpec((n_heads, dk, dk), cst3),
                  pl.BlockSpec((1, LANES), cst),
                  pl.BlockSpec((1, LANES), cst),
                  pl.BlockSpec((1, dm), cst),
                  pl.BlockSpec((LANES, dm), cst)],
        out_specs=[pl.BlockSpec((sb, 1, dm), row3),
                   pl.BlockSpec((sb, n_heads, dk, dk), lambda b: (b, 0, 0, 0)),
                   pl.BlockSpec((sb, n_heads, dk), row3),
                   pl.BlockSpec((sb, 1, LANES), row3)],
        out_shape=[jax.ShapeDtypeStruct((nb, 1, dm), BF16),
                   jax.ShapeDtypeStruct(c_st.shape, F32),
                   jax.ShapeDtypeStruct(n_st.shape, F32),
                   jax.ShapeDtypeStruct((nb, 1, LANES), F32)],
        compiler_params=_cparams("arbitrary"),
        name="mlstm_step",
    )(r3(u), r3(v), r3(o), r3(gates), buf, c_st, n_st, m_row, p["conv_m_w"], p["conv_m_b"], p["w_mq"],
      p["w_mk"], p["b_i_row"], p["b_f_row"], p["mlstm_norm_g"], p["mlstm_expand"])
    return y.reshape(nb, dm), c_new, n_new, m_new


def _outproj_kernel(ys_ref, hm_ref, x_ref, w_ref, g_ref, b_ref, x1t_ref, *, alpha):
    d_ssd = ys_ref.shape[1]
    tm = ys_ref.shape[0]
    subs = [slice(r0, min(r0 + OUTPROJ_ROWS, tm)) for r0 in range(0, tm, OUTPROJ_ROWS)]
    mix = [_dot(ys_ref[rs, :], w_ref[0:d_ssd, :]) + _dot(hm_ref[rs, :], w_ref[d_ssd:, :]) for rs in subs]
    for rs, mx in zip(subs, mix):
        r = alpha * x_ref[rs, :] + mx
        mu = jnp.mean(r, axis=-1, keepdims=True)
        var = jnp.mean(jnp.square(r - mu), axis=-1, keepdims=True)
        x1 = (r - mu) * lax.rsqrt(var + LN_EPS) * g_ref[...] + b_ref[...]
        x1t_ref[:, rs] = x1.T


def _outproj(ys, hm, x2, p, alpha, tm):
    m, d = x2.shape
    d_ssd, d_m = ys.shape[1], hm.shape[1]
    return pl.pallas_call(
        functools.partial(_outproj_kernel, alpha=alpha),
        grid=(m // tm,),
        in_specs=[pl.BlockSpec((tm, d_ssd), lambda i: (i, 0)),
                  pl.BlockSpec((tm, d_m), lambda i: (i, 0)),
                  pl.BlockSpec((tm, d), lambda i: (i, 0)),
                  pl.BlockSpec((d_ssd + d_m, d), lambda i: (0, 0)),
                  pl.BlockSpec((1, d), lambda i: (0, 0)),
                  pl.BlockSpec((1, d), lambda i: (0, 0))],
        out_specs=pl.BlockSpec((d, tm), lambda i: (0, i)),
        out_shape=jax.ShapeDtypeStruct((d, m), F32),
        compiler_params=_cparams("arbitrary"),
        name="outproj_ln1",
    )(ys, hm, x2, p["w_out"], p["ln1_g"], p["ln1_b"])


def _oddeven_merge_sort_pairs(n):
    pairs = []
    p = 1
    while p < n:
        k = p
        while k >= 1:
            for j in range(k % p, n - k, 2 * k):
                for i in range(min(k, n - j - k)):
                    if (i + j) // (2 * p) == (i + j + k) // (2 * p):
                        pairs.append((i + j, i + j + k))
            k //= 2
        p *= 2
    return pairs


_SORT_TOPK = _oddeven_merge_sort_pairs(PEER_TOPK)


def _compare_exchange(lst, i, j):
    hi, lo = jnp.maximum(lst[i], lst[j]), jnp.minimum(lst[i], lst[j])
    lst[i], lst[j] = hi, lo


def _bitonic_to_sorted(t):
    d = PEER_TOPK // 2
    while d >= 1:
        for i in range(PEER_TOPK):
            if i & d == 0:
                _compare_exchange(t, i, i + d)
        d //= 2
    return t


def _merge_top(x, y):
    neg = jnp.full_like(x[0], -jnp.inf)
    x = x + [neg] * (PEER_TOPK - len(x))
    y = y + [neg] * (PEER_TOPK - len(y))
    return _bitonic_to_sorted([jnp.maximum(x[k], y[PEER_TOPK - 1 - k]) for k in range(PEER_TOPK)])


def _top_rows(s):
    lst = [s[SUBLANES * g:SUBLANES * (g + 1), :] for g in range(PEER_TOPK)]
    for i, j in _SORT_TOPK:
        _compare_exchange(lst, i, j)
    shift = SUBLANES // 2
    while shift >= 1:
        lst = _merge_top(lst, [pltpu.roll(a, shift, 0) for a in lst])
        shift //= 2
    return lst


def _peer_topk_kernel(x1t_ref, wqt_ref, keys_ref, s1_ref, s2_ref, st_ref, b_scr, q_scr, a_scr, *, n_heads):
    tq = x1t_ref.shape[1]
    q_scr[...] = _dot(wqt_ref[...], x1t_ref[...].astype(BF16))
    half = q_scr.shape[0] // (2 * n_heads)

    for h in range(n_heads):
        for side, (s_ref, top_scr) in enumerate(((s1_ref, a_scr), (s2_ref, b_scr))):
            r0 = (2 * h + side) * half
            s = _dot(keys_ref[h, side], q_scr[r0:r0 + half, :].astype(BF16))
            for t in range(tq // LANES):
                ls = slice(t * LANES, (t + 1) * LANES)
                s_ref[h, t] = s[:, ls]
                top = _top_rows(s[:, ls])
                for k in range(PEER_TOPK):
                    top_scr[k, h:h + 1, ls] = top[k][0:1, :]

    for t in range(tq // LANES):
        ls = slice(t * LANES, (t + 1) * LANES)
        a = [a_scr[k, :, ls] for k in range(PEER_TOPK)]
        b = [b_scr[k, :, ls] for k in range(PEER_TOPK)]
        single = PEER_TOPK // 2
        lists = [[a[i] + b[j] for j in range(PEER_TOPK // (i + 1))] for i in range(single)]
        lists.append([a[i] + b[0] for i in range(single, PEER_TOPK)])
        merged = lists[0]
        for other in lists[1:]:
            merged = _merge_top(merged, other)
        tau = merged[PEER_TOPK - 1]
        top = a[0] + b[0]
        zsum = jnp.zeros_like(tau)
        for cand in (c for lst in lists for c in lst):
            zsum = zsum + jnp.where(cand >= tau, jnp.exp(cand - top), 0.0)
        st_ref[0, :, ls] = tau
        st_ref[1, :, ls] = a[0]
        st_ref[2, :, ls] = b[0]
        st_ref[3, :, ls] = 1.0 / zsum


def _peer_topk(x1t, p, tq):
    d, m = x1t.shape
    n_heads, _, n_keys, half = p["peer_keys"].shape
    nq = p["peer_wqT"].shape[0]
    assert n_heads == SUBLANES and n_keys == PEER_TOPK * SUBLANES
    keyb = lambda i: (0, i, 0, 0)
    return pl.pallas_call(
        functools.partial(_peer_topk_kernel, n_heads=n_heads),
        grid=(m // tq,),
        in_specs=[pl.BlockSpec((d, tq), lambda i: (0, i)),
                  pl.BlockSpec((nq, d), lambda i: (0, 0)),
                  pl.BlockSpec((n_heads, 2, n_keys, half), lambda i: (0, 0, 0, 0))],
        out_specs=[pl.BlockSpec((n_heads, tq // LANES, n_keys, LANES), keyb),
                   pl.BlockSpec((n_heads, tq // LANES, n_keys, LANES), keyb),
                   pl.BlockSpec((N_STATS, SUBLANES, tq), lambda i: (0, 0, i)),
                   pl.BlockSpec((PEER_TOPK, SUBLANES, tq), lambda i: (0, 0, i))],
        out_shape=[jax.ShapeDtypeStruct((n_heads, m // LANES, n_keys, LANES), F32),
                   jax.ShapeDtypeStruct((n_heads, m // LANES, n_keys, LANES), F32),
                   jax.ShapeDtypeStruct((N_STATS, SUBLANES, m), F32),
                   jax.ShapeDtypeStruct((PEER_TOPK, SUBLANES, m), F32)],
        scratch_shapes=[pltpu.VMEM((nq, tq), F32),
                        pltpu.VMEM((PEER_TOPK, SUBLANES, tq), F32)],
        compiler_params=_cparams("arbitrary"),
        name="peer_topk",
    )(x1t, p["peer_wqT"], p["peer_keys"])


def _peer_dense_kernel(x1t_ref, s1_ref, s2_ref, st_ref, bl_ref, u_ref, vt_ref, g_ref, b_ref, out_ref,
                       xb_scr, e2_scr, c_scr, thr_scr, acc_scr, s_scr, w_scr, thr_r, cr, *, alpha, n_heads):
    j = pl.program_id(1)
    ec = u_ref.shape[0]
    tb = x1t_ref.shape[1]
    nt = tb // LANES
    n_keys = s1_ref.shape[2]
    na = ec // n_keys

    @pl.when(j == 0)
    def _():
        xb_scr[...] = x1t_ref[...].astype(BF16)
        for h in range(n_heads):
            for t in range(nt):
                ls = slice(t * LANES, (t + 1) * LANES)
                e2_scr[h, t] = jnp.exp(s2_ref[h, t] - st_ref[2, h:h + 1, ls])
                s1 = s1_ref[h, t]
                c_scr[h, t] = jnp.exp(s1 - st_ref[1, h:h + 1, ls]) * (0.5 * st_ref[3, h:h + 1, ls])
                tau = st_ref[0, h:h + 1, ls]
                thr = jnp.full((n_keys, LANES), jnp.inf, F32)
                for k in range(PEER_TOPK):
                    bk = bl_ref[k, h:h + 1, ls]
                    thr = jnp.where(s1 + bk >= tau, bk, thr)
                thr_scr[h, t] = thr
        acc_scr[...] = jnp.zeros_like(acc_scr)

    s_scr[:, 0:tb] = _dot(u_ref[...], xb_scr[...])

    g0 = pl.multiple_of(j * na, na)
    for h in range(n_heads):
        for t in range(nt):
            ls = slice(t * LANES, (t + 1) * LANES)
            thr_blk = thr_scr[h, t, pl.ds(g0, na), :]
            c_blk = c_scr[h, t, pl.ds(g0, na), :]
            for al in range(na):
                thr_r[h * na + al, :, ls] = thr_blk[al:al + 1, :]
                cr[h * na + al, :, ls] = c_blk[al:al + 1, :]

    def first_key(al, carry):
        r0 = pl.multiple_of(al * n_keys, n_keys)
        for t in range(nt):
            ls = slice(t * LANES, (t + 1) * LANES)
            gate = jnp.zeros((n_keys, LANES), F32)
            for h in range(n_heads):
                gate = gate + jnp.where(s2_ref[h, t] >= thr_r[h * na + al][:, ls],
                                        e2_scr[h, t] * cr[h * na + al][:, ls], 0.0)
            x = s_scr[pl.ds(r0, n_keys), ls]
            hid = x + x * lax.erf(x * (1.0 / math.sqrt(2.0)))
            w_scr[pl.ds(r0, n_keys), ls] = (gate * hid).astype(BF16)
        return carry

    lax.fori_loop(0, na, first_key, 0)
    acc_scr[...] += _dot(vt_ref[...], w_scr[:, 0:tb])

    @pl.when(j == pl.num_programs(1) - 1)
    def _():
        r = alpha * x1t_ref[...] + acc_scr[...]
        mu = jnp.mean(r, axis=0, keepdims=True)
        var = jnp.mean(jnp.square(r - mu), axis=0, keepdims=True)
        y = (r - mu) * lax.rsqrt(var + LN_EPS) * g_ref[...] + b_ref[...]
        out_ref[...] = y.T


def _peer_dense(x1t, s1, s2, st, blist, p, alpha, tb):
    d, m = x1t.shape
    n_heads, _, n_keys, _ = s1.shape
    n_exp = p["peer_u"].shape[0]
    ec = PEER_CHUNK
    nj = n_exp // ec
    na = ec // n_keys
    assert na % SUBLANES == 0
    nt = tb // LANES
    keyb = lambda i, j: (0, i, 0, 0)
    tokb = lambda i, j: (0, 0, i)
    pitch = tb + LANES
    return pl.pallas_call(
        functools.partial(_peer_dense_kernel, alpha=alpha, n_heads=n_heads),
        grid=(m // tb, nj),
        in_specs=[pl.BlockSpec((d, tb), lambda i, j: (0, i)),
                  pl.BlockSpec((n_heads, nt, n_keys, LANES), keyb),
                  pl.BlockSpec((n_heads, nt, n_keys, LANES), keyb),
                  pl.BlockSpec((N_STATS, SUBLANES, tb), tokb),
                  pl.BlockSpec((PEER_TOPK, SUBLANES, tb), tokb),
                  pl.BlockSpec((ec, d), lambda i, j: (j, 0)),
                  pl.BlockSpec((d, ec), lambda i, j: (0, j)),
                  pl.BlockSpec((d, 1), lambda i, j: (0, 0)),
                  pl.BlockSpec((d, 1), lambda i, j: (0, 0))],
        out_specs=pl.BlockSpec((tb, d), lambda i, j: (i, 0)),
        out_shape=jax.ShapeDtypeStruct((m, d), F32),
        scratch_shapes=[pltpu.VMEM((d, tb), BF16),
                        pltpu.VMEM((n_heads, nt, n_keys, LANES), F32),
                        pltpu.VMEM((n_heads, nt, n_keys, LANES), F32),
                        pltpu.VMEM((n_heads, nt, n_keys, LANES), F32),
                        pltpu.VMEM((d, tb), F32),
                        pltpu.VMEM((ec, pitch), F32),
                        pltpu.VMEM((ec, pitch), BF16),
                        pltpu.VMEM((n_heads * na, 1, tb), F32),
                        pltpu.VMEM((n_heads * na, 1, tb), F32)],
        compiler_params=_cparams("arbitrary", "arbitrary"),
        name="peer_dense",
    )(x1t, s1, s2, st, blist, p["peer_u"], p["peer_vT"], p["ln2_g_col"], p["ln2_b_col"])


def _pad_row(vec, offset):
    return jnp.zeros((1, LANES), F32).at[0, offset:offset + vec.shape[0]].set(vec.astype(F32))


def _prep_layer(l, w_in, conv_ssd_w, conv_ssd_b, dt_bias, a_log, d_skip, ssd_norm_g, conv_m_w, conv_m_b,
                w_mq, w_mk, b_igate, b_fgate, mlstm_norm_g, w_out, ln1_g, ln1_b, peer_wq, peer_keys,
                peer_u, peer_v, ln2_g, ln2_b):
    d_model = w_in.shape[1]
    d_ssd = ssd_norm_g.shape[1]
    cd = conv_ssd_w.shape[2]
    n_sh = dt_bias.shape[1]
    d_m = conv_m_w.shape[2]
    n_mh = w_mq.shape[1]
    dk = w_mq.shape[2]
    assert n_sh <= GATE_I0 and GATE_I0 + n_mh <= LANES and d_ssd == n_sh * SSD_HEAD_DIM
    w = w_in[l]
    offs = [0]
    for n in (d_ssd, cd, n_sh, d_m, d_m, d_m, n_mh, n_mh):
        offs.append(offs[-1] + n)
    z_w, xbc_w, dt_w, u_w, v_w, o_w, i_w, f_w = (w[:, offs[k]:offs[k + 1]] for k in range(8))
    gate_a = jnp.zeros((d_model, LANES), F32).at[:, 0:n_sh].set(dt_w).at[:, GATE_I0:GATE_I0 + n_mh].set(i_w)
    gate_b = jnp.zeros((d_model, LANES), F32).at[:, GATE_I0:GATE_I0 + n_mh].set(f_w)
    w_cat = jnp.concatenate([z_w, xbc_w, u_w, v_w, o_w, gate_a, gate_b], axis=1).astype(BF16)
    widths = (d_ssd, cd, d_m, d_m, d_m, 2 * LANES)

    lane = jnp.arange(LANES)[:, None]
    ssd_expand = (lane == (jnp.arange(d_ssd)[None, :] // SSD_HEAD_DIM)).astype(BF16)
    mlstm_expand = (lane == (GATE_I0 + jnp.arange(d_m)[None, :] // dk)).astype(BF16)
    t = jnp.arange(CHUNK)
    tril = (t[:, None] >= t[None, :]).astype(BF16)
    p = {
        "w_cat": w_cat,
        "conv_ssd_w": conv_ssd_w[l], "conv_ssd_b": conv_ssd_b[l][None, :],
        "dt_bias_row": _pad_row(dt_bias[l], 0), "a_log_row": _pad_row(a_log[l], 0),
        "d_skip_row": jnp.repeat(d_skip[l], SSD_HEAD_DIM)[None, :], "ssd_norm_g": ssd_norm_g[l][None, :],
        "conv_m_w": conv_m_w[l], "conv_m_b": conv_m_b[l][None, :],
        "w_mq": w_mq[l].astype(BF16), "w_mk": w_mk[l].astype(BF16),
        "b_i_row": _pad_row(b_igate[l], GATE_I0), "b_f_row": _pad_row(b_fgate[l], GATE_I0),
        "mlstm_norm_g": mlstm_norm_g[l].reshape(1, d_m),
        "w_out": w_out[l].astype(BF16), "ln1_g": ln1_g[l][None, :], "ln1_b": ln1_b[l][None, :],
        "peer_wqT": peer_wq[l].T.astype(BF16), "peer_keys": peer_keys[l].astype(BF16),
        "peer_u": peer_u[l].astype(BF16),
        "peer_vT": peer_v[l].astype(BF16).T,
        "ln2_g_col": ln2_g[l][:, None], "ln2_b_col": ln2_b[l][:, None],
        "tril": tril, "ssd_expand": ssd_expand, "mlstm_expand": mlstm_expand,
    }
    return p, widths


def _tile_sizes(m):
    tm = 256 if m % 256 == 0 else LANES
    tb = 512 if m % 512 == 0 else LANES
    return tm, tb


def _peer_block(x1t, p, alpha):
    m = x1t.shape[1]
    tm, tb = _tile_sizes(m)
    s1, s2, st, blist = _peer_topk(x1t, p, tm)
    return _peer_dense(x1t, s1, s2, st, blist, p, alpha, tb)


def _layer_prompt(x, p, widths, alpha):
    bsz, seq, d = x.shape
    assert seq % CHUNK == 0 and seq >= CONV_K - 1
    x2 = x.reshape(bsz * seq, d)
    tm, _ = _tile_sizes(bsz * seq)
    z, xbc, u, v, o, gates = _inproj(x2, p["w_cat"], widths, tm)
    y_s, h_t = _ssd_chunk(xbc, z, gates, p, bsz, seq)
    h_m, c_new, n_new, m_new = _mlstm_chunk(u, v, o, gates, p, bsz, seq)
    x1t = _outproj(y_s, h_m, x2, p, alpha, 2 * tm if (bsz * seq) % (2 * tm) == 0 else tm)
    y = _peer_block(x1t, p, alpha).reshape(bsz, seq, d)
    n_mh = c_new.shape[1]
    d_ssd = z.shape[1]
    n_sh = d_ssd // SSD_HEAD_DIM
    hg = n_sh // SSD_GROUPS
    new_ssd = h_t.reshape(bsz, SSD_GROUPS, SSD_STATE, hg, SSD_HEAD_DIM).transpose(0, 1, 3, 4, 2)
    new_ssd = new_ssd.reshape(bsz, n_sh, SSD_HEAD_DIM, SSD_STATE)
    states = (new_ssd,
              xbc.reshape(bsz, seq, -1)[:, seq - (CONV_K - 1):, :],
              c_new,
              n_new[:, :, 0, :],
              m_new[:, 0, GATE_I0:GATE_I0 + n_mh],
              u.reshape(bsz, seq, -1)[:, seq - (CONV_K - 1):, :])
    return y, states


def _layer_sample(x, st, p, widths, alpha):
    st_ssd, st_ssd_conv, st_c, st_n, st_m, st_mconv = st
    nb, seq, d = x.shape
    assert seq == 1 and nb % LANES == 0
    x2 = x.reshape(nb, d)
    z, xbc, u, v, o, gates = _inproj(x2, p["w_cat"], widths, LANES)
    y_s, new_ssd = _ssd_step(xbc, z, gates, st_ssd_conv, st_ssd, p)
    n_mh = st_c.shape[1]
    m_row = jnp.zeros((nb, 1, LANES), F32).at[:, 0, GATE_I0:GATE_I0 + n_mh].set(st_m)
    h_m, c_new, n_new, m_new = _mlstm_step(u, v, o, gates, st_mconv, st_c, st_n, m_row, p)
    x1t = _outproj(y_s, h_m, x2, p, alpha, LANES)
    y = _peer_block(x1t, p, alpha).reshape(nb, seq, d)
    states = (new_ssd,
              jnp.concatenate([st_ssd_conv[:, 1:, :], xbc[:, None, :]], axis=1),
              c_new,
              n_new,
              m_new[:, 0, GATE_I0:GATE_I0 + n_mh],
              jnp.concatenate([st_mconv[:, 1:, :], u[:, None, :]], axis=1))
    return y, states


def kernel(x_prompt, x_sample, state_ssd, state_ssd_conv, state_mlstm_c, state_mlstm_n, state_mlstm_m,
           state_mlstm_conv, w_in, conv_ssd_w, conv_ssd_b, dt_bias, a_log, d_skip, ssd_norm_g, conv_m_w,
           conv_m_b, w_mq, w_mk, b_igate, b_fgate, mlstm_norm_g, w_out, ln1_g, ln1_b, peer_wq, peer_keys,
           peer_u, peer_v, ln2_g, ln2_b):
    depth = w_in.shape[0]
    alpha = (2.0 * depth) ** 0.25
    states = (state_ssd, state_ssd_conv, state_mlstm_c, state_mlstm_n, state_mlstm_m, state_mlstm_conv)
    yp, ys = x_prompt, x_sample
    p_list, s_list = [], []
    for l in range(depth):
        p, widths = _prep_layer(l, w_in, conv_ssd_w, conv_ssd_b, dt_bias, a_log, d_skip, ssd_norm_g, conv_m_w,
                                conv_m_b, w_mq, w_mk, b_igate, b_fgate, mlstm_norm_g, w_out, ln1_g, ln1_b,
                                peer_wq, peer_keys, peer_u, peer_v, ln2_g, ln2_b)
        yp, p_new = _layer_prompt(yp, p, widths, alpha)
        ys, s_new = _layer_sample(ys, tuple(s[l] for s in states), p, widths, alpha)
        p_list.append(p_new)
        s_list.append(s_new)
    p_out = tuple(jnp.stack([pn[i] for pn in p_list]) for i in range(6))
    s_out = tuple(jnp.stack([sn[i] for sn in s_list]) for i in range(6))
    return (yp, ys) + p_out + s_out
```

```python
import functools
import math

import jax
import jax.numpy as jnp
from jax import lax
from jax.experimental import pallas as pl
from jax.experimental.pallas import tpu as pltpu

F32 = jnp.float32
BF16 = jnp.bfloat16

LANES = 128
SUBLANES = 8
CHUNK = 128
CONV_K = 4
LN_EPS = 1e-5
VMEM_LIMIT = 58 * 1024 * 1024

SSD_HEAD_DIM = 64
SSD_STATE = 128
SSD_GROUPS = 2
PEER_TOPK = 16
GATE_I0 = 16
N_STATS = 4
PEER_CHUNK = 2048
STEP_SEQS = 4
OUTPROJ_ROWS = 256


def _cparams(*sem):
    return pltpu.CompilerParams(dimension_semantics=sem, vmem_limit_bytes=VMEM_LIMIT)


def _dot(a, b):
    return jnp.dot(a, b, preferred_element_type=F32)


def _dot_nt(a, b):
    return lax.dot_general(a, b, (((1,), (1,)), ((), ())), preferred_element_type=F32)


def _split3(x):
    h = x.astype(BF16)
    r = x - h.astype(F32)
    m = r.astype(BF16)
    lo = (r - m.astype(F32)).astype(BF16)
    return h, m, lo


def _dot_sel_l(sel_bf16, x):
    h, m, lo = _split3(x)
    return _dot(sel_bf16, h) + _dot(sel_bf16, m) + _dot(sel_bf16, lo)


def _dot_sel_r(x, sel_bf16):
    h, m, lo = _split3(x)
    return _dot(h, sel_bf16) + _dot(m, sel_bf16) + _dot(lo, sel_bf16)


def _sigmoid(x):
    return 1.0 / (1.0 + jnp.exp(-x))


def _silu(x):
    return x * _sigmoid(x)


def _softplus(x):
    return jnp.maximum(x, 0.0) + jnp.log1p(jnp.exp(-jnp.abs(x)))


def _log_sigmoid(x):
    return -_softplus(-x)


def _gelu_exact(x):
    return 0.5 * x * (1.0 + lax.erf(x * (1.0 / math.sqrt(2.0))))


def _inproj_kernel(x_ref, w_ref, *out_refs):
    xb = x_ref[...].astype(BF16)
    off = 0
    for ref in out_refs:
        n = ref.shape[-1]
        ref[...] = _dot(xb, w_ref[:, off:off + n])
        off += n


def _inproj(x2, w_cat, widths, tm):
    m, d = x2.shape
    n_all = w_cat.shape[1]
    return pl.pallas_call(
        _inproj_kernel,
        grid=(m // tm,),
        in_specs=[pl.BlockSpec((tm, d), lambda i: (i, 0)),
                  pl.BlockSpec((d, n_all), lambda i: (0, 0))],
        out_specs=[pl.BlockSpec((tm, n), lambda i: (i, 0)) for n in widths],
        out_shape=[jax.ShapeDtypeStruct((m, n), F32) for n in widths],
        compiler_params=_cparams("arbitrary"),
        name="inproj",
    )(x2, w_cat)


def _ssd_chunk_kernel(xbc_ref, z_ref, g_ref, cw_ref, cb_ref, dtb_ref, alog_ref, dsk_ref, ng_ref,
                      tril_ref, exp_ref, y_ref, hT_out_ref, xpad, hT, ybuf, *, nc, d_ssd, n_heads):
    c = pl.program_id(1)
    cd = xpad.shape[1]
    hg = n_heads // SSD_GROUPS
    gw = d_ssd // SSD_GROUPS

    @pl.when(c == 0)
    def _():
        xpad[0:SUBLANES, :] = jnp.zeros((SUBLANES, cd), F32)
        hT[...] = jnp.zeros_like(hT)

    xpad[SUBLANES:SUBLANES + CHUNK, :] = xbc_ref[...]
    conv = cb_ref[...]
    for j in range(CONV_K):
        r0 = SUBLANES - (CONV_K - 1) + j
        conv = conv + cw_ref[j:j + 1, :] * xpad[r0:r0 + CHUNK, :]
    xpad[0:SUBLANES, :] = xpad[CHUNK:CHUNK + SUBLANES, :]
    xc = _silu(conv)
    xs = xc[:, 0:d_ssd]

    lane = lax.broadcasted_iota(jnp.int32, (1, LANES), 1)
    a_row = jnp.where(lane < n_heads, -jnp.exp(alog_ref[...]), 0.0)
    dt = _softplus(g_ref[:, 0:LANES] + dtb_ref[...])
    acs = _dot_sel_l(tril_ref[...], dt * a_row)
    dt_t = dt.T
    acs_t = acs.T
    eacs = jnp.exp(acs)
    dec = jnp.exp(acs[CHUNK - 1:CHUNK, :] - acs) * dt
    both = _dot_sel_r(jnp.concatenate([eacs, dec], axis=0), exp_ref[...])
    eacs_full = both[0:CHUNK]
    dec_full = both[CHUNK:2 * CHUNK]

    row = lax.broadcasted_iota(jnp.int32, (CHUNK, CHUNK), 0)
    col = lax.broadcasted_iota(jnp.int32, (CHUNK, CHUNK), 1)
    causal = row >= col

    gs = range(SSD_GROUPS)
    b_g = [xc[:, d_ssd + g * SSD_STATE:d_ssd + (g + 1) * SSD_STATE] for g in gs]
    c_g = [xc[:, d_ssd + (SSD_GROUPS + g) * SSD_STATE:d_ssd + (SSD_GROUPS + g + 1) * SSD_STATE] for g in gs]
    b_gb = [x.astype(BF16) for x in b_g]
    c_gb = [x.astype(BF16) for x in c_g]
    cb = [_dot_nt(c_gb[g], b_gb[g]) for g in gs]
    h_old = [hT[g] for g in gs]
    y_inter = jnp.concatenate([_dot(c_gb[g], h_old[g].astype(BF16)) for g in gs], axis=1)
    for h in range(n_heads):
        diff = acs[:, h:h + 1] - acs_t[h:h + 1, :]
        lmat = jnp.exp(jnp.where(causal, diff, -jnp.inf))
        mm = cb[h // hg] * lmat * dt_t[h:h + 1, :]
        x_h = xs[:, h * SSD_HEAD_DIM:(h + 1) * SSD_HEAD_DIM]
        ybuf[:, h * SSD_HEAD_DIM:(h + 1) * SSD_HEAD_DIM] = _dot(mm.astype(BF16), x_h.astype(BF16))
    for g in gs:
        gsl = slice(g * gw, (g + 1) * gw)
        xd = xs[:, gsl] * dec_full[:, gsl]
        hT[g] = h_old[g] * eacs_full[CHUNK - 1:CHUNK, gsl] + _dot(b_g[g].T.astype(BF16), xd.astype(BF16))

    y = ybuf[...] + y_inter * eacs_full + dsk_ref[...] * xs
    y = y * _silu(z_ref[...])
    y = y * lax.rsqrt(jnp.mean(y * y, axis=-1, keepdims=True) + LN_EPS) * ng_ref[...]
    y_ref[...] = y.astype(y_ref.dtype)

    @pl.when(c == nc - 1)
    def _():
        hT_out_ref[0] = hT[...]


def _ssd_chunk(xbc, z, gates, p, bsz, seq):
    nc = seq // CHUNK
    cd = xbc.shape[1]
    d_ssd = z.shape[1]
    n_heads = d_ssd // SSD_HEAD_DIM
    gw = d_ssd // SSD_GROUPS
    tok = lambda b, c: (b * nc + c, 0)
    cst = lambda b, c: (0, 0)
    return pl.pallas_call(
        functools.partial(_ssd_chunk_kernel, nc=nc, d_ssd=d_ssd, n_heads=n_heads),
        grid=(bsz, nc),
        in_specs=[pl.BlockSpec((CHUNK, cd), tok),
                  pl.BlockSpec((CHUNK, d_ssd), tok),
                  pl.BlockSpec((CHUNK, 2 * LANES), tok),
                  pl.BlockSpec((CONV_K, cd), cst),
                  pl.BlockSpec((1, cd), cst),
                  pl.BlockSpec((1, LANES), cst),
                  pl.BlockSpec((1, LANES), cst),
                  pl.BlockSpec((1, d_ssd), cst),
                  pl.BlockSpec((1, d_ssd), cst),
                  pl.BlockSpec((CHUNK, CHUNK), cst),
                  pl.BlockSpec((LANES, d_ssd), cst)],
        out_specs=[pl.BlockSpec((CHUNK, d_ssd), tok),
                   pl.BlockSpec((1, SSD_GROUPS, SSD_STATE, gw), lambda b, c: (b, 0, 0, 0))],
        out_shape=[jax.ShapeDtypeStruct((bsz * seq, d_ssd), BF16),
                   jax.ShapeDtypeStruct((bsz, SSD_GROUPS, SSD_STATE, gw), F32)],
        scratch_shapes=[pltpu.VMEM((CHUNK + SUBLANES, cd), F32),
                        pltpu.VMEM((SSD_GROUPS, SSD_STATE, gw), F32),
                        pltpu.VMEM((CHUNK, d_ssd), F32)],
        compiler_params=_cparams("arbitrary", "arbitrary"),
        name="ssd_chunk",
    )(xbc, z, gates, p["conv_ssd_w"], p["conv_ssd_b"], p["dt_bias_row"], p["a_log_row"],
      p["d_skip_row"], p["ssd_norm_g"], p["tril"], p["ssd_expand"])


def _mlstm_chunk_kernel(u_ref, v_ref, o_ref, g_ref, cw_ref, cb_ref, wq_ref, wk_ref, bi_ref, bf_ref, ng_ref,
                        tril_ref, y_ref, c_out_ref, n_out_ref, m_out_ref, upad, c_st, n_st, m_st,
                        *, nc, n_heads, dk):
    c = pl.program_id(1)
    dm = upad.shape[1]

    @pl.when(c == 0)
    def _():
        upad[0:SUBLANES, :] = jnp.zeros((SUBLANES, dm), F32)
        c_st[...] = jnp.zeros_like(c_st)
        n_st[...] = jnp.zeros_like(n_st)
        m_st[...] = jnp.zeros_like(m_st)

    upad[SUBLANES:SUBLANES + CHUNK, :] = u_ref[...]
    conv = cb_ref[...]
    for j in range(CONV_K):
        r0 = SUBLANES - (CONV_K - 1) + j
        conv = conv + cw_ref[j:j + 1, :] * upad[r0:r0 + CHUNK, :]
    upad[0:SUBLANES, :] = upad[CHUNK:CHUNK + SUBLANES, :]
    uc = _silu(conv).astype(BF16)

    logi = g_ref[:, 0:LANES] + bi_ref[...]
    logf = _log_sigmoid(g_ref[:, LANES:2 * LANES] + bf_ref[...])
    bcs = _dot_sel_l(tril_ref[...], logf)
    m_prev = m_st[0:1, :]
    b_end = bcs[CHUNK - 1:CHUNK, :]
    g_all = b_end - bcs + logi
    m_new = jnp.maximum(b_end + m_prev, jnp.max(g_all, axis=0, keepdims=True))
    w_old = jnp.exp(b_end + m_prev - m_new)
    w_s_all = jnp.exp(g_all - m_new)
    inter_all = bcs + m_prev
    bcs_t = bcs.T
    logi_t = logi.T

    row = lax.broadcasted_iota(jnp.int32, (CHUNK, CHUNK), 0)
    col = lax.broadcasted_iota(jnp.int32, (CHUNK, CHUNK), 1)
    causal = row >= col
    scale = dk ** -0.5

    hs = range(n_heads)
    sls = [slice(h * dk, (h + 1) * dk) for h in hs]
    gls = [GATE_I0 + h for h in hs]
    q = [_dot(uc[:, sls[h]], wq_ref[h]) for h in hs]
    k = [_dot(uc[:, sls[h]], wk_ref[h]) * scale for h in hs]
    qb = [x.astype(BF16) for x in q]
    kb = [x.astype(BF16) for x in k]
    vb = [v_ref[:, sls[h]].astype(BF16) for h in hs]
    dmat = [jnp.where(causal, bcs[:, g:g + 1] - bcs_t[g:g + 1, :] + logi_t[g:g + 1, :], -jnp.inf) for g in gls]
    inter = [inter_all[:, g:g + 1] for g in gls]
    m_t = [jnp.maximum(inter[h], jnp.max(dmat[h], axis=1, keepdims=True)) for h in hs]
    w_inter = [jnp.exp(inter[h] - m_t[h]) for h in hs]
    qk = [_dot_nt(qb[h], kb[h]) for h in hs]
    c_old = [c_st[h] for h in hs]
    n_old = [n_st[h][0:1, :] for h in hs]
    qc = [_dot(qb[h], c_old[h].astype(BF16)) for h in hs]
    att = [qk[h] * jnp.exp(dmat[h] - m_t[h]) for h in hs]
    av = [_dot(att[h].astype(BF16), vb[h]) for h in hs]
    kw = [k[h] * w_s_all[:, gls[h]:gls[h] + 1] for h in hs]
    kv = [_dot(kw[h].T.astype(BF16), vb[h]) for h in hs]
    num = [av[h] + w_inter[h] * qc[h] for h in hs]
    den = [jnp.sum(att[h], axis=1, keepdims=True) + w_inter[h] * jnp.sum(q[h] * n_old[h], axis=1, keepdims=True)
           for h in hs]
    hout = [num[h] / jnp.maximum(jnp.abs(den[h]), jnp.exp(-m_t[h])) for h in hs]
    mu = [jnp.mean(x, axis=-1, keepdims=True) for x in hout]
    cen = [hout[h] - mu[h] for h in hs]
    var = [jnp.mean(jnp.square(x), axis=-1, keepdims=True) for x in cen]
    for h in hs:
        sl = sls[h]
        hn = cen[h] * lax.rsqrt(var[h] + LN_EPS) * ng_ref[:, sl]
        y_ref[:, sl] = (_sigmoid(o_ref[:, sl]) * hn).astype(y_ref.dtype)
        wo = w_old[:, gls[h]:gls[h] + 1]
        c_st[h] = wo * c_old[h] + kv[h]
        n_st[h] = jnp.broadcast_to(wo * n_old[h] + jnp.sum(kw[h], axis=0, keepdims=True), (SUBLANES, dk))

    m_st[...] = jnp.broadcast_to(m_new, (SUBLANES, LANES))

    @pl.when(c == nc - 1)
    def _():
        c_out_ref[0] = c_st[...]
        n_out_ref[0] = n_st[...]
        m_out_ref[0] = m_st[...]


def _mlstm_chunk(u, v, o, gates, p, bsz, seq):
    nc = seq // CHUNK
    dm = u.shape[1]
    n_heads, dk, _ = p["w_mq"].shape
    tok = lambda b, c: (b * nc + c, 0)
    cst = lambda b, c: (0, 0)
    cst3 = lambda b, c: (0, 0, 0)
    return pl.pallas_call(
        functools.partial(_mlstm_chunk_kernel, nc=nc, n_heads=n_heads, dk=dk),
        grid=(bsz, nc),
        in_specs=[pl.BlockSpec((CHUNK, dm), tok),
                  pl.BlockSpec((CHUNK, dm), tok),
                  pl.BlockSpec((CHUNK, dm), tok),
                  pl.BlockSpec((CHUNK, 2 * LANES), tok),
                  pl.BlockSpec((CONV_K, dm), cst),
                  pl.BlockSpec((1, dm), cst),
                  pl.BlockSpec((n_heads, dk, dk), cst3),
                  pl.BlockSpec((n_heads, dk, dk), cst3),
                  pl.BlockSpec((1, LANES), cst),
                  pl.BlockSpec((1, LANES), cst),
                  pl.BlockSpec((1, dm), cst),
                  pl.BlockSpec((CHUNK, CHUNK), cst)],
        out_specs=[pl.BlockSpec((CHUNK, dm), tok),
                   pl.BlockSpec((1, n_heads, dk, dk), lambda b, c: (b, 0, 0, 0)),
                   pl.BlockSpec((1, n_heads, SUBLANES, dk), lambda b, c: (b, 0, 0, 0)),
                   pl.BlockSpec((1, SUBLANES, LANES), lambda b, c: (b, 0, 0))],
        out_shape=[jax.ShapeDtypeStruct((bsz * seq, dm), BF16),
                   jax.ShapeDtypeStruct((bsz, n_heads, dk, dk), F32),
                   jax.ShapeDtypeStruct((bsz, n_heads, SUBLANES, dk), F32),
                   jax.ShapeDtypeStruct((bsz, SUBLANES, LANES), F32)],
        scratch_shapes=[pltpu.VMEM((CHUNK + SUBLANES, dm), F32),
                        pltpu.VMEM((n_heads, dk, dk), F32),
                        pltpu.VMEM((n_heads, SUBLANES, dk), F32),
                        pltpu.VMEM((SUBLANES, LANES), F32)],
        compiler_params=_cparams("arbitrary", "arbitrary"),
        name="mlstm_chunk",
    )(u, v, o, gates, p["conv_m_w"], p["conv_m_b"], p["w_mq"], p["w_mk"], p["b_i_row"], p["b_f_row"],
      p["mlstm_norm_g"], p["tril"])


def _col_bcast(row, n_rows):
    return jnp.broadcast_to(row, (LANES, row.shape[1])).T[0:n_rows, :]


def _per_sequence(one_sequence, n_seq_in, n_seq_out):
    def kernel(*refs, **static):
        seq_in, shared, seq_out = refs[:n_seq_in], refs[n_seq_in:len(refs) - n_seq_out], refs[len(refs) - n_seq_out:]
        for s in range(seq_in[0].shape[0]):
            one = slice(s, s + 1)
            one_sequence(*(r.at[one] for r in seq_in), *shared, *(r.at[one] for r in seq_out), **static)
    return kernel


def _ssd_step_one(xbc_ref, z_ref, g_ref, buf_ref, st_ref, cw_ref, cb_ref, dtb_ref, alog_ref, dsk_ref, ng_ref,
                  exp_ref, y_ref, st_out_ref, buf_out_ref, *, d_ssd, n_heads):
    gw = d_ssd // SSD_GROUPS
    x_row = xbc_ref[0]
    buf_out_ref[0, 0:CONV_K - 2, :] = buf_ref[0, 1:CONV_K - 1, :]
    buf_out_ref[0, CONV_K - 2:CONV_K - 1, :] = x_row
    conv = cb_ref[...] + cw_ref[CONV_K - 1:CONV_K, :] * x_row
    for j in range(CONV_K - 1):
        conv = conv + cw_ref[j:j + 1, :] * buf_ref[0, j:j + 1, :]
    xc = _silu(conv)
    xs = xc[:, 0:d_ssd]

    lane = lax.broadcasted_iota(jnp.int32, (1, LANES), 1)
    a_row = jnp.where(lane < n_heads, -jnp.exp(alog_ref[...]), 0.0)
    dt = _softplus(g_ref[0][:, 0:LANES] + dtb_ref[...])
    dec = jnp.exp(dt * a_row)
    both = _dot_sel_r(jnp.concatenate([jnp.broadcast_to(dt, (SUBLANES, LANES)),
                                       jnp.broadcast_to(dec, (SUBLANES, LANES))], axis=0), exp_ref[...])
    dt_full = both[0:1]
    dec_full = both[SUBLANES:SUBLANES + 1]
    dtx = dt_full * xs
    dtx_col = _col_bcast(dtx, d_ssd)
    dec_col = _col_bcast(dec_full, d_ssd)

    y_parts = []
    for g in range(SSD_GROUPS):
        b_g = xc[:, d_ssd + g * SSD_STATE:d_ssd + (g + 1) * SSD_STATE]
        c_g = xc[:, d_ssd + (SSD_GROUPS + g) * SSD_STATE:d_ssd + (SSD_GROUPS + g + 1) * SSD_STATE]
        h_g = st_ref[0, g * gw:(g + 1) * gw, :]
        cb = jnp.sum(c_g * b_g, axis=-1, keepdims=True)
        c8 = jnp.broadcast_to(c_g, (SUBLANES, SSD_STATE)).astype(BF16)
        y_int = _dot_nt(c8, h_g.astype(BF16))[0:1]
        sl = slice(g * gw, (g + 1) * gw)
        y_parts.append(cb * dtx[:, sl] + dec_full[:, sl] * y_int)
        st_out_ref[0, sl, :] = h_g * dec_col[sl, :] + dtx_col[sl, :] * b_g
    y = jnp.concatenate(y_parts, axis=1) + dsk_ref[...] * xs
    y = y * _silu(z_ref[0])
    y = y * lax.rsqrt(jnp.mean(y * y, axis=-1, keepdims=True) + LN_EPS) * ng_ref[...]
    y_ref[0] = y.astype(y_ref.dtype)


def _ssd_step(xbc, z, gates, buf, state, p):
    nb, cd = xbc.shape
    d_ssd = z.shape[1]
    n_heads = d_ssd // SSD_HEAD_DIM
    row3 = lambda b: (b, 0, 0)
    cst = lambda b: (0, 0)
    st2 = state.reshape(nb, d_ssd, SSD_STATE)
    sb = STEP_SEQS
    assert nb % sb == 0
    y, st_new, buf_new = pl.pallas_call(
        functools.partial(_per_sequence(_ssd_step_one, 5, 3), d_ssd=d_ssd, n_heads=n_heads),
        grid=(nb // sb,),
        in_specs=[pl.BlockSpec((sb, 1, cd), row3),
                  pl.BlockSpec((sb, 1, d_ssd), row3),
                  pl.BlockSpec((sb, 1, 2 * LANES), row3),
                  pl.BlockSpec((sb, CONV_K - 1, cd), row3),
                  pl.BlockSpec((sb, d_ssd, SSD_STATE), row3),
                  pl.BlockSpec((CONV_K, cd), cst),
                  pl.BlockSpec((1, cd), cst),
                  pl.BlockSpec((1, LANES), cst),
                  pl.BlockSpec((1, LANES), cst),
                  pl.BlockSpec((1, d_ssd), cst),
                  pl.BlockSpec((1, d_ssd), cst),
                  pl.BlockSpec((LANES, d_ssd), cst)],
        out_specs=[pl.BlockSpec((sb, 1, d_ssd), row3),
                   pl.BlockSpec((sb, d_ssd, SSD_STATE), row3),
                   pl.BlockSpec((sb, CONV_K - 1, cd), row3)],
        out_shape=[jax.ShapeDtypeStruct((nb, 1, d_ssd), BF16),
                   jax.ShapeDtypeStruct((nb, d_ssd, SSD_STATE), F32),
                   jax.ShapeDtypeStruct((nb, CONV_K - 1, cd), F32)],
        compiler_params=_cparams("arbitrary"),
        name="ssd_step",
    )(xbc.reshape(nb, 1, cd), z.reshape(nb, 1, d_ssd), gates.reshape(nb, 1, 2 * LANES), buf, st2,
      p["conv_ssd_w"], p["conv_ssd_b"], p["dt_bias_row"], p["a_log_row"], p["d_skip_row"], p["ssd_norm_g"],
      p["ssd_expand"])
    return y.reshape(nb, d_ssd), st_new.reshape(state.shape), buf_new


def _mlstm_step_one(u_ref, v_ref, o_ref, g_ref, buf_ref, c_ref, n_ref, m_ref, cw_ref, cb_ref, wq_ref, wk_ref,
                    bi_ref, bf_ref, ng_ref, hexp_ref, y_ref, c_out_ref, n_out_ref, m_out_ref, buf_out_ref,
                    *, n_heads, dk):
    buf_out_ref[0, 0:CONV_K - 2, :] = buf_ref[0, 1:CONV_K - 1, :]
    buf_out_ref[0, CONV_K - 2:CONV_K - 1, :] = u_ref[0]
    conv = cb_ref[...] + cw_ref[CONV_K - 1:CONV_K, :] * u_ref[0]
    for j in range(CONV_K - 1):
        conv = conv + cw_ref[j:j + 1, :] * buf_ref[0, j:j + 1, :]
    uc = jnp.broadcast_to(_silu(conv), (SUBLANES, conv.shape[1])).astype(BF16)

    gt = g_ref[0]
    logi = gt[:, 0:LANES] + bi_ref[...]
    logf = _log_sigmoid(gt[:, LANES:2 * LANES] + bf_ref[...])
    m_prev = m_ref[0]
    inter = logf + m_prev
    m_t = jnp.maximum(inter, logi)
    w_inter = jnp.exp(inter - m_t)
    w_in = jnp.exp(logi - m_t)
    floor = jnp.exp(-m_t)
    rows = jnp.concatenate([w_inter, w_in, floor, jnp.zeros((SUBLANES - 3, LANES), F32)], axis=0)
    full = _dot_sel_r(rows, hexp_ref[...])
    scale = dk ** -0.5

    for h in range(n_heads):
        sl = slice(h * dk, (h + 1) * dk)
        q = _dot(uc[:, sl], wq_ref[h])[0:1]
        k = _dot(uc[:, sl], wk_ref[h])[0:1] * scale
        v_h = v_ref[0][:, sl]
        wi = full[0:1, sl]
        ws = full[1:2, sl]
        fl = full[2:3, sl]
        c_h = c_ref[0, h]
        n_h = n_ref[0, h:h + 1, :]
        att = jnp.sum(q * k, axis=-1, keepdims=True) * ws
        q8 = jnp.broadcast_to(q, (SUBLANES, dk)).astype(BF16)
        num = att * v_h + wi * _dot(q8, c_h.astype(BF16))[0:1]
        den = att + wi * jnp.sum(q * n_h, axis=-1, keepdims=True)
        hout = num / jnp.maximum(jnp.abs(den), fl)
        mu = jnp.mean(hout, axis=-1, keepdims=True)
        var = jnp.mean(jnp.square(hout - mu), axis=-1, keepdims=True)
        hn = (hout - mu) * lax.rsqrt(var + LN_EPS) * ng_ref[:, sl]
        y_ref[0, :, sl] = (_sigmoid(o_ref[0][:, sl]) * hn).astype(y_ref.dtype)
        kw = k * ws
        kw_col = _col_bcast(kw, dk)
        wi_full = jnp.broadcast_to(wi, (dk, dk))
        for half in range(dk // LANES):
            hs = slice(half * LANES, (half + 1) * LANES)
            c_out_ref[0, h, :, hs] = wi_full[:, hs] * c_h[:, hs] + kw_col * v_h[:, hs]
        n_out_ref[0, h:h + 1, :] = wi * n_h + kw
    m_out_ref[0] = m_t


def _mlstm_step(u, v, o, gates, buf, c_st, n_st, m_row, p):
    nb, dm = u.shape
    n_heads, dk, _ = p["w_mq"].shape
    row3 = lambda b: (b, 0, 0)
    cst = lambda b: (0, 0)
    cst3 = lambda b: (0, 0, 0)
    r3 = lambda a: a.reshape(nb, 1, a.shape[-1])
    sb = STEP_SEQS
    assert nb % sb == 0
    y, c_new, n_new, m_new, buf_new = pl.pallas_call(
        functools.partial(_per_sequence(_mlstm_step_one, 8, 5), n_heads=n_heads, dk=dk),
        grid=(nb // sb,),
        in_specs=[pl.BlockSpec((sb, 1, dm), row3),
                  pl.BlockSpec((sb, 1, dm), row3),
                  pl.BlockSpec((sb, 1, dm), row3),
                  pl.BlockSpec((sb, 1, 2 * LANES), row3),
                  pl.BlockSpec((sb, CONV_K - 1, dm), row3),
                  pl.BlockSpec((sb, n_heads, dk, dk), lambda b: (b, 0, 0, 0)),
                  pl.BlockSpec((sb, n_heads, dk), row3),
                  pl.BlockSpec((sb, 1, LANES), row3),
                  pl.BlockSpec((CONV_K, dm), cst),
                  pl.BlockSpec((1, dm), cst),
                  pl.BlockSpec((n_heads, dk, dk), cst3),
                  pl.BlockSpec((n_heads, dk, dk), cst3),
                  pl.BlockSpec((1, LANES), cst),
                  pl.BlockSpec((1, LANES), cst),
                  pl.BlockSpec((1, dm), cst),
                  pl.BlockSpec((LANES, dm), cst)],
        out_specs=[pl.BlockSpec((sb, 1, dm), row3),
                   pl.BlockSpec((sb, n_heads, dk, dk), lambda b: (b, 0, 0, 0)),
                   pl.BlockSpec((sb, n_heads, dk), row3),
                   pl.BlockSpec((sb, 1, LANES), row3),
                   pl.BlockSpec((sb, CONV_K - 1, dm), row3)],
        out_shape=[jax.ShapeDtypeStruct((nb, 1, dm), BF16),
                   jax.ShapeDtypeStruct(c_st.shape, F32),
                   jax.ShapeDtypeStruct(n_st.shape, F32),
                   jax.ShapeDtypeStruct((nb, 1, LANES), F32),
                   jax.ShapeDtypeStruct((nb, CONV_K - 1, dm), F32)],
        compiler_params=_cparams("arbitrary"),
        name="mlstm_step",
    )(r3(u), r3(v), r3(o), r3(gates), buf, c_st, n_st, m_row, p["conv_m_w"], p["conv_m_b"], p["w_mq"],
      p["w_mk"], p["b_i_row"], p["b_f_row"], p["mlstm_norm_g"], p["mlstm_expand"])
    return y.reshape(nb, dm), c_new, n_new, m_new, buf_new


def _outproj_kernel(ys_ref, hm_ref, x_ref, w_ref, g_ref, b_ref, x1t_ref, *, alpha):
    d_ssd = ys_ref.shape[1]
    tm = ys_ref.shape[0]
    subs = [slice(r0, min(r0 + OUTPROJ_ROWS, tm)) for r0 in range(0, tm, OUTPROJ_ROWS)]
    mix = [_dot(ys_ref[rs, :], w_ref[0:d_ssd, :]) + _dot(hm_ref[rs, :], w_ref[d_ssd:, :]) for rs in subs]
    for rs, mx in zip(subs, mix):
        r = alpha * x_ref[rs, :] + mx
        mu = jnp.mean(r, axis=-1, keepdims=True)
        var = jnp.mean(jnp.square(r - mu), axis=-1, keepdims=True)
        x1 = (r - mu) * lax.rsqrt(var + LN_EPS) * g_ref[...] + b_ref[...]
        x1t_ref[:, rs] = x1.T


def _outproj(ys, hm, x2, p, alpha, tm):
    m, d = x2.shape
    d_ssd, d_m = ys.shape[1], hm.shape[1]
    return pl.pallas_call(
        functools.partial(_outproj_kernel, alpha=alpha),
        grid=(m // tm,),
        in_specs=[pl.BlockSpec((tm, d_ssd), lambda i: (i, 0)),
                  pl.BlockSpec((tm, d_m), lambda i: (i, 0)),
                  pl.BlockSpec((tm, d), lambda i: (i, 0)),
                  pl.BlockSpec((d_ssd + d_m, d), lambda i: (0, 0)),
                  pl.BlockSpec((1, d), lambda i: (0, 0)),
                  pl.BlockSpec((1, d), lambda i: (0, 0))],
        out_specs=pl.BlockSpec((d, tm), lambda i: (0, i)),
        out_shape=jax.ShapeDtypeStruct((d, m), F32),
        compiler_params=_cparams("arbitrary"),
        name="outproj_ln1",
    )(ys, hm, x2, p["w_out"], p["ln1_g"], p["ln1_b"])


def _oddeven_merge_sort_pairs(n):
    pairs = []
    p = 1
    while p < n:
        k = p
        while k >= 1:
            for j in range(k % p, n - k, 2 * k):
                for i in range(min(k, n - j - k)):
                    if (i + j) // (2 * p) == (i + j + k) // (2 * p):
                        pairs.append((i + j, i + j + k))
            k //= 2
        p *= 2
    return pairs


_SORT_TOPK = _oddeven_merge_sort_pairs(PEER_TOPK)


def _compare_exchange(lst, i, j):
    hi, lo = jnp.maximum(lst[i], lst[j]), jnp.minimum(lst[i], lst[j])
    lst[i], lst[j] = hi, lo


def _bitonic_to_sorted(t):
    d = PEER_TOPK // 2
    while d >= 1:
        for i in range(PEER_TOPK):
            if i & d == 0:
                _compare_exchange(t, i, i + d)
        d //= 2
    return t


def _merge_top(x, y):
    neg = jnp.full_like(x[0], -jnp.inf)
    x = x + [neg] * (PEER_TOPK - len(x))
    y = y + [neg] * (PEER_TOPK - len(y))
    return _bitonic_to_sorted([jnp.maximum(x[k], y[PEER_TOPK - 1 - k]) for k in range(PEER_TOPK)])


def _top_rows(s):
    lst = [s[SUBLANES * g:SUBLANES * (g + 1), :] for g in range(PEER_TOPK)]
    for i, j in _SORT_TOPK:
        _compare_exchange(lst, i, j)
    shift = SUBLANES // 2
    while shift >= 1:
        lst = _merge_top(lst, [pltpu.roll(a, shift, 0) for a in lst])
        shift //= 2
    return lst


def _peer_topk_kernel(x1t_ref, wqt_ref, keys_ref, s1_ref, s2_ref, st_ref, b_scr, q_scr, a_scr, *, n_heads):
    tq = x1t_ref.shape[1]
    q_scr[...] = _dot(wqt_ref[...], x1t_ref[...].astype(BF16))
    half = q_scr.shape[0] // (2 * n_heads)

    for h in range(n_heads):
        for side, (s_ref, top_scr) in enumerate(((s1_ref, a_scr), (s2_ref, b_scr))):
            r0 = (2 * h + side) * half
            s = _dot(keys_ref[h, side], q_scr[r0:r0 + half, :].astype(BF16))
            for t in range(tq // LANES):
                ls = slice(t * LANES, (t + 1) * LANES)
                s_ref[h, t] = s[:, ls]
                top = _top_rows(s[:, ls])
                for k in range(PEER_TOPK):
                    top_scr[k, h:h + 1, ls] = top[k][0:1, :]

    for t in range(tq // LANES):
        ls = slice(t * LANES, (t + 1) * LANES)
        a = [a_scr[k, :, ls] for k in range(PEER_TOPK)]
        b = [b_scr[k, :, ls] for k in range(PEER_TOPK)]
        single = PEER_TOPK // 2
        lists = [[a[i] + b[j] for j in range(PEER_TOPK // (i + 1))] for i in range(single)]
        lists.append([a[i] + b[0] for i in range(single, PEER_TOPK)])
        merged = lists[0]
        for other in lists[1:]:
            merged = _merge_top(merged, other)
        tau = merged[PEER_TOPK - 1]
        top = a[0] + b[0]
        zsum = jnp.zeros_like(tau)
        for cand in (c for lst in lists for c in lst):
            zsum = zsum + jnp.where(cand >= tau, jnp.exp(cand - top), 0.0)
        st_ref[0, :, ls] = tau
        st_ref[1, :, ls] = a[0]
        st_ref[2, :, ls] = b[0]
        st_ref[3, :, ls] = 1.0 / zsum


def _peer_topk(x1t, p, tq):
    d, m = x1t.shape
    n_heads, _, n_keys, half = p["peer_keys"].shape
    nq = p["peer_wqT"].shape[0]
    assert n_heads == SUBLANES and n_keys == PEER_TOPK * SUBLANES
    keyb = lambda i: (0, i, 0, 0)
    return pl.pallas_call(
        functools.partial(_peer_topk_kernel, n_heads=n_heads),
        grid=(m // tq,),
        in_specs=[pl.BlockSpec((d, tq), lambda i: (0, i)),
                  pl.BlockSpec((nq, d), lambda i: (0, 0)),
                  pl.BlockSpec((n_heads, 2, n_keys, half), lambda i: (0, 0, 0, 0))],
        out_specs=[pl.BlockSpec((n_heads, tq // LANES, n_keys, LANES), keyb),
                   pl.BlockSpec((n_heads, tq // LANES, n_keys, LANES), keyb),
                   pl.BlockSpec((N_STATS, SUBLANES, tq), lambda i: (0, 0, i)),
                   pl.BlockSpec((PEER_TOPK, SUBLANES, tq), lambda i: (0, 0, i))],
        out_shape=[jax.ShapeDtypeStruct((n_heads, m // LANES, n_keys, LANES), F32),
                   jax.ShapeDtypeStruct((n_heads, m // LANES, n_keys, LANES), F32),
                   jax.ShapeDtypeStruct((N_STATS, SUBLANES, m), F32),
                   jax.ShapeDtypeStruct((PEER_TOPK, SUBLANES, m), F32)],
        scratch_shapes=[pltpu.VMEM((nq, tq), F32),
                        pltpu.VMEM((PEER_TOPK, SUBLANES, tq), F32)],
        compiler_params=_cparams("arbitrary"),
        name="peer_topk",
    )(x1t, p["peer_wqT"], p["peer_keys"])


def _peer_dense_kernel(x1t_ref, s1_ref, s2_ref, st_ref, bl_ref, u_ref, vt_ref, g_ref, b_ref, out_ref,
                       xb_scr, e2_scr, c_scr, thr_scr, acc_scr, s_scr, w_scr, thr_r, cr, *, alpha, n_heads):
    j = pl.program_id(1)
    ec = u_ref.shape[0]
    tb = x1t_ref.shape[1]
    nt = tb // LANES
    n_keys = s1_ref.shape[2]
    na = ec // n_keys

    @pl.when(j == 0)
    def _():
        xb_scr[...] = x1t_ref[...].astype(BF16)
        for h in range(n_heads):
            for t in range(nt):
                ls = slice(t * LANES, (t + 1) * LANES)
                e2_scr[h, t] = jnp.exp(s2_ref[h, t] - st_ref[2, h:h + 1, ls])
                s1 = s1_ref[h, t]
                c_scr[h, t] = jnp.exp(s1 - st_ref[1, h:h + 1, ls]) * (0.5 * st_ref[3, h:h + 1, ls])
                tau = st_ref[0, h:h + 1, ls]
                thr = jnp.full((n_keys, LANES), jnp.inf, F32)
                for k in range(PEER_TOPK):
                    bk = bl_ref[k, h:h + 1, ls]
                    thr = jnp.where(s1 + bk >= tau, bk, thr)
                thr_scr[h, t] = thr
        acc_scr[...] = jnp.zeros_like(acc_scr)

    s_scr[:, 0:tb] = _dot(u_ref[...], xb_scr[...])

    g0 = pl.multiple_of(j * na, na)
    for h in range(n_heads):
        for t in range(nt):
            ls = slice(t * LANES, (t + 1) * LANES)
            thr_blk = thr_scr[h, t, pl.ds(g0, na), :]
            c_blk = c_scr[h, t, pl.ds(g0, na), :]
            for al in range(na):
                thr_r[h * na + al, :, ls] = thr_blk[al:al + 1, :]
                cr[h * na + al, :, ls] = c_blk[al:al + 1, :]

    def first_key(al, carry):
        r0 = pl.multiple_of(al * n_keys, n_keys)
        for t in range(nt):
            ls = slice(t * LANES, (t + 1) * LANES)
            gate = jnp.zeros((n_keys, LANES), F32)
            for h in range(n_heads):
                gate = gate + jnp.where(s2_ref[h, t] >= thr_r[h * na + al][:, ls],
                                        e2_scr[h, t] * cr[h * na + al][:, ls], 0.0)
            x = s_scr[pl.ds(r0, n_keys), ls]
            hid = x + x * lax.erf(x * (1.0 / math.sqrt(2.0)))
            w_scr[pl.ds(r0, n_keys), ls] = (gate * hid).astype(BF16)
        return carry

    lax.fori_loop(0, na, first_key, 0)
    acc_scr[...] += _dot(vt_ref[...], w_scr[:, 0:tb])

    @pl.when(j == pl.num_programs(1) - 1)
    def _():
        r = alpha * x1t_ref[...] + acc_scr[...]
        mu = jnp.mean(r, axis=0, keepdims=True)
        var = jnp.mean(jnp.square(r - mu), axis=0, keepdims=True)
        y = (r - mu) * lax.rsqrt(var + LN_EPS) * g_ref[...] + b_ref[...]
        out_ref[...] = y.T


def _peer_dense(x1t, s1, s2, st, blist, p, alpha, tb):
    d, m = x1t.shape
    n_heads, _, n_keys, _ = s1.shape
    n_exp = p["peer_u"].shape[0]
    ec = PEER_CHUNK
    nj = n_exp // ec
    na = ec // n_keys
    assert na % SUBLANES == 0
    nt = tb // LANES
    keyb = lambda i, j: (0, i, 0, 0)
    tokb = lambda i, j: (0, 0, i)
    pitch = tb + LANES
    return pl.pallas_call(
        functools.partial(_peer_dense_kernel, alpha=alpha, n_heads=n_heads),
        grid=(m // tb, nj),
        in_specs=[pl.BlockSpec((d, tb), lambda i, j: (0, i)),
                  pl.BlockSpec((n_heads, nt, n_keys, LANES), keyb),
                  pl.BlockSpec((n_heads, nt, n_keys, LANES), keyb),
                  pl.BlockSpec((N_STATS, SUBLANES, tb), tokb),
                  pl.BlockSpec((PEER_TOPK, SUBLANES, tb), tokb),
                  pl.BlockSpec((ec, d), lambda i, j: (j, 0)),
                  pl.BlockSpec((d, ec), lambda i, j: (0, j)),
                  pl.BlockSpec((d, 1), lambda i, j: (0, 0)),
                  pl.BlockSpec((d, 1), lambda i, j: (0, 0))],
        out_specs=pl.BlockSpec((tb, d), lambda i, j: (i, 0)),
        out_shape=jax.ShapeDtypeStruct((m, d), F32),
        scratch_shapes=[pltpu.VMEM((d, tb), BF16),
                        pltpu.VMEM((n_heads, nt, n_keys, LANES), F32),
                        pltpu.VMEM((n_heads, nt, n_keys, LANES), F32),
                        pltpu.VMEM((n_heads, nt, n_keys, LANES), F32),
                        pltpu.VMEM((d, tb), F32),
                        pltpu.VMEM((ec, pitch), F32),
                        pltpu.VMEM((ec, pitch), BF16),
                        pltpu.VMEM((n_heads * na, 1, tb), F32),
                        pltpu.VMEM((n_heads * na, 1, tb), F32)],
        compiler_params=_cparams("arbitrary", "arbitrary"),
        name="peer_dense",
    )(x1t, s1, s2, st, blist, p["peer_u"], p["peer_vT"], p["ln2_g_col"], p["ln2_b_col"])


def _pad_row(vec, offset):
    return jnp.zeros((1, LANES), F32).at[0, offset:offset + vec.shape[0]].set(vec.astype(F32))


def _prep_layer(l, w_in, conv_ssd_w, conv_ssd_b, dt_bias, a_log, d_skip, ssd_norm_g, conv_m_w, conv_m_b,
                w_mq, w_mk, b_igate, b_fgate, mlstm_norm_g, w_out, ln1_g, ln1_b, peer_wq, peer_keys,
                peer_u, peer_v, ln2_g, ln2_b):
    d_model = w_in.shape[1]
    d_ssd = ssd_norm_g.shape[1]
    cd = conv_ssd_w.shape[2]
    n_sh = dt_bias.shape[1]
    d_m = conv_m_w.shape[2]
    n_mh = w_mq.shape[1]
    dk = w_mq.shape[2]
    assert n_sh <= GATE_I0 and GATE_I0 + n_mh <= LANES and d_ssd == n_sh * SSD_HEAD_DIM
    w = w_in[l]
    offs = [0]
    for n in (d_ssd, cd, n_sh, d_m, d_m, d_m, n_mh, n_mh):
        offs.append(offs[-1] + n)
    z_w, xbc_w, dt_w, u_w, v_w, o_w, i_w, f_w = (w[:, offs[k]:offs[k + 1]] for k in range(8))
    gate_a = jnp.zeros((d_model, LANES), F32).at[:, 0:n_sh].set(dt_w).at[:, GATE_I0:GATE_I0 + n_mh].set(i_w)
    gate_b = jnp.zeros((d_model, LANES), F32).at[:, GATE_I0:GATE_I0 + n_mh].set(f_w)
    w_cat = jnp.concatenate([z_w, xbc_w, u_w, v_w, o_w, gate_a, gate_b], axis=1).astype(BF16)
    widths = (d_ssd, cd, d_m, d_m, d_m, 2 * LANES)

    lane = jnp.arange(LANES)[:, None]
    ssd_expand = (lane == (jnp.arange(d_ssd)[None, :] // SSD_HEAD_DIM)).astype(BF16)
    mlstm_expand = (lane == (GATE_I0 + jnp.arange(d_m)[None, :] // dk)).astype(BF16)
    t = jnp.arange(CHUNK)
    tril = (t[:, None] >= t[None, :]).astype(BF16)
    p = {
        "w_cat": w_cat,
        "conv_ssd_w": conv_ssd_w[l], "conv_ssd_b": conv_ssd_b[l][None, :],
        "dt_bias_row": _pad_row(dt_bias[l], 0), "a_log_row": _pad_row(a_log[l], 0),
        "d_skip_row": jnp.repeat(d_skip[l], SSD_HEAD_DIM)[None, :], "ssd_norm_g": ssd_norm_g[l][None, :],
        "conv_m_w": conv_m_w[l], "conv_m_b": conv_m_b[l][None, :],
        "w_mq": w_mq[l].astype(BF16), "w_mk": w_mk[l].astype(BF16),
        "b_i_row": _pad_row(b_igate[l], GATE_I0), "b_f_row": _pad_row(b_fgate[l], GATE_I0),
        "mlstm_norm_g": mlstm_norm_g[l].reshape(1, d_m),
        "w_out": w_out[l].astype(BF16), "ln1_g": ln1_g[l][None, :], "ln1_b": ln1_b[l][None, :],
        "peer_wqT": peer_wq[l].T.astype(BF16), "peer_keys": peer_keys[l].astype(BF16),
        "peer_u": peer_u[l].astype(BF16),
        "peer_vT": peer_v[l].astype(BF16).T,
        "ln2_g_col": ln2_g[l][:, None], "ln2_b_col": ln2_b[l][:, None],
        "tril": tril, "ssd_expand": ssd_expand, "mlstm_expand": mlstm_expand,
    }
    return p, widths


def _tile_sizes(m):
    tm = 256 if m % 256 == 0 else LANES
    tb = 512 if m % 512 == 0 else LANES
    return tm, tb


def _peer_block(x1t, p, alpha):
    m = x1t.shape[1]
    tm, tb = _tile_sizes(m)
    s1, s2, st, blist = _peer_topk(x1t, p, tm)
    return _peer_dense(x1t, s1, s2, st, blist, p, alpha, tb)


def _layer_prompt(x, p, widths, alpha):
    bsz, seq, d = x.shape
    assert seq % CHUNK == 0 and seq >= CONV_K - 1
    x2 = x.reshape(bsz * seq, d)
    tm, _ = _tile_sizes(bsz * seq)
    z, xbc, u, v, o, gates = _inproj(x2, p["w_cat"], widths, tm)
    y_s, h_t = _ssd_chunk(xbc, z, gates, p, bsz, seq)
    h_m, c_new, n_new, m_new = _mlstm_chunk(u, v, o, gates, p, bsz, seq)
    x1t = _outproj(y_s, h_m, x2, p, alpha, 2 * tm if (bsz * seq) % (2 * tm) == 0 else tm)
    y = _peer_block(x1t, p, alpha).reshape(bsz, seq, d)
    n_mh = c_new.shape[1]
    d_ssd = z.shape[1]
    n_sh = d_ssd // SSD_HEAD_DIM
    hg = n_sh // SSD_GROUPS
    new_ssd = h_t.reshape(bsz, SSD_GROUPS, SSD_STATE, hg, SSD_HEAD_DIM).transpose(0, 1, 3, 4, 2)
    new_ssd = new_ssd.reshape(bsz, n_sh, SSD_HEAD_DIM, SSD_STATE)
    states = (new_ssd,
              xbc.reshape(bsz, seq, -1)[:, seq - (CONV_K - 1):, :],
              c_new,
              n_new[:, :, 0, :],
              m_new[:, 0, GATE_I0:GATE_I0 + n_mh],
              u.reshape(bsz, seq, -1)[:, seq - (CONV_K - 1):, :])
    return y, states


def _layer_sample(x, st, p, widths, alpha):
    st_ssd, st_ssd_conv, st_c, st_n, st_m, st_mconv = st
    nb, seq, d = x.shape
    assert seq == 1 and nb % LANES == 0
    x2 = x.reshape(nb, d)
    z, xbc, u, v, o, gates = _inproj(x2, p["w_cat"], widths, LANES)
    y_s, new_ssd, new_ssd_conv = _ssd_step(xbc, z, gates, st_ssd_conv, st_ssd, p)
    n_mh = st_c.shape[1]
    m_row = jnp.zeros((nb, 1, LANES), F32).at[:, 0, GATE_I0:GATE_I0 + n_mh].set(st_m)
    h_m, c_new, n_new, m_new, new_mconv = _mlstm_step(u, v, o, gates, st_mconv, st_c, st_n, m_row, p)
    x1t = _outproj(y_s, h_m, x2, p, alpha, LANES)
    y = _peer_block(x1t, p, alpha).reshape(nb, seq, d)
    states = (new_ssd, new_ssd_conv, c_new, n_new, m_new[:, 0, GATE_I0:GATE_I0 + n_mh], new_mconv)
    return y, states


def kernel(x_prompt, x_sample, state_ssd, state_ssd_conv, state_mlstm_c, state_mlstm_n, state_mlstm_m,
           state_mlstm_conv, w_in, conv_ssd_w, conv_ssd_b, dt_bias, a_log, d_skip, ssd_norm_g, conv_m_w,
           conv_m_b, w_mq, w_mk, b_igate, b_fgate, mlstm_norm_g, w_out, ln1_g, ln1_b, peer_wq, peer_keys,
           peer_u, peer_v, ln2_g, ln2_b):
    depth = w_in.shape[0]
    alpha = (2.0 * depth) ** 0.25
    states = (state_ssd, state_ssd_conv, state_mlstm_c, state_mlstm_n, state_mlstm_m, state_mlstm_conv)
    yp, ys = x_prompt, x_sample
    p_list, s_list = [], []
    for l in range(depth):
        p, widths = _prep_layer(l, w_in, conv_ssd_w, conv_ssd_b, dt_bias, a_log, d_skip, ssd_norm_g, conv_m_w,
                                conv_m_b, w_mq, w_mk, b_igate, b_fgate, mlstm_norm_g, w_out, ln1_g, ln1_b,
                                peer_wq, peer_keys, peer_u, peer_v, ln2_g, ln2_b)
        yp, p_new = _layer_prompt(yp, p, widths, alpha)
        ys, s_new = _layer_sample(ys, tuple(s[l] for s in states), p, widths, alpha)
        p_list.append(p_new)
        s_list.append(s_new)
    p_out = tuple(jnp.stack([pn[i] for pn in p_list]) for i in range(6))
    s_out = tuple(jnp.stack([sn[i] for sn in s_list]) for i in range(6))
    return (yp, ys) + p_out + s_out
```

```python
import functools
import math

import jax
import jax.numpy as jnp
from jax import lax
from jax.experimental import pallas as pl
from jax.experimental.pallas import tpu as pltpu

F32 = jnp.float32
BF16 = jnp.bfloat16

LANES = 128
SUBLANES = 8
CHUNK = 128
CONV_K = 4
LN_EPS = 1e-5
VMEM_LIMIT = 58 * 1024 * 1024

SSD_HEAD_DIM = 64
SSD_STATE = 128
SSD_GROUPS = 2
PEER_TOPK = 16
GATE_I0 = 16
N_STATS = 4
PEER_CHUNK = 2048
STEP_SEQS = 4
OUTPROJ_ROWS = 256


def _cparams(*sem):
    return pltpu.CompilerParams(dimension_semantics=sem, vmem_limit_bytes=VMEM_LIMIT)


def _dot(a, b):
    return jnp.dot(a, b, preferred_element_type=F32)


def _dot_nt(a, b):
    return lax.dot_general(a, b, (((1,), (1,)), ((), ())), preferred_element_type=F32)


def _split3(x):
    h = x.astype(BF16)
    r = x - h.astype(F32)
    m = r.astype(BF16)
    lo = (r - m.astype(F32)).astype(BF16)
    return h, m, lo


def _dot_sel_l(sel_bf16, x):
    h, m, lo = _split3(x)
    return _dot(sel_bf16, h) + _dot(sel_bf16, m) + _dot(sel_bf16, lo)


def _dot_sel_r(x, sel_bf16):
    h, m, lo = _split3(x)
    return _dot(h, sel_bf16) + _dot(m, sel_bf16) + _dot(lo, sel_bf16)


def _sigmoid(x):
    return 1.0 / (1.0 + jnp.exp(-x))


def _silu(x):
    return x * _sigmoid(x)


def _softplus(x):
    return jnp.maximum(x, 0.0) + jnp.log1p(jnp.exp(-jnp.abs(x)))


def _log_sigmoid(x):
    return -_softplus(-x)


def _gelu_exact(x):
    return 0.5 * x * (1.0 + lax.erf(x * (1.0 / math.sqrt(2.0))))


def _inproj_kernel(x_ref, w_ref, *out_refs):
    xb = x_ref[...].astype(BF16)
    off = 0
    for ref in out_refs:
        n = ref.shape[-1]
        ref[...] = _dot(xb, w_ref[:, off:off + n])
        off += n


def _inproj(x2, w_cat, widths, tm):
    m, d = x2.shape
    n_all = w_cat.shape[1]
    return pl.pallas_call(
        _inproj_kernel,
        grid=(m // tm,),
        in_specs=[pl.BlockSpec((tm, d), lambda i: (i, 0)),
                  pl.BlockSpec((d, n_all), lambda i: (0, 0))],
        out_specs=[pl.BlockSpec((tm, n), lambda i: (i, 0)) for n in widths],
        out_shape=[jax.ShapeDtypeStruct((m, n), F32) for n in widths],
        compiler_params=_cparams("arbitrary"),
        name="inproj",
    )(x2, w_cat)


def _ssd_chunk_kernel(xbc_ref, z_ref, g_ref, cw_ref, cb_ref, dtb_ref, alog_ref, dsk_ref, ng_ref,
                      tril_ref, exp_ref, y_ref, hT_out_ref, xpad, hT, ybuf, *, nc, d_ssd, n_heads):
    c = pl.program_id(1)
    cd = xpad.shape[1]
    hg = n_heads // SSD_GROUPS
    gw = d_ssd // SSD_GROUPS

    @pl.when(c == 0)
    def _():
        xpad[0:SUBLANES, :] = jnp.zeros((SUBLANES, cd), F32)
        hT[...] = jnp.zeros_like(hT)

    xpad[SUBLANES:SUBLANES + CHUNK, :] = xbc_ref[...]
    conv = cb_ref[...]
    for j in range(CONV_K):
        r0 = SUBLANES - (CONV_K - 1) + j
        conv = conv + cw_ref[j:j + 1, :] * xpad[r0:r0 + CHUNK, :]
    xpad[0:SUBLANES, :] = xpad[CHUNK:CHUNK + SUBLANES, :]
    xc = _silu(conv)
    xs = xc[:, 0:d_ssd]

    lane = lax.broadcasted_iota(jnp.int32, (1, LANES), 1)
    a_row = jnp.where(lane < n_heads, -jnp.exp(alog_ref[...]), 0.0)
    dt = _softplus(g_ref[:, 0:LANES] + dtb_ref[...])
    acs = _dot_sel_l(tril_ref[...], dt * a_row)
    dt_t = dt.T
    acs_t = acs.T
    eacs = jnp.exp(acs)
    dec = jnp.exp(acs[CHUNK - 1:CHUNK, :] - acs) * dt
    both = _dot_sel_r(jnp.concatenate([eacs, dec], axis=0), exp_ref[...])
    eacs_full = both[0:CHUNK]
    dec_full = both[CHUNK:2 * CHUNK]

    row = lax.broadcasted_iota(jnp.int32, (CHUNK, CHUNK), 0)
    col = lax.broadcasted_iota(jnp.int32, (CHUNK, CHUNK), 1)
    causal = row >= col

    gs = range(SSD_GROUPS)
    b_g = [xc[:, d_ssd + g * SSD_STATE:d_ssd + (g + 1) * SSD_STATE] for g in gs]
    c_g = [xc[:, d_ssd + (SSD_GROUPS + g) * SSD_STATE:d_ssd + (SSD_GROUPS + g + 1) * SSD_STATE] for g in gs]
    b_gb = [x.astype(BF16) for x in b_g]
    c_gb = [x.astype(BF16) for x in c_g]
    cb = [_dot_nt(c_gb[g], b_gb[g]) for g in gs]
    h_old = [hT[g] for g in gs]
    y_inter = jnp.concatenate([_dot(c_gb[g], h_old[g].astype(BF16)) for g in gs], axis=1)
    for h in range(n_heads):
        diff = acs[:, h:h + 1] - acs_t[h:h + 1, :]
        lmat = jnp.exp(jnp.where(causal, diff, -jnp.inf))
        mm = cb[h // hg] * lmat * dt_t[h:h + 1, :]
        x_h = xs[:, h * SSD_HEAD_DIM:(h + 1) * SSD_HEAD_DIM]
        ybuf[:, h * SSD_HEAD_DIM:(h + 1) * SSD_HEAD_DIM] = _dot(mm.astype(BF16), x_h.astype(BF16))
    for g in gs:
        gsl = slice(g * gw, (g + 1) * gw)
        xd = xs[:, gsl] * dec_full[:, gsl]
        hT[g] = h_old[g] * eacs_full[CHUNK - 1:CHUNK, gsl] + _dot(b_g[g].T.astype(BF16), xd.astype(BF16))

    y = ybuf[...] + y_inter * eacs_full + dsk_ref[...] * xs
    y = y * _silu(z_ref[...])
    y = y * lax.rsqrt(jnp.mean(y * y, axis=-1, keepdims=True) + LN_EPS) * ng_ref[...]
    y_ref[...] = y.astype(y_ref.dtype)

    @pl.when(c == nc - 1)
    def _():
        hT_out_ref[0] = hT[...]


def _ssd_chunk(xbc, z, gates, p, bsz, seq):
    nc = seq // CHUNK
    cd = xbc.shape[1]
    d_ssd = z.shape[1]
    n_heads = d_ssd // SSD_HEAD_DIM
    gw = d_ssd // SSD_GROUPS
    tok = lambda b, c: (b * nc + c, 0)
    cst = lambda b, c: (0, 0)
    return pl.pallas_call(
        functools.partial(_ssd_chunk_kernel, nc=nc, d_ssd=d_ssd, n_heads=n_heads),
        grid=(bsz, nc),
        in_specs=[pl.BlockSpec((CHUNK, cd), tok),
                  pl.BlockSpec((CHUNK, d_ssd), tok),
                  pl.BlockSpec((CHUNK, 2 * LANES), tok),
                  pl.BlockSpec((CONV_K, cd), cst),
                  pl.BlockSpec((1, cd), cst),
                  pl.BlockSpec((1, LANES), cst),
                  pl.BlockSpec((1, LANES), cst),
                  pl.BlockSpec((1, d_ssd), cst),
                  pl.BlockSpec((1, d_ssd), cst),
                  pl.BlockSpec((CHUNK, CHUNK), cst),
                  pl.BlockSpec((LANES, d_ssd), cst)],
        out_specs=[pl.BlockSpec((CHUNK, d_ssd), tok),
                   pl.BlockSpec((1, SSD_GROUPS, SSD_STATE, gw), lambda b, c: (b, 0, 0, 0))],
        out_shape=[jax.ShapeDtypeStruct((bsz * seq, d_ssd), BF16),
                   jax.ShapeDtypeStruct((bsz, SSD_GROUPS, SSD_STATE, gw), F32)],
        scratch_shapes=[pltpu.VMEM((CHUNK + SUBLANES, cd), F32),
                        pltpu.VMEM((SSD_GROUPS, SSD_STATE, gw), F32),
                        pltpu.VMEM((CHUNK, d_ssd), F32)],
        compiler_params=_cparams("arbitrary", "arbitrary"),
        name="ssd_chunk",
    )(xbc, z, gates, p["conv_ssd_w"], p["conv_ssd_b"], p["dt_bias_row"], p["a_log_row"],
      p["d_skip_row"], p["ssd_norm_g"], p["tril"], p["ssd_expand"])


def _mlstm_chunk_kernel(u_ref, v_ref, o_ref, g_ref, cw_ref, cb_ref, wq_ref, wk_ref, bi_ref, bf_ref, ng_ref,
                        tril_ref, y_ref, c_out_ref, n_out_ref, m_out_ref, upad, c_st, n_st, m_st,
                        *, nc, n_heads, dk):
    c = pl.program_id(1)
    dm = upad.shape[1]

    @pl.when(c == 0)
    def _():
        upad[0:SUBLANES, :] = jnp.zeros((SUBLANES, dm), F32)
        c_st[...] = jnp.zeros_like(c_st)
        n_st[...] = jnp.zeros_like(n_st)
        m_st[...] = jnp.zeros_like(m_st)

    upad[SUBLANES:SUBLANES + CHUNK, :] = u_ref[...]
    conv = cb_ref[...]
    for j in range(CONV_K):
        r0 = SUBLANES - (CONV_K - 1) + j
        conv = conv + cw_ref[j:j + 1, :] * upad[r0:r0 + CHUNK, :]
    upad[0:SUBLANES, :] = upad[CHUNK:CHUNK + SUBLANES, :]
    uc = _silu(conv).astype(BF16)

    logi = g_ref[:, 0:LANES] + bi_ref[...]
    logf = _log_sigmoid(g_ref[:, LANES:2 * LANES] + bf_ref[...])
    bcs = _dot_sel_l(tril_ref[...], logf)
    m_prev = m_st[0:1, :]
    b_end = bcs[CHUNK - 1:CHUNK, :]
    g_all = b_end - bcs + logi
    m_new = jnp.maximum(b_end + m_prev, jnp.max(g_all, axis=0, keepdims=True))
    w_old = jnp.exp(b_end + m_prev - m_new)
    w_s_all = jnp.exp(g_all - m_new)
    inter_all = bcs + m_prev
    bcs_t = bcs.T
    logi_t = logi.T

    row = lax.broadcasted_iota(jnp.int32, (CHUNK, CHUNK), 0)
    col = lax.broadcasted_iota(jnp.int32, (CHUNK, CHUNK), 1)
    causal = row >= col
    scale = dk ** -0.5

    hs = range(n_heads)
    sls = [slice(h * dk, (h + 1) * dk) for h in hs]
    gls = [GATE_I0 + h for h in hs]
    q = [_dot(uc[:, sls[h]], wq_ref[h]) for h in hs]
    k = [_dot(uc[:, sls[h]], wk_ref[h]) * scale for h in hs]
    qb = [x.astype(BF16) for x in q]
    kb = [x.astype(BF16) for x in k]
    vb = [v_ref[:, sls[h]].astype(BF16) for h in hs]
    dmat = [jnp.where(causal, bcs[:, g:g + 1] - bcs_t[g:g + 1, :] + logi_t[g:g + 1, :], -jnp.inf) for g in gls]
    inter = [inter_all[:, g:g + 1] for g in gls]
    m_t = [jnp.maximum(inter[h], jnp.max(dmat[h], axis=1, keepdims=True)) for h in hs]
    w_inter = [jnp.exp(inter[h] - m_t[h]) for h in hs]
    qk = [_dot_nt(qb[h], kb[h]) for h in hs]
    c_old = [c_st[h] for h in hs]
    n_old = [n_st[h][0:1, :] for h in hs]
    qc = [_dot(qb[h], c_old[h].astype(BF16)) for h in hs]
    att = [qk[h] * jnp.exp(dmat[h] - m_t[h]) for h in hs]
    av = [_dot(att[h].astype(BF16), vb[h]) for h in hs]
    kw = [k[h] * w_s_all[:, gls[h]:gls[h] + 1] for h in hs]
    kv = [_dot(kw[h].T.astype(BF16), vb[h]) for h in hs]
    num = [av[h] + w_inter[h] * qc[h] for h in hs]
    den = [jnp.sum(att[h], axis=1, keepdims=True) + w_inter[h] * jnp.sum(q[h] * n_old[h], axis=1, keepdims=True)
           for h in hs]
    hout = [num[h] / jnp.maximum(jnp.abs(den[h]), jnp.exp(-m_t[h])) for h in hs]
    mu = [jnp.mean(x, axis=-1, keepdims=True) for x in hout]
    cen = [hout[h] - mu[h] for h in hs]
    var = [jnp.mean(jnp.square(x), axis=-1, keepdims=True) for x in cen]
    for h in hs:
        sl = sls[h]
        hn = cen[h] * lax.rsqrt(var[h] + LN_EPS) * ng_ref[:, sl]
        y_ref[:, sl] = (_sigmoid(o_ref[:, sl]) * hn).astype(y_ref.dtype)
        wo = w_old[:, gls[h]:gls[h] + 1]
        c_st[h] = wo * c_old[h] + kv[h]
        n_st[h] = jnp.broadcast_to(wo * n_old[h] + jnp.sum(kw[h], axis=0, keepdims=True), (SUBLANES, dk))

    m_st[...] = jnp.broadcast_to(m_new, (SUBLANES, LANES))

    @pl.when(c == nc - 1)
    def _():
        c_out_ref[0] = c_st[...]
        n_out_ref[0] = n_st[...]
        m_out_ref[0] = m_st[...]


def _mlstm_chunk(u, v, o, gates, p, bsz, seq):
    nc = seq // CHUNK
    dm = u.shape[1]
    n_heads, dk, _ = p["w_mq"].shape
    tok = lambda b, c: (b * nc + c, 0)
    cst = lambda b, c: (0, 0)
    cst3 = lambda b, c: (0, 0, 0)
    return pl.pallas_call(
        functools.partial(_mlstm_chunk_kernel, nc=nc, n_heads=n_heads, dk=dk),
        grid=(bsz, nc),
        in_specs=[pl.BlockSpec((CHUNK, dm), tok),
                  pl.BlockSpec((CHUNK, dm), tok),
                  pl.BlockSpec((CHUNK, dm), tok),
                  pl.BlockSpec((CHUNK, 2 * LANES), tok),
                  pl.BlockSpec((CONV_K, dm), cst),
                  pl.BlockSpec((1, dm), cst),
                  pl.BlockSpec((n_heads, dk, dk), cst3),
                  pl.BlockSpec((n_heads, dk, dk), cst3),
                  pl.BlockSpec((1, LANES), cst),
                  pl.BlockSpec((1, LANES), cst),
                  pl.BlockSpec((1, dm), cst),
                  pl.BlockSpec((CHUNK, CHUNK), cst)],
        out_specs=[pl.BlockSpec((CHUNK, dm), tok),
                   pl.BlockSpec((1, n_heads, dk, dk), lambda b, c: (b, 0, 0, 0)),
                   pl.BlockSpec((1, n_heads, SUBLANES, dk), lambda b, c: (b, 0, 0, 0)),
                   pl.BlockSpec((1, SUBLANES, LANES), lambda b, c: (b, 0, 0))],
        out_shape=[jax.ShapeDtypeStruct((bsz * seq, dm), BF16),
                   jax.ShapeDtypeStruct((bsz, n_heads, dk, dk), F32),
                   jax.ShapeDtypeStruct((bsz, n_heads, SUBLANES, dk), F32),
                   jax.ShapeDtypeStruct((bsz, SUBLANES, LANES), F32)],
        scratch_shapes=[pltpu.VMEM((CHUNK + SUBLANES, dm), F32),
                        pltpu.VMEM((n_heads, dk, dk), F32),
                        pltpu.VMEM((n_heads, SUBLANES, dk), F32),
                        pltpu.VMEM((SUBLANES, LANES), F32)],
        compiler_params=_cparams("arbitrary", "arbitrary"),
        name="mlstm_chunk",
    )(u, v, o, gates, p["conv_m_w"], p["conv_m_b"], p["w_mq"], p["w_mk"], p["b_i_row"], p["b_f_row"],
      p["mlstm_norm_g"], p["tril"])


def _col_bcast(row, n_rows):
    return jnp.broadcast_to(row, (LANES, row.shape[1])).T[0:n_rows, :]


def _per_sequence(one_sequence, n_seq_in, n_seq_out):
    def kernel(*refs, **static):
        seq_in, shared, seq_out = refs[:n_seq_in], refs[n_seq_in:len(refs) - n_seq_out], refs[len(refs) - n_seq_out:]
        for s in range(seq_in[0].shape[0]):
            one = slice(s, s + 1)
            one_sequence(*(r.at[one] for r in seq_in), *shared, *(r.at[one] for r in seq_out), **static)
    return kernel


def _ssd_step_one(xbc_ref, z_ref, g_ref, buf_ref, st_ref, cw_ref, cb_ref, dtb_ref, alog_ref, dsk_ref, ng_ref,
                  exp_ref, y_ref, st_out_ref, buf_out_ref, *, d_ssd, n_heads):
    gw = d_ssd // SSD_GROUPS
    x_row = xbc_ref[0]
    buf_out_ref[0, 0:CONV_K - 2, :] = buf_ref[0, 1:CONV_K - 1, :]
    buf_out_ref[0, CONV_K - 2:CONV_K - 1, :] = x_row
    conv = cb_ref[...] + cw_ref[CONV_K - 1:CONV_K, :] * x_row
    for j in range(CONV_K - 1):
        conv = conv + cw_ref[j:j + 1, :] * buf_ref[0, j:j + 1, :]
    xc = _silu(conv)
    xs = xc[:, 0:d_ssd]

    lane = lax.broadcasted_iota(jnp.int32, (1, LANES), 1)
    a_row = jnp.where(lane < n_heads, -jnp.exp(alog_ref[...]), 0.0)
    dt = _softplus(g_ref[0][:, 0:LANES] + dtb_ref[...])
    dec = jnp.exp(dt * a_row)
    both = _dot_sel_r(jnp.concatenate([jnp.broadcast_to(dt, (SUBLANES, LANES)),
                                       jnp.broadcast_to(dec, (SUBLANES, LANES))], axis=0), exp_ref[...])
    dt_full = both[0:1]
    dec_full = both[SUBLANES:SUBLANES + 1]
    dtx = dt_full * xs
    dtx_col = _col_bcast(dtx, d_ssd)
    dec_col = _col_bcast(dec_full, d_ssd)

    y_parts = []
    for g in range(SSD_GROUPS):
        b_g = xc[:, d_ssd + g * SSD_STATE:d_ssd + (g + 1) * SSD_STATE]
        c_g = xc[:, d_ssd + (SSD_GROUPS + g) * SSD_STATE:d_ssd + (SSD_GROUPS + g + 1) * SSD_STATE]
        h_g = st_ref[0, g * gw:(g + 1) * gw, :]
        cb = jnp.sum(c_g * b_g, axis=-1, keepdims=True)
        c8 = jnp.broadcast_to(c_g, (SUBLANES, SSD_STATE)).astype(BF16)
        y_int = _dot_nt(c8, h_g.astype(BF16))[0:1]
        sl = slice(g * gw, (g + 1) * gw)
        y_parts.append(cb * dtx[:, sl] + dec_full[:, sl] * y_int)
        st_out_ref[0, sl, :] = h_g * dec_col[sl, :] + dtx_col[sl, :] * b_g
    y = jnp.concatenate(y_parts, axis=1) + dsk_ref[...] * xs
    y = y * _silu(z_ref[0])
    y = y * lax.rsqrt(jnp.mean(y * y, axis=-1, keepdims=True) + LN_EPS) * ng_ref[...]
    y_ref[0] = y.astype(y_ref.dtype)


def _ssd_step(xbc, z, gates, buf, state, p):
    nb, cd = xbc.shape
    d_ssd = z.shape[1]
    n_heads = d_ssd // SSD_HEAD_DIM
    row3 = lambda b: (b, 0, 0)
    cst = lambda b: (0, 0)
    st2 = state.reshape(nb, d_ssd, SSD_STATE)
    sb = STEP_SEQS
    assert nb % sb == 0
    y, st_new, buf_new = pl.pallas_call(
        functools.partial(_per_sequence(_ssd_step_one, 5, 3), d_ssd=d_ssd, n_heads=n_heads),
        grid=(nb // sb,),
        in_specs=[pl.BlockSpec((sb, 1, cd), row3),
                  pl.BlockSpec((sb, 1, d_ssd), row3),
                  pl.BlockSpec((sb, 1, 2 * LANES), row3),
                  pl.BlockSpec((sb, CONV_K - 1, cd), row3),
                  pl.BlockSpec((sb, d_ssd, SSD_STATE), row3),
                  pl.BlockSpec((CONV_K, cd), cst),
                  pl.BlockSpec((1, cd), cst),
                  pl.BlockSpec((1, LANES), cst),
                  pl.BlockSpec((1, LANES), cst),
                  pl.BlockSpec((1, d_ssd), cst),
                  pl.BlockSpec((1, d_ssd), cst),
                  pl.BlockSpec((LANES, d_ssd), cst)],
        out_specs=[pl.BlockSpec((sb, 1, d_ssd), row3),
                   pl.BlockSpec((sb, d_ssd, SSD_STATE), row3),
                   pl.BlockSpec((sb, CONV_K - 1, cd), row3)],
        out_shape=[jax.ShapeDtypeStruct((nb, 1, d_ssd), BF16),
                   jax.ShapeDtypeStruct((nb, d_ssd, SSD_STATE), F32),
                   jax.ShapeDtypeStruct((nb, CONV_K - 1, cd), F32)],
        compiler_params=_cparams("arbitrary"),
        name="ssd_step",
    )(xbc.reshape(nb, 1, cd), z.reshape(nb, 1, d_ssd), gates.reshape(nb, 1, 2 * LANES), buf, st2,
      p["conv_ssd_w"], p["conv_ssd_b"], p["dt_bias_row"], p["a_log_row"], p["d_skip_row"], p["ssd_norm_g"],
      p["ssd_expand"])
    return y.reshape(nb, d_ssd), st_new.reshape(state.shape), buf_new


def _mlstm_step_one(u_ref, v_ref, o_ref, g_ref, buf_ref, c_ref, n_ref, m_ref, cw_ref, cb_ref, wq_ref, wk_ref,
                    bi_ref, bf_ref, ng_ref, hexp_ref, y_ref, c_out_ref, n_out_ref, m_out_ref, buf_out_ref,
                    *, n_heads, dk):
    buf_out_ref[0, 0:CONV_K - 2, :] = buf_ref[0, 1:CONV_K - 1, :]
    buf_out_ref[0, CONV_K - 2:CONV_K - 1, :] = u_ref[0]
    conv = cb_ref[...] + cw_ref[CONV_K - 1:CONV_K, :] * u_ref[0]
    for j in range(CONV_K - 1):
        conv = conv + cw_ref[j:j + 1, :] * buf_ref[0, j:j + 1, :]
    uc = jnp.broadcast_to(_silu(conv), (SUBLANES, conv.shape[1])).astype(BF16)

    gt = g_ref[0]
    logi = gt[:, 0:LANES] + bi_ref[...]
    logf = _log_sigmoid(gt[:, LANES:2 * LANES] + bf_ref[...])
    m_prev = m_ref[0]
    inter = logf + m_prev
    m_t = jnp.maximum(inter, logi)
    w_inter = jnp.exp(inter - m_t)
    w_in = jnp.exp(logi - m_t)
    floor = jnp.exp(-m_t)
    rows = jnp.concatenate([w_inter, w_in, floor, jnp.zeros((SUBLANES - 3, LANES), F32)], axis=0)
    full = _dot_sel_r(rows, hexp_ref[...])
    scale = dk ** -0.5

    for h in range(n_heads):
        sl = slice(h * dk, (h + 1) * dk)
        q = _dot(uc[:, sl], wq_ref[h])[0:1]
        k = _dot(uc[:, sl], wk_ref[h])[0:1] * scale
        v_h = v_ref[0][:, sl]
        wi = full[0:1, sl]
        ws = full[1:2, sl]
        fl = full[2:3, sl]
        c_h = c_ref[0, h]
        n_h = n_ref[0, h:h + 1, :]
        att = jnp.sum(q * k, axis=-1, keepdims=True) * ws
        q8 = jnp.broadcast_to(q, (SUBLANES, dk)).astype(BF16)
        num = att * v_h + wi * _dot(q8, c_h.astype(BF16))[0:1]
        den = att + wi * jnp.sum(q * n_h, axis=-1, keepdims=True)
        hout = num / jnp.maximum(jnp.abs(den), fl)
        mu = jnp.mean(hout, axis=-1, keepdims=True)
        var = jnp.mean(jnp.square(hout - mu), axis=-1, keepdims=True)
        hn = (hout - mu) * lax.rsqrt(var + LN_EPS) * ng_ref[:, sl]
        y_ref[0, :, sl] = (_sigmoid(o_ref[0][:, sl]) * hn).astype(y_ref.dtype)
        kw = k * ws
        kw_col = _col_bcast(kw, dk)
        wi_full = jnp.broadcast_to(wi, (dk, dk))
        for half in range(dk // LANES):
            hs = slice(half * LANES, (half + 1) * LANES)
            c_out_ref[0, h, :, hs] = wi_full[:, hs] * c_h[:, hs] + kw_col * v_h[:, hs]
        n_out_ref[0, h:h + 1, :] = wi * n_h + kw
    m_out_ref[0] = m_t


def _mlstm_step(u, v, o, gates, buf, c_st, n_st, m_row, p):
    nb, dm = u.shape
    n_heads, dk, _ = p["w_mq"].shape
    row3 = lambda b: (b, 0, 0)
    cst = lambda b: (0, 0)
    cst3 = lambda b: (0, 0, 0)
    r3 = lambda a: a.reshape(nb, 1, a.shape[-1])
    sb = STEP_SEQS
    assert nb % sb == 0
    y, c_new, n_new, m_new, buf_new = pl.pallas_call(
        functools.partial(_per_sequence(_mlstm_step_one, 8, 5), n_heads=n_heads, dk=dk),
        grid=(nb // sb,),
        in_specs=[pl.BlockSpec((sb, 1, dm), row3),
                  pl.BlockSpec((sb, 1, dm), row3),
                  pl.BlockSpec((sb, 1, dm), row3),
                  pl.BlockSpec((sb, 1, 2 * LANES), row3),
                  pl.BlockSpec((sb, CONV_K - 1, dm), row3),
                  pl.BlockSpec((sb, n_heads, dk, dk), lambda b: (b, 0, 0, 0)),
                  pl.BlockSpec((sb, n_heads, dk), row3),
                  pl.BlockSpec((sb, 1, LANES), row3),
                  pl.BlockSpec((CONV_K, dm), cst),
                  pl.BlockSpec((1, dm), cst),
                  pl.BlockSpec((n_heads, dk, dk), cst3),
                  pl.BlockSpec((n_heads, dk, dk), cst3),
                  pl.BlockSpec((1, LANES), cst),
                  pl.BlockSpec((1, LANES), cst),
                  pl.BlockSpec((1, dm), cst),
                  pl.BlockSpec((LANES, dm), cst)],
        out_specs=[pl.BlockSpec((sb, 1, dm), row3),
                   pl.BlockSpec((sb, n_heads, dk, dk), lambda b: (b, 0, 0, 0)),
                   pl.BlockSpec((sb, n_heads, dk), row3),
                   pl.BlockSpec((sb, 1, LANES), row3),
                   pl.BlockSpec((sb, CONV_K - 1, dm), row3)],
        out_shape=[jax.ShapeDtypeStruct((nb, 1, dm), BF16),
                   jax.ShapeDtypeStruct(c_st.shape, F32),
                   jax.ShapeDtypeStruct(n_st.shape, F32),
                   jax.ShapeDtypeStruct((nb, 1, LANES), F32),
                   jax.ShapeDtypeStruct((nb, CONV_K - 1, dm), F32)],
        compiler_params=_cparams("arbitrary"),
        name="mlstm_step",
    )(r3(u), r3(v), r3(o), r3(gates), buf, c_st, n_st, m_row, p["conv_m_w"], p["conv_m_b"], p["w_mq"],
      p["w_mk"], p["b_i_row"], p["b_f_row"], p["mlstm_norm_g"], p["mlstm_expand"])
    return y.reshape(nb, dm), c_new, n_new, m_new, buf_new


def _outproj_kernel(ys_ref, hm_ref, x_ref, w_ref, g_ref, b_ref, x1t_ref, *, alpha):
    d_ssd = ys_ref.shape[1]
    tm = ys_ref.shape[0]
    subs = [slice(r0, min(r0 + OUTPROJ_ROWS, tm)) for r0 in range(0, tm, OUTPROJ_ROWS)]
    mix = [_dot(ys_ref[rs, :], w_ref[0:d_ssd, :]) + _dot(hm_ref[rs, :], w_ref[d_ssd:, :]) for rs in subs]
    for rs, mx in zip(subs, mix):
        r = alpha * x_ref[rs, :] + mx
        mu = jnp.mean(r, axis=-1, keepdims=True)
        var = jnp.mean(jnp.square(r - mu), axis=-1, keepdims=True)
        x1 = (r - mu) * lax.rsqrt(var + LN_EPS) * g_ref[...] + b_ref[...]
        x1t_ref[:, rs] = x1.T


def _outproj(ys, hm, x2, p, alpha, tm):
    m, d = x2.shape
    d_ssd, d_m = ys.shape[1], hm.shape[1]
    return pl.pallas_call(
        functools.partial(_outproj_kernel, alpha=alpha),
        grid=(m // tm,),
        in_specs=[pl.BlockSpec((tm, d_ssd), lambda i: (i, 0)),
                  pl.BlockSpec((tm, d_m), lambda i: (i, 0)),
                  pl.BlockSpec((tm, d), lambda i: (i, 0)),
                  pl.BlockSpec((d_ssd + d_m, d), lambda i: (0, 0)),
                  pl.BlockSpec((1, d), lambda i: (0, 0)),
                  pl.BlockSpec((1, d), lambda i: (0, 0))],
        out_specs=pl.BlockSpec((d, tm), lambda i: (0, i)),
        out_shape=jax.ShapeDtypeStruct((d, m), F32),
        compiler_params=_cparams("arbitrary"),
        name="outproj_ln1",
    )(ys, hm, x2, p["w_out"], p["ln1_g"], p["ln1_b"])


def _oddeven_merge_sort_pairs(n):
    pairs = []
    p = 1
    while p < n:
        k = p
        while k >= 1:
            for j in range(k % p, n - k, 2 * k):
                for i in range(min(k, n - j - k)):
                    if (i + j) // (2 * p) == (i + j + k) // (2 * p):
                        pairs.append((i + j, i + j + k))
            k //= 2
        p *= 2
    return pairs


_SORT_TOPK = _oddeven_merge_sort_pairs(PEER_TOPK)


def _compare_exchange(lst, i, j):
    hi, lo = jnp.maximum(lst[i], lst[j]), jnp.minimum(lst[i], lst[j])
    lst[i], lst[j] = hi, lo


def _bitonic_to_sorted(t):
    d = PEER_TOPK // 2
    while d >= 1:
        for i in range(PEER_TOPK):
            if i & d == 0:
                _compare_exchange(t, i, i + d)
        d //= 2
    return t


def _merge_top(x, y):
    neg = jnp.full_like(x[0], -jnp.inf)
    x = x + [neg] * (PEER_TOPK - len(x))
    y = y + [neg] * (PEER_TOPK - len(y))
    return _bitonic_to_sorted([jnp.maximum(x[k], y[PEER_TOPK - 1 - k]) for k in range(PEER_TOPK)])


def _top_rows(s):
    lst = [s[SUBLANES * g:SUBLANES * (g + 1), :] for g in range(PEER_TOPK)]
    for i, j in _SORT_TOPK:
        _compare_exchange(lst, i, j)
    shift = SUBLANES // 2
    while shift >= 1:
        lst = _merge_top(lst, [pltpu.roll(a, shift, 0) for a in lst])
        shift //= 2
    return lst


def _peer_topk_kernel(x1t_ref, wqt_ref, keys_ref, s1_ref, s2_ref, st_ref, b_scr, q_scr, a_scr, *, n_heads):
    tq = x1t_ref.shape[1]
    q_scr[...] = _dot(wqt_ref[...], x1t_ref[...].astype(BF16))
    half = q_scr.shape[0] // (2 * n_heads)

    for h in range(n_heads):
        for side, (s_ref, top_scr) in enumerate(((s1_ref, a_scr), (s2_ref, b_scr))):
            r0 = (2 * h + side) * half
            s = _dot(keys_ref[h, side], q_scr[r0:r0 + half, :].astype(BF16))
            for t in range(tq // LANES):
                ls = slice(t * LANES, (t + 1) * LANES)
                s_ref[h, t] = s[:, ls]
                top = _top_rows(s[:, ls])
                for k in range(PEER_TOPK):
                    top_scr[k, h:h + 1, ls] = top[k][0:1, :]

    for t in range(tq // LANES):
        ls = slice(t * LANES, (t + 1) * LANES)
        a = [a_scr[k, :, ls] for k in range(PEER_TOPK)]
        b = [b_scr[k, :, ls] for k in range(PEER_TOPK)]
        single = PEER_TOPK // 2
        lists = [[a[i] + b[j] for j in range(PEER_TOPK // (i + 1))] for i in range(single)]
        lists.append([a[i] + b[0] for i in range(single, PEER_TOPK)])
        merged = lists[0]
        for other in lists[1:]:
            merged = _merge_top(merged, other)
        tau = merged[PEER_TOPK - 1]
        top = a[0] + b[0]
        zsum = jnp.zeros_like(tau)
        for cand in (c for lst in lists for c in lst):
            zsum = zsum + jnp.where(cand >= tau, jnp.exp(cand - top), 0.0)
        st_ref[0, :, ls] = tau
        st_ref[1, :, ls] = a[0]
        st_ref[2, :, ls] = b[0]
        st_ref[3, :, ls] = 1.0 / zsum


def _peer_topk(x1t, p, tq):
    d, m = x1t.shape
    n_heads, _, n_keys, half = p["peer_keys"].shape
    nq = p["peer_wqT"].shape[0]
    assert n_heads == SUBLANES and n_keys == PEER_TOPK * SUBLANES
    keyb = lambda i: (0, i, 0, 0)
    return pl.pallas_call(
        functools.partial(_peer_topk_kernel, n_heads=n_heads),
        grid=(m // tq,),
        in_specs=[pl.BlockSpec((d, tq), lambda i: (0, i)),
                  pl.BlockSpec((nq, d), lambda i: (0, 0)),
                  pl.BlockSpec((n_heads, 2, n_keys, half), lambda i: (0, 0, 0, 0))],
        out_specs=[pl.BlockSpec((n_heads, tq // LANES, n_keys, LANES), keyb),
                   pl.BlockSpec((n_heads, tq // LANES, n_keys, LANES), keyb),
                   pl.BlockSpec((N_STATS, SUBLANES, tq), lambda i: (0, 0, i)),
                   pl.BlockSpec((PEER_TOPK, SUBLANES, tq), lambda i: (0, 0, i))],
        out_shape=[jax.ShapeDtypeStruct((n_heads, m // LANES, n_keys, LANES), F32),
                   jax.ShapeDtypeStruct((n_heads, m // LANES, n_keys, LANES), F32),
                   jax.ShapeDtypeStruct((N_STATS, SUBLANES, m), F32),
                   jax.ShapeDtypeStruct((PEER_TOPK, SUBLANES, m), F32)],
        scratch_shapes=[pltpu.VMEM((nq, tq), F32),
                        pltpu.VMEM((PEER_TOPK, SUBLANES, tq), F32)],
        compiler_params=_cparams("arbitrary"),
        name="peer_topk",
    )(x1t, p["peer_wqT"], p["peer_keys"])


def _peer_dense_kernel(x1t_ref, s1_ref, s2_ref, st_ref, bl_ref, u_ref, vt_ref, g_ref, b_ref, out_ref,
                       xb_scr, e2_scr, c_scr, thr_scr, acc_scr, s_scr, w_scr, thr_r, cr, *, alpha, n_heads):
    j = pl.program_id(1)
    ec = u_ref.shape[0]
    tb = x1t_ref.shape[1]
    nt = tb // LANES
    n_keys = s1_ref.shape[2]
    na = ec // n_keys

    @pl.when(j == 0)
    def _():
        xb_scr[...] = x1t_ref[...].astype(BF16)
        for h in range(n_heads):
            for t in range(nt):
                ls = slice(t * LANES, (t + 1) * LANES)
                e2_scr[h, t] = jnp.exp(s2_ref[h, t] - st_ref[2, h:h + 1, ls])
                s1 = s1_ref[h, t]
                c_scr[h, t] = jnp.exp(s1 - st_ref[1, h:h + 1, ls]) * (0.5 * st_ref[3, h:h + 1, ls])
                tau = st_ref[0, h:h + 1, ls]
                thr = jnp.full((n_keys, LANES), jnp.inf, F32)
                for k in range(PEER_TOPK):
                    bk = bl_ref[k, h:h + 1, ls]
                    thr = jnp.where(s1 + bk >= tau, bk, thr)
                thr_scr[h, t] = thr
        acc_scr[...] = jnp.zeros_like(acc_scr)

    s_scr[:, 0:tb] = _dot(u_ref[...], xb_scr[...])

    g0 = pl.multiple_of(j * na, na)
    for h in range(n_heads):
        for t in range(nt):
            ls = slice(t * LANES, (t + 1) * LANES)
            thr_blk = thr_scr[h, t, pl.ds(g0, na), :]
            c_blk = c_scr[h, t, pl.ds(g0, na), :]
            for al in range(na):
                thr_r[h * na + al, :, ls] = thr_blk[al:al + 1, :]
                cr[h * na + al, :, ls] = c_blk[al:al + 1, :]

    def first_key(al, carry):
        r0 = pl.multiple_of(al * n_keys, n_keys)
        for t in range(nt):
            ls = slice(t * LANES, (t + 1) * LANES)
            gate = jnp.zeros((n_keys, LANES), F32)
            for h in range(n_heads):
                gate = gate + jnp.where(s2_ref[h, t] >= thr_r[h * na + al][:, ls],
                                        e2_scr[h, t] * cr[h * na + al][:, ls], 0.0)
            x = s_scr[pl.ds(r0, n_keys), ls]
            hid = x + x * lax.erf(x * (1.0 / math.sqrt(2.0)))
            w_scr[pl.ds(r0, n_keys), ls] = (gate * hid).astype(BF16)
        return carry

    lax.fori_loop(0, na, first_key, 0)
    acc_scr[...] += _dot(vt_ref[...], w_scr[:, 0:tb])

    @pl.when(j == pl.num_programs(1) - 1)
    def _():
        r = alpha * x1t_ref[...] + acc_scr[...]
        mu = jnp.mean(r, axis=0, keepdims=True)
        var = jnp.mean(jnp.square(r - mu), axis=0, keepdims=True)
        y = (r - mu) * lax.rsqrt(var + LN_EPS) * g_ref[...] + b_ref[...]
        out_ref[...] = y.T


def _peer_dense(x1t, s1, s2, st, blist, p, alpha, tb):
    d, m = x1t.shape
    n_heads, _, n_keys, _ = s1.shape
    n_exp = p["peer_u"].shape[0]
    ec = PEER_CHUNK
    nj = n_exp // ec
    na = ec // n_keys
    assert na % SUBLANES == 0
    nt = tb // LANES
    keyb = lambda i, j: (0, i, 0, 0)
    tokb = lambda i, j: (0, 0, i)
    pitch = tb + LANES
    return pl.pallas_call(
        functools.partial(_peer_dense_kernel, alpha=alpha, n_heads=n_heads),
        grid=(m // tb, nj),
        in_specs=[pl.BlockSpec((d, tb), lambda i, j: (0, i)),
                  pl.BlockSpec((n_heads, nt, n_keys, LANES), keyb),
                  pl.BlockSpec((n_heads, nt, n_keys, LANES), keyb),
                  pl.BlockSpec((N_STATS, SUBLANES, tb), tokb),
                  pl.BlockSpec((PEER_TOPK, SUBLANES, tb), tokb),
                  pl.BlockSpec((ec, d), lambda i, j: (j, 0)),
                  pl.BlockSpec((d, ec), lambda i, j: (0, j)),
                  pl.BlockSpec((d, 1), lambda i, j: (0, 0)),
                  pl.BlockSpec((d, 1), lambda i, j: (0, 0))],
        out_specs=pl.BlockSpec((tb, d), lambda i, j: (i, 0)),
        out_shape=jax.ShapeDtypeStruct((m, d), F32),
        scratch_shapes=[pltpu.VMEM((d, tb), BF16),
                        pltpu.VMEM((n_heads, nt, n_keys, LANES), F32),
                        pltpu.VMEM((n_heads, nt, n_keys, LANES), F32),
                        pltpu.VMEM((n_heads, nt, n_keys, LANES), F32),
                        pltpu.VMEM((d, tb), F32),
                        pltpu.VMEM((ec, pitch), F32),
                        pltpu.VMEM((ec, pitch), BF16),
                        pltpu.VMEM((n_heads * na, 1, tb), F32),
                        pltpu.VMEM((n_heads * na, 1, tb), F32)],
        compiler_params=_cparams("arbitrary", "arbitrary"),
        name="peer_dense",
    )(x1t, s1, s2, st, blist, p["peer_u"], p["peer_vT"], p["ln2_g_col"], p["ln2_b_col"])


def _pad_row(vec, offset):
    return jnp.pad(vec.astype(F32), (offset, LANES - offset - vec.shape[0]))[None, :]


def _prep_layer(l, w_in, conv_ssd_w, conv_ssd_b, dt_bias, a_log, d_skip, ssd_norm_g, conv_m_w, conv_m_b,
                w_mq, w_mk, b_igate, b_fgate, mlstm_norm_g, w_out, ln1_g, ln1_b, peer_wq, peer_keys,
                peer_u, peer_v, ln2_g, ln2_b):
    d_model = w_in.shape[1]
    d_ssd = ssd_norm_g.shape[1]
    cd = conv_ssd_w.shape[2]
    n_sh = dt_bias.shape[1]
    d_m = conv_m_w.shape[2]
    n_mh = w_mq.shape[1]
    dk = w_mq.shape[2]
    assert n_sh <= GATE_I0 and GATE_I0 + n_mh <= LANES and d_ssd == n_sh * SSD_HEAD_DIM
    w = w_in[l]
    offs = [0]
    for n in (d_ssd, cd, n_sh, d_m, d_m, d_m, n_mh, n_mh):
        offs.append(offs[-1] + n)
    z_w, xbc_w, dt_w, u_w, v_w, o_w, i_w, f_w = (w[:, offs[k]:offs[k + 1]] for k in range(8))
    tail = ((0, 0), (0, LANES - GATE_I0 - n_mh))
    gate_a = jnp.pad(jnp.concatenate([jnp.pad(dt_w, ((0, 0), (0, GATE_I0 - n_sh))), i_w], axis=1), tail)
    gate_b = jnp.pad(f_w, ((0, 0), (GATE_I0, LANES - GATE_I0 - n_mh)))
    w_cat = jnp.concatenate([z_w, xbc_w, u_w, v_w, o_w, gate_a, gate_b], axis=1).astype(BF16)
    widths = (d_ssd, cd, d_m, d_m, d_m, 2 * LANES)

    lane = jnp.arange(LANES)[:, None]
    ssd_expand = (lane == (jnp.arange(d_ssd)[None, :] // SSD_HEAD_DIM)).astype(BF16)
    mlstm_expand = (lane == (GATE_I0 + jnp.arange(d_m)[None, :] // dk)).astype(BF16)
    t = jnp.arange(CHUNK)
    tril = (t[:, None] >= t[None, :]).astype(BF16)
    p = {
        "w_cat": w_cat,
        "conv_ssd_w": conv_ssd_w[l], "conv_ssd_b": conv_ssd_b[l][None, :],
        "dt_bias_row": _pad_row(dt_bias[l], 0), "a_log_row": _pad_row(a_log[l], 0),
        "d_skip_row": jnp.repeat(d_skip[l], SSD_HEAD_DIM)[None, :], "ssd_norm_g": ssd_norm_g[l][None, :],
        "conv_m_w": conv_m_w[l], "conv_m_b": conv_m_b[l][None, :],
        "w_mq": w_mq[l].astype(BF16), "w_mk": w_mk[l].astype(BF16),
        "b_i_row": _pad_row(b_igate[l], GATE_I0), "b_f_row": _pad_row(b_fgate[l], GATE_I0),
        "mlstm_norm_g": mlstm_norm_g[l].reshape(1, d_m),
        "w_out": w_out[l].astype(BF16), "ln1_g": ln1_g[l][None, :], "ln1_b": ln1_b[l][None, :],
        "peer_wqT": peer_wq[l].T.astype(BF16), "peer_keys": peer_keys[l].astype(BF16),
        "peer_u": peer_u[l].astype(BF16),
        "peer_vT": peer_v[l].astype(BF16).T,
        "ln2_g_col": ln2_g[l][:, None], "ln2_b_col": ln2_b[l][:, None],
        "tril": tril, "ssd_expand": ssd_expand, "mlstm_expand": mlstm_expand,
    }
    return p, widths


def _tile_sizes(m):
    tm = 256 if m % 256 == 0 else LANES
    tb = 512 if m % 512 == 0 else LANES
    return tm, tb


def _peer_block(x1t, p, alpha):
    m = x1t.shape[1]
    tm, tb = _tile_sizes(m)
    s1, s2, st, blist = _peer_topk(x1t, p, tm)
    return _peer_dense(x1t, s1, s2, st, blist, p, alpha, tb)


def _layer_prompt(x, p, widths, alpha):
    bsz, seq, d = x.shape
    assert seq % CHUNK == 0 and seq >= CONV_K - 1
    x2 = x.reshape(bsz * seq, d)
    tm, _ = _tile_sizes(bsz * seq)
    z, xbc, u, v, o, gates = _inproj(x2, p["w_cat"], widths, tm)
    y_s, h_t = _ssd_chunk(xbc, z, gates, p, bsz, seq)
    h_m, c_new, n_new, m_new = _mlstm_chunk(u, v, o, gates, p, bsz, seq)
    x1t = _outproj(y_s, h_m, x2, p, alpha, 2 * tm if (bsz * seq) % (2 * tm) == 0 else tm)
    y = _peer_block(x1t, p, alpha).reshape(bsz, seq, d)
    n_mh = c_new.shape[1]
    d_ssd = z.shape[1]
    n_sh = d_ssd // SSD_HEAD_DIM
    hg = n_sh // SSD_GROUPS
    new_ssd = h_t.reshape(bsz, SSD_GROUPS, SSD_STATE, hg, SSD_HEAD_DIM).transpose(0, 1, 3, 4, 2)
    new_ssd = new_ssd.reshape(bsz, n_sh, SSD_HEAD_DIM, SSD_STATE)
    states = (new_ssd,
              xbc.reshape(bsz, seq, -1)[:, seq - (CONV_K - 1):, :],
              c_new,
              n_new[:, :, 0, :],
              m_new[:, 0, GATE_I0:GATE_I0 + n_mh],
              u.reshape(bsz, seq, -1)[:, seq - (CONV_K - 1):, :])
    return y, states


def _layer_sample(x, st, p, widths, alpha):
    st_ssd, st_ssd_conv, st_c, st_n, st_m, st_mconv = st
    nb, seq, d = x.shape
    assert seq == 1 and nb % LANES == 0
    x2 = x.reshape(nb, d)
    z, xbc, u, v, o, gates = _inproj(x2, p["w_cat"], widths, LANES)
    y_s, new_ssd, new_ssd_conv = _ssd_step(xbc, z, gates, st_ssd_conv, st_ssd, p)
    n_mh = st_c.shape[1]
    m_row = jnp.pad(st_m.astype(F32), ((0, 0), (GATE_I0, LANES - GATE_I0 - n_mh)))[:, None, :]
    h_m, c_new, n_new, m_new, new_mconv = _mlstm_step(u, v, o, gates, st_mconv, st_c, st_n, m_row, p)
    x1t = _outproj(y_s, h_m, x2, p, alpha, LANES)
    y = _peer_block(x1t, p, alpha).reshape(nb, seq, d)
    states = (new_ssd, new_ssd_conv, c_new, n_new, m_new[:, 0, GATE_I0:GATE_I0 + n_mh], new_mconv)
    return y, states


def kernel(x_prompt, x_sample, state_ssd, state_ssd_conv, state_mlstm_c, state_mlstm_n, state_mlstm_m,
           state_mlstm_conv, w_in, conv_ssd_w, conv_ssd_b, dt_bias, a_log, d_skip, ssd_norm_g, conv_m_w,
           conv_m_b, w_mq, w_mk, b_igate, b_fgate, mlstm_norm_g, w_out, ln1_g, ln1_b, peer_wq, peer_keys,
           peer_u, peer_v, ln2_g, ln2_b):
    depth = w_in.shape[0]
    alpha = (2.0 * depth) ** 0.25
    states = (state_ssd, state_ssd_conv, state_mlstm_c, state_mlstm_n, state_mlstm_m, state_mlstm_conv)
    yp, ys = x_prompt, x_sample
    p_list, s_list = [], []
    for l in range(depth):
        p, widths = _prep_layer(l, w_in, conv_ssd_w, conv_ssd_b, dt_bias, a_log, d_skip, ssd_norm_g, conv_m_w,
                                conv_m_b, w_mq, w_mk, b_igate, b_fgate, mlstm_norm_g, w_out, ln1_g, ln1_b,
                                peer_wq, peer_keys, peer_u, peer_v, ln2_g, ln2_b)
        yp, p_new = _layer_prompt(yp, p, widths, alpha)
        ys, s_new = _layer_sample(ys, tuple(s[l] for s in states), p, widths, alpha)
        p_list.append(p_new)
        s_list.append(s_new)
    p_out = tuple(jnp.stack([pn[i] for pn in p_list]) for i in range(6))
    s_out = tuple(jnp.stack([sn[i] for sn in s_list]) for i in range(6))
    return (yp, ys) + p_out + s_out
```

```python
import functools
import math

import jax
import jax.numpy as jnp
from jax import lax
from jax.experimental import pallas as pl
from jax.experimental.pallas import tpu as pltpu

F32 = jnp.float32
BF16 = jnp.bfloat16

LANES = 128
SUBLANES = 8
CHUNK = 128
CONV_K = 4
LN_EPS = 1e-5
VMEM_LIMIT = 58 * 1024 * 1024

SSD_HEAD_DIM = 64
SSD_STATE = 128
SSD_GROUPS = 2
PEER_TOPK = 16
GATE_I0 = 16
N_STATS = 4
PEER_CHUNK = 2048
STEP_SEQS = 4
OUTPROJ_ROWS = 256


def _cparams(*sem):
    return pltpu.CompilerParams(dimension_semantics=sem, vmem_limit_bytes=VMEM_LIMIT)


def _dot(a, b):
    return jnp.dot(a, b, preferred_element_type=F32)


def _dot_nt(a, b):
    return lax.dot_general(a, b, (((1,), (1,)), ((), ())), preferred_element_type=F32)


def _split3(x):
    h = x.astype(BF16)
    r = x - h.astype(F32)
    m = r.astype(BF16)
    lo = (r - m.astype(F32)).astype(BF16)
    return h, m, lo


def _dot_sel_l(sel_bf16, x):
    h, m, lo = _split3(x)
    return _dot(sel_bf16, h) + _dot(sel_bf16, m) + _dot(sel_bf16, lo)


def _dot_sel_r(x, sel_bf16):
    h, m, lo = _split3(x)
    return _dot(h, sel_bf16) + _dot(m, sel_bf16) + _dot(lo, sel_bf16)


def _sigmoid(x):
    return 1.0 / (1.0 + jnp.exp(-x))


def _silu(x):
    return x * _sigmoid(x)


def _softplus(x):
    return jnp.maximum(x, 0.0) + jnp.log1p(jnp.exp(-jnp.abs(x)))


def _log_sigmoid(x):
    return -_softplus(-x)


def _gelu_exact(x):
    return 0.5 * x * (1.0 + lax.erf(x * (1.0 / math.sqrt(2.0))))


def _inproj_kernel(x_ref, w_ref, *out_refs):
    xb = x_ref[...].astype(BF16)
    off = 0
    for ref in out_refs:
        n = ref.shape[-1]
        ref[...] = _dot(xb, w_ref[:, off:off + n])
        off += n


def _inproj(x2, w_cat, widths, tm):
    m, d = x2.shape
    n_all = w_cat.shape[1]
    return pl.pallas_call(
        _inproj_kernel,
        grid=(m // tm,),
        in_specs=[pl.BlockSpec((tm, d), lambda i: (i, 0)),
                  pl.BlockSpec((d, n_all), lambda i: (0, 0))],
        out_specs=[pl.BlockSpec((tm, n), lambda i: (i, 0)) for n in widths],
        out_shape=[jax.ShapeDtypeStruct((m, n), F32) for n in widths],
        compiler_params=_cparams("arbitrary"),
        name="inproj",
    )(x2, w_cat)


def _ssd_chunk_kernel(xbc_ref, z_ref, g_ref, cw_ref, cb_ref, dtb_ref, alog_ref, dsk_ref, ng_ref,
                      tril_ref, exp_ref, y_ref, hT_out_ref, xpad, hT, ybuf, *, nc, d_ssd, n_heads):
    c = pl.program_id(1)
    cd = xpad.shape[1]
    hg = n_heads // SSD_GROUPS
    gw = d_ssd // SSD_GROUPS

    @pl.when(c == 0)
    def _():
        xpad[0:SUBLANES, :] = jnp.zeros((SUBLANES, cd), F32)
        hT[...] = jnp.zeros_like(hT)

    xpad[SUBLANES:SUBLANES + CHUNK, :] = xbc_ref[...]
    conv = cb_ref[...]
    for j in range(CONV_K):
        r0 = SUBLANES - (CONV_K - 1) + j
        conv = conv + cw_ref[j:j + 1, :] * xpad[r0:r0 + CHUNK, :]
    xpad[0:SUBLANES, :] = xpad[CHUNK:CHUNK + SUBLANES, :]
    xc = _silu(conv)
    xs = xc[:, 0:d_ssd]

    lane = lax.broadcasted_iota(jnp.int32, (1, LANES), 1)
    a_row = jnp.where(lane < n_heads, -jnp.exp(alog_ref[...]), 0.0)
    dt = _softplus(g_ref[:, 0:LANES] + dtb_ref[...])
    acs = _dot_sel_l(tril_ref[...], dt * a_row)
    dt_t = dt.T
    acs_t = acs.T
    eacs = jnp.exp(acs)
    dec = jnp.exp(acs[CHUNK - 1:CHUNK, :] - acs) * dt
    both = _dot_sel_r(jnp.concatenate([eacs, dec], axis=0), exp_ref[...])
    eacs_full = both[0:CHUNK]
    dec_full = both[CHUNK:2 * CHUNK]

    row = lax.broadcasted_iota(jnp.int32, (CHUNK, CHUNK), 0)
    col = lax.broadcasted_iota(jnp.int32, (CHUNK, CHUNK), 1)
    causal = row >= col

    gs = range(SSD_GROUPS)
    b_g = [xc[:, d_ssd + g * SSD_STATE:d_ssd + (g + 1) * SSD_STATE] for g in gs]
    c_g = [xc[:, d_ssd + (SSD_GROUPS + g) * SSD_STATE:d_ssd + (SSD_GROUPS + g + 1) * SSD_STATE] for g in gs]
    b_gb = [x.astype(BF16) for x in b_g]
    c_gb = [x.astype(BF16) for x in c_g]
    cb = [_dot_nt(c_gb[g], b_gb[g]) for g in gs]
    h_old = [hT[g] for g in gs]
    y_inter = jnp.concatenate([_dot(c_gb[g], h_old[g].astype(BF16)) for g in gs], axis=1)
    for h in range(n_heads):
        diff = acs[:, h:h + 1] - acs_t[h:h + 1, :]
        lmat = jnp.exp(jnp.where(causal, diff, -jnp.inf))
        mm = cb[h // hg] * lmat * dt_t[h:h + 1, :]
        x_h = xs[:, h * SSD_HEAD_DIM:(h + 1) * SSD_HEAD_DIM]
        ybuf[:, h * SSD_HEAD_DIM:(h + 1) * SSD_HEAD_DIM] = _dot(mm.astype(BF16), x_h.astype(BF16))
    for g in gs:
        gsl = slice(g * gw, (g + 1) * gw)
        xd = xs[:, gsl] * dec_full[:, gsl]
        hT[g] = h_old[g] * eacs_full[CHUNK - 1:CHUNK, gsl] + _dot(b_g[g].T.astype(BF16), xd.astype(BF16))

    y = ybuf[...] + y_inter * eacs_full + dsk_ref[...] * xs
    y = y * _silu(z_ref[...])
    y = y * lax.rsqrt(jnp.mean(y * y, axis=-1, keepdims=True) + LN_EPS) * ng_ref[...]
    y_ref[...] = y.astype(y_ref.dtype)

    @pl.when(c == nc - 1)
    def _():
        hT_out_ref[0] = hT[...]


def _ssd_chunk(xbc, z, gates, p, bsz, seq):
    nc = seq // CHUNK
    cd = xbc.shape[1]
    d_ssd = z.shape[1]
    n_heads = d_ssd // SSD_HEAD_DIM
    gw = d_ssd // SSD_GROUPS
    tok = lambda b, c: (b * nc + c, 0)
    cst = lambda b, c: (0, 0)
    return pl.pallas_call(
        functools.partial(_ssd_chunk_kernel, nc=nc, d_ssd=d_ssd, n_heads=n_heads),
        grid=(bsz, nc),
        in_specs=[pl.BlockSpec((CHUNK, cd), tok),
                  pl.BlockSpec((CHUNK, d_ssd), tok),
                  pl.BlockSpec((CHUNK, 2 * LANES), tok),
                  pl.BlockSpec((CONV_K, cd), cst),
                  pl.BlockSpec((1, cd), cst),
                  pl.BlockSpec((1, LANES), cst),
                  pl.BlockSpec((1, LANES), cst),
                  pl.BlockSpec((1, d_ssd), cst),
                  pl.BlockSpec((1, d_ssd), cst),
                  pl.BlockSpec((CHUNK, CHUNK), cst),
                  pl.BlockSpec((LANES, d_ssd), cst)],
        out_specs=[pl.BlockSpec((CHUNK, d_ssd), tok),
                   pl.BlockSpec((1, SSD_GROUPS, SSD_STATE, gw), lambda b, c: (b, 0, 0, 0))],
        out_shape=[jax.ShapeDtypeStruct((bsz * seq, d_ssd), BF16),
                   jax.ShapeDtypeStruct((bsz, SSD_GROUPS, SSD_STATE, gw), F32)],
        scratch_shapes=[pltpu.VMEM((CHUNK + SUBLANES, cd), F32),
                        pltpu.VMEM((SSD_GROUPS, SSD_STATE, gw), F32),
                        pltpu.VMEM((CHUNK, d_ssd), F32)],
        compiler_params=_cparams("arbitrary", "arbitrary"),
        name="ssd_chunk",
    )(xbc, z, gates, p["conv_ssd_w"], p["conv_ssd_b"], p["dt_bias_row"], p["a_log_row"],
      p["d_skip_row"], p["ssd_norm_g"], p["tril"], p["ssd_expand"])


def _mlstm_chunk_kernel(u_ref, v_ref, o_ref, g_ref, cw_ref, cb_ref, wq_ref, wk_ref, bi_ref, bf_ref, ng_ref,
                        tril_ref, y_ref, c_out_ref, n_out_ref, m_out_ref, upad, c_st, n_st, m_st,
                        *, nc, n_heads, dk):
    c = pl.program_id(1)
    dm = upad.shape[1]

    @pl.when(c == 0)
    def _():
        upad[0:SUBLANES, :] = jnp.zeros((SUBLANES, dm), F32)
        c_st[...] = jnp.zeros_like(c_st)
        n_st[...] = jnp.zeros_like(n_st)
        m_st[...] = jnp.zeros_like(m_st)

    upad[SUBLANES:SUBLANES + CHUNK, :] = u_ref[...]
    conv = cb_ref[...]
    for j in range(CONV_K):
        r0 = SUBLANES - (CONV_K - 1) + j
        conv = conv + cw_ref[j:j + 1, :] * upad[r0:r0 + CHUNK, :]
    upad[0:SUBLANES, :] = upad[CHUNK:CHUNK + SUBLANES, :]
    uc = _silu(conv).astype(BF16)

    logi = g_ref[:, 0:LANES] + bi_ref[...]
    logf = _log_sigmoid(g_ref[:, LANES:2 * LANES] + bf_ref[...])
    bcs = _dot_sel_l(tril_ref[...], logf)
    m_prev = m_st[0:1, :]
    b_end = bcs[CHUNK - 1:CHUNK, :]
    g_all = b_end - bcs + logi
    m_new = jnp.maximum(b_end + m_prev, jnp.max(g_all, axis=0, keepdims=True))
    w_old = jnp.exp(b_end + m_prev - m_new)
    w_s_all = jnp.exp(g_all - m_new)
    inter_all = bcs + m_prev
    bcs_t = bcs.T
    logi_t = logi.T

    row = lax.broadcasted_iota(jnp.int32, (CHUNK, CHUNK), 0)
    col = lax.broadcasted_iota(jnp.int32, (CHUNK, CHUNK), 1)
    causal = row >= col
    scale = dk ** -0.5

    hs = range(n_heads)
    sls = [slice(h * dk, (h + 1) * dk) for h in hs]
    gls = [GATE_I0 + h for h in hs]
    q = [_dot(uc[:, sls[h]], wq_ref[h]) for h in hs]
    k = [_dot(uc[:, sls[h]], wk_ref[h]) * scale for h in hs]
    qb = [x.astype(BF16) for x in q]
    kb = [x.astype(BF16) for x in k]
    vb = [v_ref[:, sls[h]].astype(BF16) for h in hs]
    dmat = [jnp.where(causal, bcs[:, g:g + 1] - bcs_t[g:g + 1, :] + logi_t[g:g + 1, :], -jnp.inf) for g in gls]
    inter = [inter_all[:, g:g + 1] for g in gls]
    m_t = [jnp.maximum(inter[h], jnp.max(dmat[h], axis=1, keepdims=True)) for h in hs]
    w_inter = [jnp.exp(inter[h] - m_t[h]) for h in hs]
    qk = [_dot_nt(qb[h], kb[h]) for h in hs]
    c_old = [c_st[h] for h in hs]
    n_old = [n_st[h][0:1, :] for h in hs]
    qc = [_dot(qb[h], c_old[h].astype(BF16)) for h in hs]
    att = [qk[h] * jnp.exp(dmat[h] - m_t[h]) for h in hs]
    av = [_dot(att[h].astype(BF16), vb[h]) for h in hs]
    kw = [k[h] * w_s_all[:, gls[h]:gls[h] + 1] for h in hs]
    kv = [_dot(kw[h].T.astype(BF16), vb[h]) for h in hs]
    num = [av[h] + w_inter[h] * qc[h] for h in hs]
    den = [jnp.sum(att[h], axis=1, keepdims=True) + w_inter[h] * jnp.sum(q[h] * n_old[h], axis=1, keepdims=True)
           for h in hs]
    hout = [num[h] / jnp.maximum(jnp.abs(den[h]), jnp.exp(-m_t[h])) for h in hs]
    mu = [jnp.mean(x, axis=-1, keepdims=True) for x in hout]
    cen = [hout[h] - mu[h] for h in hs]
    var = [jnp.mean(jnp.square(x), axis=-1, keepdims=True) for x in cen]
    for h in hs:
        sl = sls[h]
        hn = cen[h] * lax.rsqrt(var[h] + LN_EPS) * ng_ref[:, sl]
        y_ref[:, sl] = (_sigmoid(o_ref[:, sl]) * hn).astype(y_ref.dtype)
        wo = w_old[:, gls[h]:gls[h] + 1]
        c_st[h] = wo * c_old[h] + kv[h]
        n_st[h] = jnp.broadcast_to(wo * n_old[h] + jnp.sum(kw[h], axis=0, keepdims=True), (SUBLANES, dk))

    m_st[...] = jnp.broadcast_to(m_new, (SUBLANES, LANES))

    @pl.when(c == nc - 1)
    def _():
        c_out_ref[0] = c_st[...]
        n_out_ref[0] = n_st[...]
        m_out_ref[0] = m_st[...]


def _mlstm_chunk(u, v, o, gates, p, bsz, seq):
    nc = seq // CHUNK
    dm = u.shape[1]
    n_heads, dk, _ = p["w_mq"].shape
    tok = lambda b, c: (b * nc + c, 0)
    cst = lambda b, c: (0, 0)
    cst3 = lambda b, c: (0, 0, 0)
    return pl.pallas_call(
        functools.partial(_mlstm_chunk_kernel, nc=nc, n_heads=n_heads, dk=dk),
        grid=(bsz, nc),
        in_specs=[pl.BlockSpec((CHUNK, dm), tok),
                  pl.BlockSpec((CHUNK, dm), tok),
                  pl.BlockSpec((CHUNK, dm), tok),
                  pl.BlockSpec((CHUNK, 2 * LANES), tok),
                  pl.BlockSpec((CONV_K, dm), cst),
                  pl.BlockSpec((1, dm), cst),
                  pl.BlockSpec((n_heads, dk, dk), cst3),
                  pl.BlockSpec((n_heads, dk, dk), cst3),
                  pl.BlockSpec((1, LANES), cst),
                  pl.BlockSpec((1, LANES), cst),
                  pl.BlockSpec((1, dm), cst),
                  pl.BlockSpec((CHUNK, CHUNK), cst)],
        out_specs=[pl.BlockSpec((CHUNK, dm), tok),
                   pl.BlockSpec((1, n_heads, dk, dk), lambda b, c: (b, 0, 0, 0)),
                   pl.BlockSpec((1, n_heads, SUBLANES, dk), lambda b, c: (b, 0, 0, 0)),
                   pl.BlockSpec((1, SUBLANES, LANES), lambda b, c: (b, 0, 0))],
        out_shape=[jax.ShapeDtypeStruct((bsz * seq, dm), BF16),
                   jax.ShapeDtypeStruct((bsz, n_heads, dk, dk), F32),
                   jax.ShapeDtypeStruct((bsz, n_heads, SUBLANES, dk), F32),
                   jax.ShapeDtypeStruct((bsz, SUBLANES, LANES), F32)],
        scratch_shapes=[pltpu.VMEM((CHUNK + SUBLANES, dm), F32),
                        pltpu.VMEM((n_heads, dk, dk), F32),
                        pltpu.VMEM((n_heads, SUBLANES, dk), F32),
                        pltpu.VMEM((SUBLANES, LANES), F32)],
        compiler_params=_cparams("arbitrary", "arbitrary"),
        name="mlstm_chunk",
    )(u, v, o, gates, p["conv_m_w"], p["conv_m_b"], p["w_mq"], p["w_mk"], p["b_i_row"], p["b_f_row"],
      p["mlstm_norm_g"], p["tril"])


def _col_bcast(row, n_rows):
    return jnp.broadcast_to(row, (LANES, row.shape[1])).T[0:n_rows, :]


def _per_sequence(one_sequence, n_seq_in, n_seq_out):
    def kernel(*refs, **static):
        seq_in, shared, seq_out = refs[:n_seq_in], refs[n_seq_in:len(refs) - n_seq_out], refs[len(refs) - n_seq_out:]
        for s in range(seq_in[0].shape[0]):
            one = slice(s, s + 1)
            one_sequence(*(r.at[one] for r in seq_in), *shared, *(r.at[one] for r in seq_out), **static)
    return kernel


def _ssd_step_one(xbc_ref, z_ref, g_ref, buf_ref, st_ref, cw_ref, cb_ref, dtb_ref, alog_ref, dsk_ref, ng_ref,
                  exp_ref, y_ref, st_out_ref, buf_out_ref, *, d_ssd, n_heads):
    gw = d_ssd // SSD_GROUPS
    x_row = xbc_ref[0]
    buf_out_ref[0, 0:CONV_K - 2, :] = buf_ref[0, 1:CONV_K - 1, :]
    buf_out_ref[0, CONV_K - 2:CONV_K - 1, :] = x_row
    conv = cb_ref[...] + cw_ref[CONV_K - 1:CONV_K, :] * x_row
    for j in range(CONV_K - 1):
        conv = conv + cw_ref[j:j + 1, :] * buf_ref[0, j:j + 1, :]
    xc = _silu(conv)
    xs = xc[:, 0:d_ssd]

    lane = lax.broadcasted_iota(jnp.int32, (1, LANES), 1)
    a_row = jnp.where(lane < n_heads, -jnp.exp(alog_ref[...]), 0.0)
    dt = _softplus(g_ref[0][:, 0:LANES] + dtb_ref[...])
    dec = jnp.exp(dt * a_row)
    both = _dot_sel_r(jnp.concatenate([jnp.broadcast_to(dt, (SUBLANES, LANES)),
                                       jnp.broadcast_to(dec, (SUBLANES, LANES))], axis=0), exp_ref[...])
    dt_full = both[0:1]
    dec_full = both[SUBLANES:SUBLANES + 1]
    dtx = dt_full * xs
    dtx_col = _col_bcast(dtx, d_ssd)
    dec_col = _col_bcast(dec_full, d_ssd)

    y_parts = []
    for g in range(SSD_GROUPS):
        b_g = xc[:, d_ssd + g * SSD_STATE:d_ssd + (g + 1) * SSD_STATE]
        c_g = xc[:, d_ssd + (SSD_GROUPS + g) * SSD_STATE:d_ssd + (SSD_GROUPS + g + 1) * SSD_STATE]
        h_g = st_ref[0, g * gw:(g + 1) * gw, :]
        cb = jnp.sum(c_g * b_g, axis=-1, keepdims=True)
        c8 = jnp.broadcast_to(c_g, (SUBLANES, SSD_STATE)).astype(BF16)
        y_int = _dot_nt(c8, h_g.astype(BF16))[0:1]
        sl = slice(g * gw, (g + 1) * gw)
        y_parts.append(cb * dtx[:, sl] + dec_full[:, sl] * y_int)
        st_out_ref[0, sl, :] = h_g * dec_col[sl, :] + dtx_col[sl, :] * b_g
    y = jnp.concatenate(y_parts, axis=1) + dsk_ref[...] * xs
    y = y * _silu(z_ref[0])
    y = y * lax.rsqrt(jnp.mean(y * y, axis=-1, keepdims=True) + LN_EPS) * ng_ref[...]
    y_ref[0] = y.astype(y_ref.dtype)


def _ssd_step(xbc, z, gates, buf, state, p):
    nb, cd = xbc.shape
    d_ssd = z.shape[1]
    n_heads = d_ssd // SSD_HEAD_DIM
    row3 = lambda b: (b, 0, 0)
    cst = lambda b: (0, 0)
    st2 = state.reshape(nb, d_ssd, SSD_STATE)
    sb = STEP_SEQS
    assert nb % sb == 0
    y, st_new, buf_new = pl.pallas_call(
        functools.partial(_per_sequence(_ssd_step_one, 5, 3), d_ssd=d_ssd, n_heads=n_heads),
        grid=(nb // sb,),
        in_specs=[pl.BlockSpec((sb, 1, cd), row3),
                  pl.BlockSpec((sb, 1, d_ssd), row3),
                  pl.BlockSpec((sb, 1, 2 * LANES), row3),
                  pl.BlockSpec((sb, CONV_K - 1, cd), row3),
                  pl.BlockSpec((sb, d_ssd, SSD_STATE), row3),
                  pl.BlockSpec((CONV_K, cd), cst),
                  pl.BlockSpec((1, cd), cst),
                  pl.BlockSpec((1, LANES), cst),
                  pl.BlockSpec((1, LANES), cst),
                  pl.BlockSpec((1, d_ssd), cst),
                  pl.BlockSpec((1, d_ssd), cst),
                  pl.BlockSpec((LANES, d_ssd), cst)],
        out_specs=[pl.BlockSpec((sb, 1, d_ssd), row3),
                   pl.BlockSpec((sb, d_ssd, SSD_STATE), row3),
                   pl.BlockSpec((sb, CONV_K - 1, cd), row3)],
        out_shape=[jax.ShapeDtypeStruct((nb, 1, d_ssd), BF16),
                   jax.ShapeDtypeStruct((nb, d_ssd, SSD_STATE), F32),
                   jax.ShapeDtypeStruct((nb, CONV_K - 1, cd), F32)],
        compiler_params=_cparams("arbitrary"),
        name="ssd_step",
    )(xbc.reshape(nb, 1, cd), z.reshape(nb, 1, d_ssd), gates.reshape(nb, 1, 2 * LANES), buf, st2,
      p["conv_ssd_w"], p["conv_ssd_b"], p["dt_bias_row"], p["a_log_row"], p["d_skip_row"], p["ssd_norm_g"],
      p["ssd_expand"])
    return y.reshape(nb, d_ssd), st_new.reshape(state.shape), buf_new


def _mlstm_step_one(u_ref, v_ref, o_ref, g_ref, buf_ref, c_ref, n_ref, m_ref, cw_ref, cb_ref, wq_ref, wk_ref,
                    bi_ref, bf_ref, ng_ref, hexp_ref, y_ref, c_out_ref, n_out_ref, m_out_ref, buf_out_ref,
                    *, n_heads, dk):
    buf_out_ref[0, 0:CONV_K - 2, :] = buf_ref[0, 1:CONV_K - 1, :]
    buf_out_ref[0, CONV_K - 2:CONV_K - 1, :] = u_ref[0]
    conv = cb_ref[...] + cw_ref[CONV_K - 1:CONV_K, :] * u_ref[0]
    for j in range(CONV_K - 1):
        conv = conv + cw_ref[j:j + 1, :] * buf_ref[0, j:j + 1, :]
    uc = jnp.broadcast_to(_silu(conv), (SUBLANES, conv.shape[1])).astype(BF16)

    gt = g_ref[0]
    logi = gt[:, 0:LANES] + bi_ref[...]
    logf = _log_sigmoid(gt[:, LANES:2 * LANES] + bf_ref[...])
    m_prev = m_ref[0]
    inter = logf + m_prev
    m_t = jnp.maximum(inter, logi)
    w_inter = jnp.exp(inter - m_t)
    w_in = jnp.exp(logi - m_t)
    floor = jnp.exp(-m_t)
    rows = jnp.concatenate([w_inter, w_in, floor, jnp.zeros((SUBLANES - 3, LANES), F32)], axis=0)
    full = _dot_sel_r(rows, hexp_ref[...])
    scale = dk ** -0.5

    for h in range(n_heads):
        sl = slice(h * dk, (h + 1) * dk)
        q = _dot(uc[:, sl], wq_ref[h])[0:1]
        k = _dot(uc[:, sl], wk_ref[h])[0:1] * scale
        v_h = v_ref[0][:, sl]
        wi = full[0:1, sl]
        ws = full[1:2, sl]
        fl = full[2:3, sl]
        c_h = c_ref[0, h]
        n_h = n_ref[0, h:h + 1, :]
        att = jnp.sum(q * k, axis=-1, keepdims=True) * ws
        q8 = jnp.broadcast_to(q, (SUBLANES, dk)).astype(BF16)
        num = att * v_h + wi * _dot(q8, c_h.astype(BF16))[0:1]
        den = att + wi * jnp.sum(q * n_h, axis=-1, keepdims=True)
        hout = num / jnp.maximum(jnp.abs(den), fl)
        mu = jnp.mean(hout, axis=-1, keepdims=True)
        var = jnp.mean(jnp.square(hout - mu), axis=-1, keepdims=True)
        hn = (hout - mu) * lax.rsqrt(var + LN_EPS) * ng_ref[:, sl]
        y_ref[0, :, sl] = (_sigmoid(o_ref[0][:, sl]) * hn).astype(y_ref.dtype)
        kw = k * ws
        kw_col = _col_bcast(kw, dk)
        wi_full = jnp.broadcast_to(wi, (dk, dk))
        for half in range(dk // LANES):
            hs = slice(half * LANES, (half + 1) * LANES)
            c_out_ref[0, h, :, hs] = wi_full[:, hs] * c_h[:, hs] + kw_col * v_h[:, hs]
        n_out_ref[0, h:h + 1, :] = wi * n_h + kw
    m_out_ref[0] = m_t


def _mlstm_step(u, v, o, gates, buf, c_st, n_st, m_row, p):
    nb, dm = u.shape
    n_heads, dk, _ = p["w_mq"].shape
    row3 = lambda b: (b, 0, 0)
    cst = lambda b: (0, 0)
    cst3 = lambda b: (0, 0, 0)
    r3 = lambda a: a.reshape(nb, 1, a.shape[-1])
    sb = STEP_SEQS
    assert nb % sb == 0
    y, c_new, n_new, m_new, buf_new = pl.pallas_call(
        functools.partial(_per_sequence(_mlstm_step_one, 8, 5), n_heads=n_heads, dk=dk),
        grid=(nb // sb,),
        in_specs=[pl.BlockSpec((sb, 1, dm), row3),
                  pl.BlockSpec((sb, 1, dm), row3),
                  pl.BlockSpec((sb, 1, dm), row3),
                  pl.BlockSpec((sb, 1, 2 * LANES), row3),
                  pl.BlockSpec((sb, CONV_K - 1, dm), row3),
                  pl.BlockSpec((sb, n_heads, dk, dk), lambda b: (b, 0, 0, 0)),
                  pl.BlockSpec((sb, n_heads, dk), row3),
                  pl.BlockSpec((sb, 1, LANES), row3),
                  pl.BlockSpec((CONV_K, dm), cst),
                  pl.BlockSpec((1, dm), cst),
                  pl.BlockSpec((n_heads, dk, dk), cst3),
                  pl.BlockSpec((n_heads, dk, dk), cst3),
                  pl.BlockSpec((1, LANES), cst),
                  pl.BlockSpec((1, LANES), cst),
                  pl.BlockSpec((1, dm), cst),
                  pl.BlockSpec((LANES, dm), cst)],
        out_specs=[pl.BlockSpec((sb, 1, dm), row3),
                   pl.BlockSpec((sb, n_heads, dk, dk), lambda b: (b, 0, 0, 0)),
                   pl.BlockSpec((sb, n_heads, dk), row3),
                   pl.BlockSpec((sb, 1, LANES), row3),
                   pl.BlockSpec((sb, CONV_K - 1, dm), row3)],
        out_shape=[jax.ShapeDtypeStruct((nb, 1, dm), BF16),
                   jax.ShapeDtypeStruct(c_st.shape, F32),
                   jax.ShapeDtypeStruct(n_st.shape, F32),
                   jax.ShapeDtypeStruct((nb, 1, LANES), F32),
                   jax.ShapeDtypeStruct((nb, CONV_K - 1, dm), F32)],
        compiler_params=_cparams("arbitrary"),
        name="mlstm_step",
    )(r3(u), r3(v), r3(o), r3(gates), buf, c_st, n_st, m_row, p["conv_m_w"], p["conv_m_b"], p["w_mq"],
      p["w_mk"], p["b_i_row"], p["b_f_row"], p["mlstm_norm_g"], p["mlstm_expand"])
    return y.reshape(nb, dm), c_new, n_new, m_new, buf_new


def _outproj_kernel(ys_ref, hm_ref, x_ref, w_ref, g_ref, b_ref, x1t_ref, *, alpha):
    d_ssd = ys_ref.shape[1]
    tm = ys_ref.shape[0]
    subs = [slice(r0, min(r0 + OUTPROJ_ROWS, tm)) for r0 in range(0, tm, OUTPROJ_ROWS)]
    mix = [_dot(ys_ref[rs, :], w_ref[0:d_ssd, :]) + _dot(hm_ref[rs, :], w_ref[d_ssd:, :]) for rs in subs]
    for rs, mx in zip(subs, mix):
        r = alpha * x_ref[rs, :] + mx
        mu = jnp.mean(r, axis=-1, keepdims=True)
        var = jnp.mean(jnp.square(r - mu), axis=-1, keepdims=True)
        x1 = (r - mu) * lax.rsqrt(var + LN_EPS) * g_ref[...] + b_ref[...]
        x1t_ref[:, rs] = x1.T


def _outproj(ys, hm, x2, p, alpha, tm):
    m, d = x2.shape
    d_ssd, d_m = ys.shape[1], hm.shape[1]
    return pl.pallas_call(
        functools.partial(_outproj_kernel, alpha=alpha),
        grid=(m // tm,),
        in_specs=[pl.BlockSpec((tm, d_ssd), lambda i: (i, 0)),
                  pl.BlockSpec((tm, d_m), lambda i: (i, 0)),
                  pl.BlockSpec((tm, d), lambda i: (i, 0)),
                  pl.BlockSpec((d_ssd + d_m, d), lambda i: (0, 0)),
                  pl.BlockSpec((1, d), lambda i: (0, 0)),
                  pl.BlockSpec((1, d), lambda i: (0, 0))],
        out_specs=pl.BlockSpec((d, tm), lambda i: (0, i)),
        out_shape=jax.ShapeDtypeStruct((d, m), F32),
        compiler_params=_cparams("arbitrary"),
        name="outproj_ln1",
    )(ys, hm, x2, p["w_out"], p["ln1_g"], p["ln1_b"])


def _oddeven_merge_sort_pairs(n):
    pairs = []
    p = 1
    while p < n:
        k = p
        while k >= 1:
            for j in range(k % p, n - k, 2 * k):
                for i in range(min(k, n - j - k)):
                    if (i + j) // (2 * p) == (i + j + k) // (2 * p):
                        pairs.append((i + j, i + j + k))
            k //= 2
        p *= 2
    return pairs


_SORT_TOPK = _oddeven_merge_sort_pairs(PEER_TOPK)


def _compare_exchange(lst, i, j):
    hi, lo = jnp.maximum(lst[i], lst[j]), jnp.minimum(lst[i], lst[j])
    lst[i], lst[j] = hi, lo


def _bitonic_to_sorted(t):
    d = PEER_TOPK // 2
    while d >= 1:
        for i in range(PEER_TOPK):
            if i & d == 0:
                _compare_exchange(t, i, i + d)
        d //= 2
    return t


def _merge_top(x, y):
    neg = jnp.full_like(x[0], -jnp.inf)
    x = x + [neg] * (PEER_TOPK - len(x))
    y = y + [neg] * (PEER_TOPK - len(y))
    return _bitonic_to_sorted([jnp.maximum(x[k], y[PEER_TOPK - 1 - k]) for k in range(PEER_TOPK)])


def _top_rows(s):
    lst = [s[SUBLANES * g:SUBLANES * (g + 1), :] for g in range(PEER_TOPK)]
    for i, j in _SORT_TOPK:
        _compare_exchange(lst, i, j)
    shift = SUBLANES // 2
    while shift >= 1:
        lst = _merge_top(lst, [pltpu.roll(a, shift, 0) for a in lst])
        shift //= 2
    return lst


def _peer_topk_kernel(x1t_ref, wqt_ref, keys_ref, s1_ref, s2_ref, st_ref, a_scr, thr_ref, q_scr, b_scr, *, n_heads):
    tq = x1t_ref.shape[1]
    q_scr[...] = _dot(wqt_ref[...], x1t_ref[...].astype(BF16))
    half = q_scr.shape[0] // (2 * n_heads)

    for h in range(n_heads):
        for side, (s_ref, top_scr) in enumerate(((s1_ref, a_scr), (s2_ref, b_scr))):
            r0 = (2 * h + side) * half
            s = _dot(keys_ref[h, side], q_scr[r0:r0 + half, :].astype(BF16))
            for t in range(tq // LANES):
                ls = slice(t * LANES, (t + 1) * LANES)
                s_ref[h, t] = s[:, ls]
                top = _top_rows(s[:, ls])
                for k in range(PEER_TOPK):
                    top_scr[k, h:h + 1, ls] = top[k][0:1, :]

    for t in range(tq // LANES):
        ls = slice(t * LANES, (t + 1) * LANES)
        a = [a_scr[k, :, ls] for k in range(PEER_TOPK)]
        b = [b_scr[k, :, ls] for k in range(PEER_TOPK)]
        single = PEER_TOPK // 2
        lists = [[a[i] + b[j] for j in range(PEER_TOPK // (i + 1))] for i in range(single)]
        lists.append([a[i] + b[0] for i in range(single, PEER_TOPK)])
        merged = lists[0]
        for other in lists[1:]:
            merged = _merge_top(merged, other)
        tau = merged[PEER_TOPK - 1]
        top = a[0] + b[0]
        zsum = jnp.zeros_like(tau)
        for cand in (c for lst in lists for c in lst):
            zsum = zsum + jnp.where(cand >= tau, jnp.exp(cand - top), 0.0)
        st_ref[0, :, ls] = tau
        st_ref[1, :, ls] = a[0]
        st_ref[2, :, ls] = b[0]
        st_ref[3, :, ls] = 1.0 / zsum
        for i in range(PEER_TOPK):
            sums = lists[i] if i < single else [lists[single][i - single]]
            thr = jnp.full_like(tau, jnp.inf)
            for k, cand in enumerate(sums):
                thr = jnp.where(cand >= tau, b[k], thr)
            thr_ref[i, :, ls] = thr


def _peer_topk(x1t, p, tq):
    d, m = x1t.shape
    n_heads, _, n_keys, half = p["peer_keys"].shape
    nq = p["peer_wqT"].shape[0]
    assert n_heads == SUBLANES and n_keys == PEER_TOPK * SUBLANES
    keyb = lambda i: (0, i, 0, 0)
    return pl.pallas_call(
        functools.partial(_peer_topk_kernel, n_heads=n_heads),
        grid=(m // tq,),
        in_specs=[pl.BlockSpec((d, tq), lambda i: (0, i)),
                  pl.BlockSpec((nq, d), lambda i: (0, 0)),
                  pl.BlockSpec((n_heads, 2, n_keys, half), lambda i: (0, 0, 0, 0))],
        out_specs=[pl.BlockSpec((n_heads, tq // LANES, n_keys, LANES), keyb),
                   pl.BlockSpec((n_heads, tq // LANES, n_keys, LANES), keyb),
                   pl.BlockSpec((N_STATS, SUBLANES, tq), lambda i: (0, 0, i)),
                   pl.BlockSpec((PEER_TOPK, SUBLANES, tq), lambda i: (0, 0, i)),
                   pl.BlockSpec((PEER_TOPK, SUBLANES, tq), lambda i: (0, 0, i))],
        out_shape=[jax.ShapeDtypeStruct((n_heads, m // LANES, n_keys, LANES), F32),
                   jax.ShapeDtypeStruct((n_heads, m // LANES, n_keys, LANES), F32),
                   jax.ShapeDtypeStruct((N_STATS, SUBLANES, m), F32),
                   jax.ShapeDtypeStruct((PEER_TOPK, SUBLANES, m), F32),
                   jax.ShapeDtypeStruct((PEER_TOPK, SUBLANES, m), F32)],
        scratch_shapes=[pltpu.VMEM((nq, tq), F32),
                        pltpu.VMEM((PEER_TOPK, SUBLANES, tq), F32)],
        compiler_params=_cparams("arbitrary"),
        name="peer_topk",
    )(x1t, p["peer_wqT"], p["peer_keys"])


def _peer_dense_kernel(x1t_ref, s1_ref, s2_ref, st_ref, al_ref, tr_ref, u_ref, vt_ref, g_ref, b_ref, out_ref,
                       xb_scr, e2_scr, c_scr, thr_scr, acc_scr, s_scr, w_scr, thr_r, cr, *, alpha, n_heads):
    j = pl.program_id(1)
    ec = u_ref.shape[0]
    tb = x1t_ref.shape[1]
    nt = tb // LANES
    n_keys = s1_ref.shape[2]
    na = ec // n_keys

    @pl.when(j == 0)
    def _():
        xb_scr[...] = x1t_ref[...].astype(BF16)
        for h in range(n_heads):
            for t in range(nt):
                ls = slice(t * LANES, (t + 1) * LANES)
                e2_scr[h, t] = jnp.exp(s2_ref[h, t] - st_ref[2, h:h + 1, ls])
                s1 = s1_ref[h, t]
                c_scr[h, t] = jnp.exp(s1 - st_ref[1, h:h + 1, ls]) * (0.5 * st_ref[3, h:h + 1, ls])
                thr = jnp.full((n_keys, LANES), jnp.inf, F32)
                for k in range(PEER_TOPK):
                    thr = jnp.where(s1 == al_ref[k, h:h + 1, ls], tr_ref[k, h:h + 1, ls], thr)
                thr_scr[h, t] = thr
        acc_scr[...] = jnp.zeros_like(acc_scr)

    s_scr[:, 0:tb] = _dot(u_ref[...], xb_scr[...])

    g0 = pl.multiple_of(j * na, na)
    for h in range(n_heads):
        for t in range(nt):
            ls = slice(t * LANES, (t + 1) * LANES)
            thr_blk = thr_scr[h, t, pl.ds(g0, na), :]
            c_blk = c_scr[h, t, pl.ds(g0, na), :]
            for al in range(na):
                thr_r[h * na + al, :, ls] = thr_blk[al:al + 1, :]
                cr[h * na + al, :, ls] = c_blk[al:al + 1, :]

    def first_key(al, carry):
        r0 = pl.multiple_of(al * n_keys, n_keys)
        for t in range(nt):
            ls = slice(t * LANES, (t + 1) * LANES)
            gate = jnp.zeros((n_keys, LANES), F32)
            for h in range(n_heads):
                gate = gate + jnp.where(s2_ref[h, t] >= thr_r[h * na + al][:, ls],
                                        e2_scr[h, t] * cr[h * na + al][:, ls], 0.0)
            x = s_scr[pl.ds(r0, n_keys), ls]
            hid = x + x * lax.erf(x * (1.0 / math.sqrt(2.0)))
            w_scr[pl.ds(r0, n_keys), ls] = (gate * hid).astype(BF16)
        return carry

    lax.fori_loop(0, na, first_key, 0)
    acc_scr[...] += _dot(vt_ref[...], w_scr[:, 0:tb])

    @pl.when(j == pl.num_programs(1) - 1)
    def _():
        r = alpha * x1t_ref[...] + acc_scr[...]
        mu = jnp.mean(r, axis=0, keepdims=True)
        var = jnp.mean(jnp.square(r - mu), axis=0, keepdims=True)
        y = (r - mu) * lax.rsqrt(var + LN_EPS) * g_ref[...] + b_ref[...]
        out_ref[...] = y.T


def _peer_dense(x1t, s1, s2, st, alist, thr_rank, p, alpha, tb):
    d, m = x1t.shape
    n_heads, _, n_keys, _ = s1.shape
    n_exp = p["peer_u"].shape[0]
    ec = PEER_CHUNK
    nj = n_exp // ec
    na = ec // n_keys
    assert na % SUBLANES == 0
    nt = tb // LANES
    keyb = lambda i, j: (0, i, 0, 0)
    tokb = lambda i, j: (0, 0, i)
    pitch = tb + LANES
    return pl.pallas_call(
        functools.partial(_peer_dense_kernel, alpha=alpha, n_heads=n_heads),
        grid=(m // tb, nj),
        in_specs=[pl.BlockSpec((d, tb), lambda i, j: (0, i)),
                  pl.BlockSpec((n_heads, nt, n_keys, LANES), keyb),
                  pl.BlockSpec((n_heads, nt, n_keys, LANES), keyb),
                  pl.BlockSpec((N_STATS, SUBLANES, tb), tokb),
                  pl.BlockSpec((PEER_TOPK, SUBLANES, tb), tokb),
                  pl.BlockSpec((PEER_TOPK, SUBLANES, tb), tokb),
                  pl.BlockSpec((ec, d), lambda i, j: (j, 0)),
                  pl.BlockSpec((d, ec), lambda i, j: (0, j)),
                  pl.BlockSpec((d, 1), lambda i, j: (0, 0)),
                  pl.BlockSpec((d, 1), lambda i, j: (0, 0))],
        out_specs=pl.BlockSpec((tb, d), lambda i, j: (i, 0)),
        out_shape=jax.ShapeDtypeStruct((m, d), F32),
        scratch_shapes=[pltpu.VMEM((d, tb), BF16),
                        pltpu.VMEM((n_heads, nt, n_keys, LANES), F32),
                        pltpu.VMEM((n_heads, nt, n_keys, LANES), F32),
                        pltpu.VMEM((n_heads, nt, n_keys, LANES), F32),
                        pltpu.VMEM((d, tb), F32),
                        pltpu.VMEM((ec, pitch), F32),
                        pltpu.VMEM((ec, pitch), BF16),
                        pltpu.VMEM((n_heads * na, 1, tb), F32),
                        pltpu.VMEM((n_heads * na, 1, tb), F32)],
        compiler_params=_cparams("arbitrary", "arbitrary"),
        name="peer_dense",
    )(x1t, s1, s2, st, alist, thr_rank, p["peer_u"], p["peer_vT"], p["ln2_g_col"], p["ln2_b_col"])


def _pad_row(vec, offset):
    return jnp.pad(vec.astype(F32), (offset, LANES - offset - vec.shape[0]))[None, :]


def _prep_layer(l, w_in, conv_ssd_w, conv_ssd_b, dt_bias, a_log, d_skip, ssd_norm_g, conv_m_w, conv_m_b,
                w_mq, w_mk, b_igate, b_fgate, mlstm_norm_g, w_out, ln1_g, ln1_b, peer_wq, peer_keys,
                peer_u, peer_v, ln2_g, ln2_b):
    d_model = w_in.shape[1]
    d_ssd = ssd_norm_g.shape[1]
    cd = conv_ssd_w.shape[2]
    n_sh = dt_bias.shape[1]
    d_m = conv_m_w.shape[2]
    n_mh = w_mq.shape[1]
    dk = w_mq.shape[2]
    assert n_sh <= GATE_I0 and GATE_I0 + n_mh <= LANES and d_ssd == n_sh * SSD_HEAD_DIM
    w = w_in[l]
    offs = [0]
    for n in (d_ssd, cd, n_sh, d_m, d_m, d_m, n_mh, n_mh):
        offs.append(offs[-1] + n)
    z_w, xbc_w, dt_w, u_w, v_w, o_w, i_w, f_w = (w[:, offs[k]:offs[k + 1]] for k in range(8))
    tail = ((0, 0), (0, LANES - GATE_I0 - n_mh))
    gate_a = jnp.pad(jnp.concatenate([jnp.pad(dt_w, ((0, 0), (0, GATE_I0 - n_sh))), i_w], axis=1), tail)
    gate_b = jnp.pad(f_w, ((0, 0), (GATE_I0, LANES - GATE_I0 - n_mh)))
    w_cat = jnp.concatenate([z_w, xbc_w, u_w, v_w, o_w, gate_a, gate_b], axis=1).astype(BF16)
    widths = (d_ssd, cd, d_m, d_m, d_m, 2 * LANES)

    lane = jnp.arange(LANES)[:, None]
    ssd_expand = (lane == (jnp.arange(d_ssd)[None, :] // SSD_HEAD_DIM)).astype(BF16)
    mlstm_expand = (lane == (GATE_I0 + jnp.arange(d_m)[None, :] // dk)).astype(BF16)
    t = jnp.arange(CHUNK)
    tril = (t[:, None] >= t[None, :]).astype(BF16)
    p = {
        "w_cat": w_cat,
        "conv_ssd_w": conv_ssd_w[l], "conv_ssd_b": conv_ssd_b[l][None, :],
        "dt_bias_row": _pad_row(dt_bias[l], 0), "a_log_row": _pad_row(a_log[l], 0),
        "d_skip_row": jnp.repeat(d_skip[l], SSD_HEAD_DIM)[None, :], "ssd_norm_g": ssd_norm_g[l][None, :],
        "conv_m_w": conv_m_w[l], "conv_m_b": conv_m_b[l][None, :],
        "w_mq": w_mq[l].astype(BF16), "w_mk": w_mk[l].astype(BF16),
        "b_i_row": _pad_row(b_igate[l], GATE_I0), "b_f_row": _pad_row(b_fgate[l], GATE_I0),
        "mlstm_norm_g": mlstm_norm_g[l].reshape(1, d_m),
        "w_out": w_out[l].astype(BF16), "ln1_g": ln1_g[l][None, :], "ln1_b": ln1_b[l][None, :],
        "peer_wqT": peer_wq[l].T.astype(BF16), "peer_keys": peer_keys[l].astype(BF16),
        "peer_u": peer_u[l].astype(BF16),
        "peer_vT": peer_v[l].astype(BF16).T,
        "ln2_g_col": ln2_g[l][:, None], "ln2_b_col": ln2_b[l][:, None],
        "tril": tril, "ssd_expand": ssd_expand, "mlstm_expand": mlstm_expand,
    }
    return p, widths


def _tile_sizes(m):
    tm = 256 if m % 256 == 0 else LANES
    tb = 512 if m % 512 == 0 else LANES
    return tm, tb


def _peer_block(x1t, p, alpha):
    m = x1t.shape[1]
    tm, tb = _tile_sizes(m)
    s1, s2, st, alist, thr_rank = _peer_topk(x1t, p, tm)
    return _peer_dense(x1t, s1, s2, st, alist, thr_rank, p, alpha, tb)


def _layer_prompt(x, p, widths, alpha):
    bsz, seq, d = x.shape
    assert seq % CHUNK == 0 and seq >= CONV_K - 1
    x2 = x.reshape(bsz * seq, d)
    tm, _ = _tile_sizes(bsz * seq)
    z, xbc, u, v, o, gates = _inproj(x2, p["w_cat"], widths, tm)
    y_s, h_t = _ssd_chunk(xbc, z, gates, p, bsz, seq)
    h_m, c_new, n_new, m_new = _mlstm_chunk(u, v, o, gates, p, bsz, seq)
    x1t = _outproj(y_s, h_m, x2, p, alpha, 2 * tm if (bsz * seq) % (2 * tm) == 0 else tm)
    y = _peer_block(x1t, p, alpha).reshape(bsz, seq, d)
    n_mh = c_new.shape[1]
    d_ssd = z.shape[1]
    n_sh = d_ssd // SSD_HEAD_DIM
    hg = n_sh // SSD_GROUPS
    new_ssd = h_t.reshape(bsz, SSD_GROUPS, SSD_STATE, hg, SSD_HEAD_DIM).transpose(0, 1, 3, 4, 2)
    new_ssd = new_ssd.reshape(bsz, n_sh, SSD_HEAD_DIM, SSD_STATE)
    states = (new_ssd,
              xbc.reshape(bsz, seq, -1)[:, seq - (CONV_K - 1):, :],
              c_new,
              n_new[:, :, 0, :],
              m_new[:, 0, GATE_I0:GATE_I0 + n_mh],
              u.reshape(bsz, seq, -1)[:, seq - (CONV_K - 1):, :])
    return y, states


def _layer_sample(x, st, p, widths, alpha):
    st_ssd, st_ssd_conv, st_c, st_n, st_m, st_mconv = st
    nb, seq, d = x.shape
    assert seq == 1 and nb % LANES == 0
    x2 = x.reshape(nb, d)
    z, xbc, u, v, o, gates = _inproj(x2, p["w_cat"], widths, LANES)
    y_s, new_ssd, new_ssd_conv = _ssd_step(xbc, z, gates, st_ssd_conv, st_ssd, p)
    n_mh = st_c.shape[1]
    m_row = jnp.pad(st_m.astype(F32), ((0, 0), (GATE_I0, LANES - GATE_I0 - n_mh)))[:, None, :]
    h_m, c_new, n_new, m_new, new_mconv = _mlstm_step(u, v, o, gates, st_mconv, st_c, st_n, m_row, p)
    x1t = _outproj(y_s, h_m, x2, p, alpha, LANES)
    y = _peer_block(x1t, p, alpha).reshape(nb, seq, d)
    states = (new_ssd, new_ssd_conv, c_new, n_new, m_new[:, 0, GATE_I0:GATE_I0 + n_mh], new_mconv)
    return y, states


def kernel(x_prompt, x_sample, state_ssd, state_ssd_conv, state_mlstm_c, state_mlstm_n, state_mlstm_m,
           state_mlstm_conv, w_in, conv_ssd_w, conv_ssd_b, dt_bias, a_log, d_skip, ssd_norm_g, conv_m_w,
           conv_m_b, w_mq, w_mk, b_igate, b_fgate, mlstm_norm_g, w_out, ln1_g, ln1_b, peer_wq, peer_keys,
           peer_u, peer_v, ln2_g, ln2_b):
    depth = w_in.shape[0]
    alpha = (2.0 * depth) ** 0.25
    states = (state_ssd, state_ssd_conv, state_mlstm_c, state_mlstm_n, state_mlstm_m, state_mlstm_conv)
    yp, ys = x_prompt, x_sample
    p_list, s_list = [], []
    for l in range(depth):
        p, widths = _prep_layer(l, w_in, conv_ssd_w, conv_ssd_b, dt_bias, a_log, d_skip, ssd_norm_g, conv_m_w,
                                conv_m_b, w_mq, w_mk, b_igate, b_fgate, mlstm_norm_g, w_out, ln1_g, ln1_b,
                                peer_wq, peer_keys, peer_u, peer_v, ln2_g, ln2_b)
        yp, p_new = _layer_prompt(yp, p, widths, alpha)
        ys, s_new = _layer_sample(ys, tuple(s[l] for s in states), p, widths, alpha)
        p_list.append(p_new)
        s_list.append(s_new)
    p_out = tuple(jnp.stack([pn[i] for pn in p_list]) for i in range(6))
    s_out = tuple(jnp.stack([sn[i] for sn in s_list]) for i in range(6))
    return (yp, ys) + p_out + s_out
```

```python
import functools
import math

import jax
import jax.numpy as jnp
from jax import lax
from jax.experimental import pallas as pl
from jax.experimental.pallas import tpu as pltpu

F32 = jnp.float32
BF16 = jnp.bfloat16

LANES = 128
SUBLANES = 8
CHUNK = 128
CONV_K = 4
LN_EPS = 1e-5
VMEM_LIMIT = 58 * 1024 * 1024

SSD_HEAD_DIM = 64
SSD_STATE = 128
SSD_GROUPS = 2
PEER_TOPK = 16
GATE_I0 = 16
N_STATS = 4
PEER_CHUNK = 2048
STEP_SEQS = 4
OUTPROJ_ROWS = 256


def _cparams(*sem):
    return pltpu.CompilerParams(dimension_semantics=sem, vmem_limit_bytes=VMEM_LIMIT)


def _dot(a, b):
    return jnp.dot(a, b, preferred_element_type=F32)


def _dot_nt(a, b):
    return lax.dot_general(a, b, (((1,), (1,)), ((), ())), preferred_element_type=F32)


def _split3(x):
    h = x.astype(BF16)
    r = x - h.astype(F32)
    m = r.astype(BF16)
    lo = (r - m.astype(F32)).astype(BF16)
    return h, m, lo


def _dot_sel_l(sel_bf16, x):
    h, m, lo = _split3(x)
    return _dot(sel_bf16, h) + _dot(sel_bf16, m) + _dot(sel_bf16, lo)


def _dot_sel_r(x, sel_bf16):
    h, m, lo = _split3(x)
    return _dot(h, sel_bf16) + _dot(m, sel_bf16) + _dot(lo, sel_bf16)


def _sigmoid(x):
    return 1.0 / (1.0 + jnp.exp(-x))


def _silu(x):
    return x * _sigmoid(x)


def _softplus(x):
    return jnp.maximum(x, 0.0) + jnp.log1p(jnp.exp(-jnp.abs(x)))


def _log_sigmoid(x):
    return -_softplus(-x)


def _gelu_exact(x):
    return 0.5 * x * (1.0 + lax.erf(x * (1.0 / math.sqrt(2.0))))


def _inproj_kernel(x_ref, w_ref, *out_refs):
    xb = x_ref[...].astype(BF16)
    off = 0
    for ref in out_refs:
        n = ref.shape[-1]
        ref[...] = _dot(xb, w_ref[:, off:off + n])
        off += n


def _inproj(x2, w_cat, widths, tm):
    m, d = x2.shape
    n_all = w_cat.shape[1]
    return pl.pallas_call(
        _inproj_kernel,
        grid=(m // tm,),
        in_specs=[pl.BlockSpec((tm, d), lambda i: (i, 0)),
                  pl.BlockSpec((d, n_all), lambda i: (0, 0))],
        out_specs=[pl.BlockSpec((tm, n), lambda i: (i, 0)) for n in widths],
        out_shape=[jax.ShapeDtypeStruct((m, n), F32) for n in widths],
        compiler_params=_cparams("arbitrary"),
        name="inproj",
    )(x2, w_cat)


def _ssd_chunk_kernel(xbc_ref, z_ref, g_ref, cw_ref, cb_ref, dtb_ref, alog_ref, dsk_ref, ng_ref,
                      tril_ref, exp_ref, y_ref, hT_out_ref, xpad, hT, ybuf, *, nc, d_ssd, n_heads):
    c = pl.program_id(1)
    cd = xpad.shape[1]
    hg = n_heads // SSD_GROUPS
    gw = d_ssd // SSD_GROUPS

    @pl.when(c == 0)
    def _():
        xpad[0:SUBLANES, :] = jnp.zeros((SUBLANES, cd), F32)
        hT[...] = jnp.zeros_like(hT)

    xpad[SUBLANES:SUBLANES + CHUNK, :] = xbc_ref[...]
    conv = cb_ref[...]
    for j in range(CONV_K):
        r0 = SUBLANES - (CONV_K - 1) + j
        conv = conv + cw_ref[j:j + 1, :] * xpad[r0:r0 + CHUNK, :]
    xpad[0:SUBLANES, :] = xpad[CHUNK:CHUNK + SUBLANES, :]
    xc = _silu(conv)
    xs = xc[:, 0:d_ssd]

    lane = lax.broadcasted_iota(jnp.int32, (1, LANES), 1)
    a_row = jnp.where(lane < n_heads, -jnp.exp(alog_ref[...]), 0.0)
    dt = _softplus(g_ref[:, 0:LANES] + dtb_ref[...])
    acs = _dot_sel_l(tril_ref[...], dt * a_row)
    dt_t = dt.T
    acs_t = acs.T
    eacs = jnp.exp(acs)
    dec = jnp.exp(acs[CHUNK - 1:CHUNK, :] - acs) * dt
    both = _dot_sel_r(jnp.concatenate([eacs, dec], axis=0), exp_ref[...])
    eacs_full = both[0:CHUNK]
    dec_full = both[CHUNK:2 * CHUNK]

    row = lax.broadcasted_iota(jnp.int32, (CHUNK, CHUNK), 0)
    col = lax.broadcasted_iota(jnp.int32, (CHUNK, CHUNK), 1)
    causal = row >= col

    gs = range(SSD_GROUPS)
    b_g = [xc[:, d_ssd + g * SSD_STATE:d_ssd + (g + 1) * SSD_STATE] for g in gs]
    c_g = [xc[:, d_ssd + (SSD_GROUPS + g) * SSD_STATE:d_ssd + (SSD_GROUPS + g + 1) * SSD_STATE] for g in gs]
    b_gb = [x.astype(BF16) for x in b_g]
    c_gb = [x.astype(BF16) for x in c_g]
    cb = [_dot_nt(c_gb[g], b_gb[g]) for g in gs]
    h_old = [hT[g] for g in gs]
    y_inter = jnp.concatenate([_dot(c_gb[g], h_old[g].astype(BF16)) for g in gs], axis=1)
    for h in range(n_heads):
        diff = acs[:, h:h + 1] - acs_t[h:h + 1, :]
        lmat = jnp.exp(jnp.where(causal, diff, -jnp.inf))
        mm = cb[h // hg] * lmat * dt_t[h:h + 1, :]
        x_h = xs[:, h * SSD_HEAD_DIM:(h + 1) * SSD_HEAD_DIM]
        ybuf[:, h * SSD_HEAD_DIM:(h + 1) * SSD_HEAD_DIM] = _dot(mm.astype(BF16), x_h.astype(BF16))
    for g in gs:
        gsl = slice(g * gw, (g + 1) * gw)
        xd = xs[:, gsl] * dec_full[:, gsl]
        hT[g] = h_old[g] * eacs_full[CHUNK - 1:CHUNK, gsl] + _dot(b_g[g].T.astype(BF16), xd.astype(BF16))

    y = ybuf[...] + y_inter * eacs_full + dsk_ref[...] * xs
    y = y * _silu(z_ref[...])
    y = y * lax.rsqrt(jnp.mean(y * y, axis=-1, keepdims=True) + LN_EPS) * ng_ref[...]
    y_ref[...] = y.astype(y_ref.dtype)

    @pl.when(c == nc - 1)
    def _():
        hT_out_ref[0] = hT[...]


def _ssd_chunk(xbc, z, gates, p, bsz, seq):
    nc = seq // CHUNK
    cd = xbc.shape[1]
    d_ssd = z.shape[1]
    n_heads = d_ssd // SSD_HEAD_DIM
    gw = d_ssd // SSD_GROUPS
    tok = lambda b, c: (b * nc + c, 0)
    cst = lambda b, c: (0, 0)
    return pl.pallas_call(
        functools.partial(_ssd_chunk_kernel, nc=nc, d_ssd=d_ssd, n_heads=n_heads),
        grid=(bsz, nc),
        in_specs=[pl.BlockSpec((CHUNK, cd), tok),
                  pl.BlockSpec((CHUNK, d_ssd), tok),
                  pl.BlockSpec((CHUNK, 2 * LANES), tok),
                  pl.BlockSpec((CONV_K, cd), cst),
                  pl.BlockSpec((1, cd), cst),
                  pl.BlockSpec((1, LANES), cst),
                  pl.BlockSpec((1, LANES), cst),
                  pl.BlockSpec((1, d_ssd), cst),
                  pl.BlockSpec((1, d_ssd), cst),
                  pl.BlockSpec((CHUNK, CHUNK), cst),
                  pl.BlockSpec((LANES, d_ssd), cst)],
        out_specs=[pl.BlockSpec((CHUNK, d_ssd), tok),
                   pl.BlockSpec((1, SSD_GROUPS, SSD_STATE, gw), lambda b, c: (b, 0, 0, 0))],
        out_shape=[jax.ShapeDtypeStruct((bsz * seq, d_ssd), BF16),
                   jax.ShapeDtypeStruct((bsz, SSD_GROUPS, SSD_STATE, gw), F32)],
        scratch_shapes=[pltpu.VMEM((CHUNK + SUBLANES, cd), F32),
                        pltpu.VMEM((SSD_GROUPS, SSD_STATE, gw), F32),
                        pltpu.VMEM((CHUNK, d_ssd), F32)],
        compiler_params=_cparams("arbitrary", "arbitrary"),
        name="ssd_chunk",
    )(xbc, z, gates, p["conv_ssd_w"], p["conv_ssd_b"], p["dt_bias_row"], p["a_log_row"],
      p["d_skip_row"], p["ssd_norm_g"], p["tril"], p["ssd_expand"])


def _mlstm_chunk_kernel(u_ref, v_ref, o_ref, g_ref, cw_ref, cb_ref, wq_ref, wk_ref, bi_ref, bf_ref, ng_ref,
                        tril_ref, y_ref, c_out_ref, n_out_ref, m_out_ref, upad, c_st, n_st, m_st,
                        *, nc, n_heads, dk):
    c = pl.program_id(1)
    dm = upad.shape[1]

    @pl.when(c == 0)
    def _():
        upad[0:SUBLANES, :] = jnp.zeros((SUBLANES, dm), F32)
        c_st[...] = jnp.zeros_like(c_st)
        n_st[...] = jnp.zeros_like(n_st)
        m_st[...] = jnp.zeros_like(m_st)

    upad[SUBLANES:SUBLANES + CHUNK, :] = u_ref[...]
    conv = cb_ref[...]
    for j in range(CONV_K):
        r0 = SUBLANES - (CONV_K - 1) + j
        conv = conv + cw_ref[j:j + 1, :] * upad[r0:r0 + CHUNK, :]
    upad[0:SUBLANES, :] = upad[CHUNK:CHUNK + SUBLANES, :]
    uc = _silu(conv).astype(BF16)

    logi = g_ref[:, 0:LANES] + bi_ref[...]
    logf = _log_sigmoid(g_ref[:, LANES:2 * LANES] + bf_ref[...])
    bcs = _dot_sel_l(tril_ref[...], logf)
    m_prev = m_st[0:1, :]
    b_end = bcs[CHUNK - 1:CHUNK, :]
    g_all = b_end - bcs + logi
    m_new = jnp.maximum(b_end + m_prev, jnp.max(g_all, axis=0, keepdims=True))
    w_old = jnp.exp(b_end + m_prev - m_new)
    w_s_all = jnp.exp(g_all - m_new)
    inter_all = bcs + m_prev
    bcs_t = bcs.T
    logi_t = logi.T

    row = lax.broadcasted_iota(jnp.int32, (CHUNK, CHUNK), 0)
    col = lax.broadcasted_iota(jnp.int32, (CHUNK, CHUNK), 1)
    causal = row >= col
    scale = dk ** -0.5

    hs = range(n_heads)
    sls = [slice(h * dk, (h + 1) * dk) for h in hs]
    gls = [GATE_I0 + h for h in hs]
    q = [_dot(uc[:, sls[h]], wq_ref[h]) for h in hs]
    k = [_dot(uc[:, sls[h]], wk_ref[h]) * scale for h in hs]
    qb = [x.astype(BF16) for x in q]
    kb = [x.astype(BF16) for x in k]
    vb = [v_ref[:, sls[h]].astype(BF16) for h in hs]
    dmat = [jnp.where(causal, bcs[:, g:g + 1] - bcs_t[g:g + 1, :] + logi_t[g:g + 1, :], -jnp.inf) for g in gls]
    inter = [inter_all[:, g:g + 1] for g in gls]
    m_t = [jnp.maximum(inter[h], jnp.max(dmat[h], axis=1, keepdims=True)) for h in hs]
    w_inter = [jnp.exp(inter[h] - m_t[h]) for h in hs]
    qk = [_dot_nt(qb[h], kb[h]) for h in hs]
    c_old = [c_st[h] for h in hs]
    n_old = [n_st[h][0:1, :] for h in hs]
    qc = [_dot(qb[h], c_old[h].astype(BF16)) for h in hs]
    att = [qk[h] * jnp.exp(dmat[h] - m_t[h]) for h in hs]
    av = [_dot(att[h].astype(BF16), vb[h]) for h in hs]
    kw = [k[h] * w_s_all[:, gls[h]:gls[h] + 1] for h in hs]
    kv = [_dot(kw[h].T.astype(BF16), vb[h]) for h in hs]
    num = [av[h] + w_inter[h] * qc[h] for h in hs]
    den = [jnp.sum(att[h], axis=1, keepdims=True) + w_inter[h] * jnp.sum(q[h] * n_old[h], axis=1, keepdims=True)
           for h in hs]
    hout = [num[h] / jnp.maximum(jnp.abs(den[h]), jnp.exp(-m_t[h])) for h in hs]
    mu = [jnp.mean(x, axis=-1, keepdims=True) for x in hout]
    cen = [hout[h] - mu[h] for h in hs]
    var = [jnp.mean(jnp.square(x), axis=-1, keepdims=True) for x in cen]
    for h in hs:
        sl = sls[h]
        hn = cen[h] * lax.rsqrt(var[h] + LN_EPS) * ng_ref[:, sl]
        y_ref[:, sl] = (_sigmoid(o_ref[:, sl]) * hn).astype(y_ref.dtype)
        wo = w_old[:, gls[h]:gls[h] + 1]
        c_st[h] = wo * c_old[h] + kv[h]
        n_st[h] = jnp.broadcast_to(wo * n_old[h] + jnp.sum(kw[h], axis=0, keepdims=True), (SUBLANES, dk))

    m_st[...] = jnp.broadcast_to(m_new, (SUBLANES, LANES))

    @pl.when(c == nc - 1)
    def _():
        c_out_ref[0] = c_st[...]
        n_out_ref[0] = n_st[...]
        m_out_ref[0] = m_st[...]


def _mlstm_chunk(u, v, o, gates, p, bsz, seq):
    nc = seq // CHUNK
    dm = u.shape[1]
    n_heads, dk, _ = p["w_mq"].shape
    tok = lambda b, c: (b * nc + c, 0)
    cst = lambda b, c: (0, 0)
    cst3 = lambda b, c: (0, 0, 0)
    return pl.pallas_call(
        functools.partial(_mlstm_chunk_kernel, nc=nc, n_heads=n_heads, dk=dk),
        grid=(bsz, nc),
        in_specs=[pl.BlockSpec((CHUNK, dm), tok),
                  pl.BlockSpec((CHUNK, dm), tok),
                  pl.BlockSpec((CHUNK, dm), tok),
                  pl.BlockSpec((CHUNK, 2 * LANES), tok),
                  pl.BlockSpec((CONV_K, dm), cst),
                  pl.BlockSpec((1, dm), cst),
                  pl.BlockSpec((n_heads, dk, dk), cst3),
                  pl.BlockSpec((n_heads, dk, dk), cst3),
                  pl.BlockSpec((1, LANES), cst),
                  pl.BlockSpec((1, LANES), cst),
                  pl.BlockSpec((1, dm), cst),
                  pl.BlockSpec((CHUNK, CHUNK), cst)],
        out_specs=[pl.BlockSpec((CHUNK, dm), tok),
                   pl.BlockSpec((1, n_heads, dk, dk), lambda b, c: (b, 0, 0, 0)),
                   pl.BlockSpec((1, n_heads, SUBLANES, dk), lambda b, c: (b, 0, 0, 0)),
                   pl.BlockSpec((1, SUBLANES, LANES), lambda b, c: (b, 0, 0))],
        out_shape=[jax.ShapeDtypeStruct((bsz * seq, dm), BF16),
                   jax.ShapeDtypeStruct((bsz, n_heads, dk, dk), F32),
                   jax.ShapeDtypeStruct((bsz, n_heads, SUBLANES, dk), F32),
                   jax.ShapeDtypeStruct((bsz, SUBLANES, LANES), F32)],
        scratch_shapes=[pltpu.VMEM((CHUNK + SUBLANES, dm), F32),
                        pltpu.VMEM((n_heads, dk, dk), F32),
                        pltpu.VMEM((n_heads, SUBLANES, dk), F32),
                        pltpu.VMEM((SUBLANES, LANES), F32)],
        compiler_params=_cparams("arbitrary", "arbitrary"),
        name="mlstm_chunk",
    )(u, v, o, gates, p["conv_m_w"], p["conv_m_b"], p["w_mq"], p["w_mk"], p["b_i_row"], p["b_f_row"],
      p["mlstm_norm_g"], p["tril"])


def _col_bcast(row, n_rows):
    return jnp.broadcast_to(row, (LANES, row.shape[1])).T[0:n_rows, :]


def _per_sequence(one_sequence, n_seq_in, n_seq_out):
    def kernel(*refs, **static):
        seq_in, shared, seq_out = refs[:n_seq_in], refs[n_seq_in:len(refs) - n_seq_out], refs[len(refs) - n_seq_out:]
        for s in range(seq_in[0].shape[0]):
            one = slice(s, s + 1)
            one_sequence(*(r.at[one] for r in seq_in), *shared, *(r.at[one] for r in seq_out), **static)
    return kernel


def _ssd_step_one(xbc_ref, z_ref, g_ref, buf_ref, st_ref, cw_ref, cb_ref, dtb_ref, alog_ref, dsk_ref, ng_ref,
                  exp_ref, y_ref, st_out_ref, buf_out_ref, *, d_ssd, n_heads):
    gw = d_ssd // SSD_GROUPS
    x_row = xbc_ref[0]
    buf_out_ref[0, 0:CONV_K - 2, :] = buf_ref[0, 1:CONV_K - 1, :]
    buf_out_ref[0, CONV_K - 2:CONV_K - 1, :] = x_row
    conv = cb_ref[...] + cw_ref[CONV_K - 1:CONV_K, :] * x_row
    for j in range(CONV_K - 1):
        conv = conv + cw_ref[j:j + 1, :] * buf_ref[0, j:j + 1, :]
    xc = _silu(conv)
    xs = xc[:, 0:d_ssd]

    lane = lax.broadcasted_iota(jnp.int32, (1, LANES), 1)
    a_row = jnp.where(lane < n_heads, -jnp.exp(alog_ref[...]), 0.0)
    dt = _softplus(g_ref[0][:, 0:LANES] + dtb_ref[...])
    dec = jnp.exp(dt * a_row)
    both = _dot_sel_r(jnp.concatenate([jnp.broadcast_to(dt, (SUBLANES, LANES)),
                                       jnp.broadcast_to(dec, (SUBLANES, LANES))], axis=0), exp_ref[...])
    dt_full = both[0:1]
    dec_full = both[SUBLANES:SUBLANES + 1]
    dtx = dt_full * xs
    dtx_col = _col_bcast(dtx, d_ssd)
    dec_col = jnp.concatenate([jnp.broadcast_to(dec[:, h:h + 1], (SSD_HEAD_DIM, LANES)) for h in range(n_heads)],
                              axis=0)

    y_parts = []
    for g in range(SSD_GROUPS):
        b_g = xc[:, d_ssd + g * SSD_STATE:d_ssd + (g + 1) * SSD_STATE]
        c_g = xc[:, d_ssd + (SSD_GROUPS + g) * SSD_STATE:d_ssd + (SSD_GROUPS + g + 1) * SSD_STATE]
        h_g = st_ref[0, g * gw:(g + 1) * gw, :]
        cb = jnp.sum(c_g * b_g, axis=-1, keepdims=True)
        c8 = jnp.broadcast_to(c_g, (SUBLANES, SSD_STATE)).astype(BF16)
        y_int = _dot_nt(c8, h_g.astype(BF16))[0:1]
        sl = slice(g * gw, (g + 1) * gw)
        y_parts.append(cb * dtx[:, sl] + dec_full[:, sl] * y_int)
        st_out_ref[0, sl, :] = h_g * dec_col[sl, :] + dtx_col[sl, :] * b_g
    y = jnp.concatenate(y_parts, axis=1) + dsk_ref[...] * xs
    y = y * _silu(z_ref[0])
    y = y * lax.rsqrt(jnp.mean(y * y, axis=-1, keepdims=True) + LN_EPS) * ng_ref[...]
    y_ref[0] = y.astype(y_ref.dtype)


def _ssd_step(xbc, z, gates, buf, state, p):
    nb, cd = xbc.shape
    d_ssd = z.shape[1]
    n_heads = d_ssd // SSD_HEAD_DIM
    row3 = lambda b: (b, 0, 0)
    cst = lambda b: (0, 0)
    st2 = state.reshape(nb, d_ssd, SSD_STATE)
    sb = STEP_SEQS
    assert nb % sb == 0
    y, st_new, buf_new = pl.pallas_call(
        functools.partial(_per_sequence(_ssd_step_one, 5, 3), d_ssd=d_ssd, n_heads=n_heads),
        grid=(nb // sb,),
        in_specs=[pl.BlockSpec((sb, 1, cd), row3),
                  pl.BlockSpec((sb, 1, d_ssd), row3),
                  pl.BlockSpec((sb, 1, 2 * LANES), row3),
                  pl.BlockSpec((sb, CONV_K - 1, cd), row3),
                  pl.BlockSpec((sb, d_ssd, SSD_STATE), row3),
                  pl.BlockSpec((CONV_K, cd), cst),
                  pl.BlockSpec((1, cd), cst),
                  pl.BlockSpec((1, LANES), cst),
                  pl.BlockSpec((1, LANES), cst),
                  pl.BlockSpec((1, d_ssd), cst),
                  pl.BlockSpec((1, d_ssd), cst),
                  pl.BlockSpec((LANES, d_ssd), cst)],
        out_specs=[pl.BlockSpec((sb, 1, d_ssd), row3),
                   pl.BlockSpec((sb, d_ssd, SSD_STATE), row3),
                   pl.BlockSpec((sb, CONV_K - 1, cd), row3)],
        out_shape=[jax.ShapeDtypeStruct((nb, 1, d_ssd), BF16),
                   jax.ShapeDtypeStruct((nb, d_ssd, SSD_STATE), F32),
                   jax.ShapeDtypeStruct((nb, CONV_K - 1, cd), F32)],
        compiler_params=_cparams("arbitrary"),
        name="ssd_step",
    )(xbc.reshape(nb, 1, cd), z.reshape(nb, 1, d_ssd), gates.reshape(nb, 1, 2 * LANES), buf, st2,
      p["conv_ssd_w"], p["conv_ssd_b"], p["dt_bias_row"], p["a_log_row"], p["d_skip_row"], p["ssd_norm_g"],
      p["ssd_expand"])
    return y.reshape(nb, d_ssd), st_new.reshape(state.shape), buf_new


def _mlstm_step_one(u_ref, v_ref, o_ref, g_ref, buf_ref, c_ref, n_ref, m_ref, cw_ref, cb_ref, wq_ref, wk_ref,
                    bi_ref, bf_ref, ng_ref, hexp_ref, y_ref, c_out_ref, n_out_ref, m_out_ref, buf_out_ref,
                    *, n_heads, dk):
    buf_out_ref[0, 0:CONV_K - 2, :] = buf_ref[0, 1:CONV_K - 1, :]
    buf_out_ref[0, CONV_K - 2:CONV_K - 1, :] = u_ref[0]
    conv = cb_ref[...] + cw_ref[CONV_K - 1:CONV_K, :] * u_ref[0]
    for j in range(CONV_K - 1):
        conv = conv + cw_ref[j:j + 1, :] * buf_ref[0, j:j + 1, :]
    uc = jnp.broadcast_to(_silu(conv), (SUBLANES, conv.shape[1])).astype(BF16)

    gt = g_ref[0]
    logi = gt[:, 0:LANES] + bi_ref[...]
    logf = _log_sigmoid(gt[:, LANES:2 * LANES] + bf_ref[...])
    m_prev = m_ref[0]
    inter = logf + m_prev
    m_t = jnp.maximum(inter, logi)
    w_inter = jnp.exp(inter - m_t)
    w_in = jnp.exp(logi - m_t)
    floor = jnp.exp(-m_t)
    rows = jnp.concatenate([w_inter, w_in, floor, jnp.zeros((SUBLANES - 3, LANES), F32)], axis=0)
    full = _dot_sel_r(rows, hexp_ref[...])
    scale = dk ** -0.5

    for h in range(n_heads):
        sl = slice(h * dk, (h + 1) * dk)
        q = _dot(uc[:, sl], wq_ref[h])[0:1]
        k = _dot(uc[:, sl], wk_ref[h])[0:1] * scale
        v_h = v_ref[0][:, sl]
        wi = full[0:1, sl]
        ws = full[1:2, sl]
        fl = full[2:3, sl]
        c_h = c_ref[0, h]
        n_h = n_ref[0, h:h + 1, :]
        att = jnp.sum(q * k, axis=-1, keepdims=True) * ws
        q8 = jnp.broadcast_to(q, (SUBLANES, dk)).astype(BF16)
        num = att * v_h + wi * _dot(q8, c_h.astype(BF16))[0:1]
        den = att + wi * jnp.sum(q * n_h, axis=-1, keepdims=True)
        hout = num / jnp.maximum(jnp.abs(den), fl)
        mu = jnp.mean(hout, axis=-1, keepdims=True)
        var = jnp.mean(jnp.square(hout - mu), axis=-1, keepdims=True)
        hn = (hout - mu) * lax.rsqrt(var + LN_EPS) * ng_ref[:, sl]
        y_ref[0, :, sl] = (_sigmoid(o_ref[0][:, sl]) * hn).astype(y_ref.dtype)
        kw = k * ws
        kw_col = _col_bcast(kw, dk)
        wi_full = jnp.broadcast_to(wi, (dk, dk))
        for half in range(dk // LANES):
            hs = slice(half * LANES, (half + 1) * LANES)
            c_out_ref[0, h, :, hs] = wi_full[:, hs] * c_h[:, hs] + kw_col * v_h[:, hs]
        n_out_ref[0, h:h + 1, :] = wi * n_h + kw
    m_out_ref[0] = m_t


def _mlstm_step(u, v, o, gates, buf, c_st, n_st, m_row, p):
    nb, dm = u.shape
    n_heads, dk, _ = p["w_mq"].shape
    row3 = lambda b: (b, 0, 0)
    cst = lambda b: (0, 0)
    cst3 = lambda b: (0, 0, 0)
    r3 = lambda a: a.reshape(nb, 1, a.shape[-1])
    sb = STEP_SEQS
    assert nb % sb == 0
    y, c_new, n_new, m_new, buf_new = pl.pallas_call(
        functools.partial(_per_sequence(_mlstm_step_one, 8, 5), n_heads=n_heads, dk=dk),
        grid=(nb // sb,),
        in_specs=[pl.BlockSpec((sb, 1, dm), row3),
                  pl.BlockSpec((sb, 1, dm), row3),
                  pl.BlockSpec((sb, 1, dm), row3),
                  pl.BlockSpec((sb, 1, 2 * LANES), row3),
                  pl.BlockSpec((sb, CONV_K - 1, dm), row3),
                  pl.BlockSpec((sb, n_heads, dk, dk), lambda b: (b, 0, 0, 0)),
                  pl.BlockSpec((sb, n_heads, dk), row3),
                  pl.BlockSpec((sb, 1, LANES), row3),
                  pl.BlockSpec((CONV_K, dm), cst),
                  pl.BlockSpec((1, dm), cst),
                  pl.BlockSpec((n_heads, dk, dk), cst3),
                  pl.BlockSpec((n_heads, dk, dk), cst3),
                  pl.BlockSpec((1, LANES), cst),
                  pl.BlockSpec((1, LANES), cst),
                  pl.BlockSpec((1, dm), cst),
                  pl.BlockSpec((LANES, dm), cst)],
        out_specs=[pl.BlockSpec((sb, 1, dm), row3),
                   pl.BlockSpec((sb, n_heads, dk, dk), lambda b: (b, 0, 0, 0)),
                   pl.BlockSpec((sb, n_heads, dk), row3),
                   pl.BlockSpec((sb, 1, LANES), row3),
                   pl.BlockSpec((sb, CONV_K - 1, dm), row3)],
        out_shape=[jax.ShapeDtypeStruct((nb, 1, dm), BF16),
                   jax.ShapeDtypeStruct(c_st.shape, F32),
                   jax.ShapeDtypeStruct(n_st.shape, F32),
                   jax.ShapeDtypeStruct((nb, 1, LANES), F32),
                   jax.ShapeDtypeStruct((nb, CONV_K - 1, dm), F32)],
        compiler_params=_cparams("arbitrary"),
        name="mlstm_step",
    )(r3(u), r3(v), r3(o), r3(gates), buf, c_st, n_st, m_row, p["conv_m_w"], p["conv_m_b"], p["w_mq"],
      p["w_mk"], p["b_i_row"], p["b_f_row"], p["mlstm_norm_g"], p["mlstm_expand"])
    return y.reshape(nb, dm), c_new, n_new, m_new, buf_new


def _outproj_kernel(ys_ref, hm_ref, x_ref, w_ref, g_ref, b_ref, x1t_ref, *, alpha):
    d_ssd = ys_ref.shape[1]
    tm = ys_ref.shape[0]
    subs = [slice(r0, min(r0 + OUTPROJ_ROWS, tm)) for r0 in range(0, tm, OUTPROJ_ROWS)]
    mix = [_dot(ys_ref[rs, :], w_ref[0:d_ssd, :]) + _dot(hm_ref[rs, :], w_ref[d_ssd:, :]) for rs in subs]
    for rs, mx in zip(subs, mix):
        r = alpha * x_ref[rs, :] + mx
        mu = jnp.mean(r, axis=-1, keepdims=True)
        var = jnp.mean(jnp.square(r - mu), axis=-1, keepdims=True)
        x1 = (r - mu) * lax.rsqrt(var + LN_EPS) * g_ref[...] + b_ref[...]
        x1t_ref[:, rs] = x1.T


def _outproj(ys, hm, x2, p, alpha, tm):
    m, d = x2.shape
    d_ssd, d_m = ys.shape[1], hm.shape[1]
    return pl.pallas_call(
        functools.partial(_outproj_kernel, alpha=alpha),
        grid=(m // tm,),
        in_specs=[pl.BlockSpec((tm, d_ssd), lambda i: (i, 0)),
                  pl.BlockSpec((tm, d_m), lambda i: (i, 0)),
                  pl.BlockSpec((tm, d), lambda i: (i, 0)),
                  pl.BlockSpec((d_ssd + d_m, d), lambda i: (0, 0)),
                  pl.BlockSpec((1, d), lambda i: (0, 0)),
                  pl.BlockSpec((1, d), lambda i: (0, 0))],
        out_specs=pl.BlockSpec((d, tm), lambda i: (0, i)),
        out_shape=jax.ShapeDtypeStruct((d, m), F32),
        compiler_params=_cparams("arbitrary"),
        name="outproj_ln1",
    )(ys, hm, x2, p["w_out"], p["ln1_g"], p["ln1_b"])


def _oddeven_merge_sort_pairs(n):
    pairs = []
    p = 1
    while p < n:
        k = p
        while k >= 1:
            for j in range(k % p, n - k, 2 * k):
                for i in range(min(k, n - j - k)):
                    if (i + j) // (2 * p) == (i + j + k) // (2 * p):
                        pairs.append((i + j, i + j + k))
            k //= 2
        p *= 2
    return pairs


_SORT_TOPK = _oddeven_merge_sort_pairs(PEER_TOPK)


def _compare_exchange(lst, i, j):
    hi, lo = jnp.maximum(lst[i], lst[j]), jnp.minimum(lst[i], lst[j])
    lst[i], lst[j] = hi, lo


def _bitonic_to_sorted(t):
    d = PEER_TOPK // 2
    while d >= 1:
        for i in range(PEER_TOPK):
            if i & d == 0:
                _compare_exchange(t, i, i + d)
        d //= 2
    return t


def _merge_top(x, y):
    neg = jnp.full_like(x[0], -jnp.inf)
    x = x + [neg] * (PEER_TOPK - len(x))
    y = y + [neg] * (PEER_TOPK - len(y))
    return _bitonic_to_sorted([jnp.maximum(x[k], y[PEER_TOPK - 1 - k]) for k in range(PEER_TOPK)])


def _top_rows(s):
    lst = [s[SUBLANES * g:SUBLANES * (g + 1), :] for g in range(PEER_TOPK)]
    for i, j in _SORT_TOPK:
        _compare_exchange(lst, i, j)
    shift = SUBLANES // 2
    while shift >= 1:
        lst = _merge_top(lst, [pltpu.roll(a, shift, 0) for a in lst])
        shift //= 2
    return lst


def _peer_topk_kernel(x1t_ref, wqt_ref, keys_ref, s1_ref, s2_ref, st_ref, a_scr, thr_ref, q_scr, b_scr, *, n_heads):
    tq = x1t_ref.shape[1]
    q_scr[...] = _dot(wqt_ref[...], x1t_ref[...].astype(BF16))
    half = q_scr.shape[0] // (2 * n_heads)

    for h in range(n_heads):
        for side, (s_ref, top_scr) in enumerate(((s1_ref, a_scr), (s2_ref, b_scr))):
            r0 = (2 * h + side) * half
            s = _dot(keys_ref[h, side], q_scr[r0:r0 + half, :].astype(BF16))
            for t in range(tq // LANES):
                ls = slice(t * LANES, (t + 1) * LANES)
                s_ref[h, t] = s[:, ls]
                top = _top_rows(s[:, ls])
                for k in range(PEER_TOPK):
                    top_scr[k, h:h + 1, ls] = top[k][0:1, :]

    for t in range(tq // LANES):
        ls = slice(t * LANES, (t + 1) * LANES)
        a = [a_scr[k, :, ls] for k in range(PEER_TOPK)]
        b = [b_scr[k, :, ls] for k in range(PEER_TOPK)]
        single = PEER_TOPK // 2
        lists = [[a[i] + b[j] for j in range(PEER_TOPK // (i + 1))] for i in range(single)]
        lists.append([a[i] + b[0] for i in range(single, PEER_TOPK)])
        merged = lists[0]
        for other in lists[1:]:
            merged = _merge_top(merged, other)
        tau = merged[PEER_TOPK - 1]
        top = a[0] + b[0]
        zsum = jnp.zeros_like(tau)
        for cand in (c for lst in lists for c in lst):
            zsum = zsum + jnp.where(cand >= tau, jnp.exp(cand - top), 0.0)
        st_ref[0, :, ls] = tau
        st_ref[1, :, ls] = a[0]
        st_ref[2, :, ls] = b[0]
        st_ref[3, :, ls] = 1.0 / zsum
        for i in range(PEER_TOPK):
            sums = lists[i] if i < single else [lists[single][i - single]]
            thr = jnp.full_like(tau, jnp.inf)
            for k, cand in enumerate(sums):
                thr = jnp.where(cand >= tau, b[k], thr)
            thr_ref[i, :, ls] = thr


def _peer_topk(x1t, p, tq):
    d, m = x1t.shape
    n_heads, _, n_keys, half = p["peer_keys"].shape
    nq = p["peer_wqT"].shape[0]
    assert n_heads == SUBLANES and n_keys == PEER_TOPK * SUBLANES
    keyb = lambda i: (0, i, 0, 0)
    return pl.pallas_call(
        functools.partial(_peer_topk_kernel, n_heads=n_heads),
        grid=(m // tq,),
        in_specs=[pl.BlockSpec((d, tq), lambda i: (0, i)),
                  pl.BlockSpec((nq, d), lambda i: (0, 0)),
                  pl.BlockSpec((n_heads, 2, n_keys, half), lambda i: (0, 0, 0, 0))],
        out_specs=[pl.BlockSpec((n_heads, tq // LANES, n_keys, LANES), keyb),
                   pl.BlockSpec((n_heads, tq // LANES, n_keys, LANES), keyb),
                   pl.BlockSpec((N_STATS, SUBLANES, tq), lambda i: (0, 0, i)),
                   pl.BlockSpec((PEER_TOPK, SUBLANES, tq), lambda i: (0, 0, i)),
                   pl.BlockSpec((PEER_TOPK, SUBLANES, tq), lambda i: (0, 0, i))],
        out_shape=[jax.ShapeDtypeStruct((n_heads, m // LANES, n_keys, LANES), F32),
                   jax.ShapeDtypeStruct((n_heads, m // LANES, n_keys, LANES), F32),
                   jax.ShapeDtypeStruct((N_STATS, SUBLANES, m), F32),
                   jax.ShapeDtypeStruct((PEER_TOPK, SUBLANES, m), F32),
                   jax.ShapeDtypeStruct((PEER_TOPK, SUBLANES, m), F32)],
        scratch_shapes=[pltpu.VMEM((nq, tq), F32),
                        pltpu.VMEM((PEER_TOPK, SUBLANES, tq), F32)],
        compiler_params=_cparams("arbitrary"),
        name="peer_topk",
    )(x1t, p["peer_wqT"], p["peer_keys"])


def _peer_dense_kernel(x1t_ref, s1_ref, s2_ref, st_ref, al_ref, tr_ref, u_ref, vt_ref, g_ref, b_ref, out_ref,
                       xb_scr, e2_scr, c_scr, thr_scr, acc_scr, s_scr, w_scr, thr_r, cr, *, alpha, n_heads):
    j = pl.program_id(1)
    ec = u_ref.shape[0]
    tb = x1t_ref.shape[1]
    nt = tb // LANES
    n_keys = s1_ref.shape[2]
    na = ec // n_keys

    @pl.when(j == 0)
    def _():
        xb_scr[...] = x1t_ref[...].astype(BF16)
        for h in range(n_heads):
            for t in range(nt):
                ls = slice(t * LANES, (t + 1) * LANES)
                e2_scr[h, t] = jnp.exp(s2_ref[h, t] - st_ref[2, h:h + 1, ls])
                s1 = s1_ref[h, t]
                c_scr[h, t] = jnp.exp(s1 - st_ref[1, h:h + 1, ls]) * (0.5 * st_ref[3, h:h + 1, ls])
                thr = jnp.full((n_keys, LANES), jnp.inf, F32)
                for k in range(PEER_TOPK):
                    thr = jnp.where(s1 == al_ref[k, h:h + 1, ls], tr_ref[k, h:h + 1, ls], thr)
                thr_scr[h, t] = thr
        acc_scr[...] = jnp.zeros_like(acc_scr)

    s_scr[:, 0:tb] = _dot(u_ref[...], xb_scr[...])

    g0 = pl.multiple_of(j * na, na)
    for h in range(n_heads):
        for t in range(nt):
            ls = slice(t * LANES, (t + 1) * LANES)
            thr_blk = thr_scr[h, t, pl.ds(g0, na), :]
            c_blk = c_scr[h, t, pl.ds(g0, na), :]
            for al in range(na):
                thr_r[h * na + al, :, ls] = thr_blk[al:al + 1, :]
                cr[h * na + al, :, ls] = c_blk[al:al + 1, :]

    def first_key(al, carry):
        r0 = pl.multiple_of(al * n_keys, n_keys)
        for t in range(nt):
            ls = slice(t * LANES, (t + 1) * LANES)
            gate = jnp.zeros((n_keys, LANES), F32)
            for h in range(n_heads):
                gate = gate + jnp.where(s2_ref[h, t] >= thr_r[h * na + al][:, ls],
                                        e2_scr[h, t] * cr[h * na + al][:, ls], 0.0)
            x = s_scr[pl.ds(r0, n_keys), ls]
            hid = x + x * lax.erf(x * (1.0 / math.sqrt(2.0)))
            w_scr[pl.ds(r0, n_keys), ls] = (gate * hid).astype(BF16)
        return carry

    lax.fori_loop(0, na, first_key, 0)
    acc_scr[...] += _dot(vt_ref[...], w_scr[:, 0:tb])

    @pl.when(j == pl.num_programs(1) - 1)
    def _():
        r = alpha * x1t_ref[...] + acc_scr[...]
        mu = jnp.mean(r, axis=0, keepdims=True)
        var = jnp.mean(jnp.square(r - mu), axis=0, keepdims=True)
        y = (r - mu) * lax.rsqrt(var + LN_EPS) * g_ref[...] + b_ref[...]
        out_ref[...] = y.T


def _peer_dense(x1t, s1, s2, st, alist, thr_rank, p, alpha, tb):
    d, m = x1t.shape
    n_heads, _, n_keys, _ = s1.shape
    n_exp = p["peer_u"].shape[0]
    ec = PEER_CHUNK
    nj = n_exp // ec
    na = ec // n_keys
    assert na % SUBLANES == 0
    nt = tb // LANES
    keyb = lambda i, j: (0, i, 0, 0)
    tokb = lambda i, j: (0, 0, i)
    pitch = tb + LANES
    return pl.pallas_call(
        functools.partial(_peer_dense_kernel, alpha=alpha, n_heads=n_heads),
        grid=(m // tb, nj),
        in_specs=[pl.BlockSpec((d, tb), lambda i, j: (0, i)),
                  pl.BlockSpec((n_heads, nt, n_keys, LANES), keyb),
                  pl.BlockSpec((n_heads, nt, n_keys, LANES), keyb),
                  pl.BlockSpec((N_STATS, SUBLANES, tb), tokb),
                  pl.BlockSpec((PEER_TOPK, SUBLANES, tb), tokb),
                  pl.BlockSpec((PEER_TOPK, SUBLANES, tb), tokb),
                  pl.BlockSpec((ec, d), lambda i, j: (j, 0)),
                  pl.BlockSpec((d, ec), lambda i, j: (0, j)),
                  pl.BlockSpec((d, 1), lambda i, j: (0, 0)),
                  pl.BlockSpec((d, 1), lambda i, j: (0, 0))],
        out_specs=pl.BlockSpec((tb, d), lambda i, j: (i, 0)),
        out_shape=jax.ShapeDtypeStruct((m, d), F32),
        scratch_shapes=[pltpu.VMEM((d, tb), BF16),
                        pltpu.VMEM((n_heads, nt, n_keys, LANES), F32),
                        pltpu.VMEM((n_heads, nt, n_keys, LANES), F32),
                        pltpu.VMEM((n_heads, nt, n_keys, LANES), F32),
                        pltpu.VMEM((d, tb), F32),
                        pltpu.VMEM((ec, pitch), F32),
                        pltpu.VMEM((ec, pitch), BF16),
                        pltpu.VMEM((n_heads * na, 1, tb), F32),
                        pltpu.VMEM((n_heads * na, 1, tb), F32)],
        compiler_params=_cparams("arbitrary", "arbitrary"),
        name="peer_dense",
    )(x1t, s1, s2, st, alist, thr_rank, p["peer_u"], p["peer_vT"], p["ln2_g_col"], p["ln2_b_col"])


def _pad_row(vec, offset):
    return jnp.pad(vec.astype(F32), (offset, LANES - offset - vec.shape[0]))[None, :]


def _prep_layer(l, w_in, conv_ssd_w, conv_ssd_b, dt_bias, a_log, d_skip, ssd_norm_g, conv_m_w, conv_m_b,
                w_mq, w_mk, b_igate, b_fgate, mlstm_norm_g, w_out, ln1_g, ln1_b, peer_wq, peer_keys,
                peer_u, peer_v, ln2_g, ln2_b):
    d_model = w_in.shape[1]
    d_ssd = ssd_norm_g.shape[1]
    cd = conv_ssd_w.shape[2]
    n_sh = dt_bias.shape[1]
    d_m = conv_m_w.shape[2]
    n_mh = w_mq.shape[1]
    dk = w_mq.shape[2]
    assert n_sh <= GATE_I0 and GATE_I0 + n_mh <= LANES and d_ssd == n_sh * SSD_HEAD_DIM
    w = w_in[l]
    offs = [0]
    for n in (d_ssd, cd, n_sh, d_m, d_m, d_m, n_mh, n_mh):
        offs.append(offs[-1] + n)
    z_w, xbc_w, dt_w, u_w, v_w, o_w, i_w, f_w = (w[:, offs[k]:offs[k + 1]] for k in range(8))
    tail = ((0, 0), (0, LANES - GATE_I0 - n_mh))
    gate_a = jnp.pad(jnp.concatenate([jnp.pad(dt_w, ((0, 0), (0, GATE_I0 - n_sh))), i_w], axis=1), tail)
    gate_b = jnp.pad(f_w, ((0, 0), (GATE_I0, LANES - GATE_I0 - n_mh)))
    w_cat = jnp.concatenate([z_w, xbc_w, u_w, v_w, o_w, gate_a, gate_b], axis=1).astype(BF16)
    widths = (d_ssd, cd, d_m, d_m, d_m, 2 * LANES)

    lane = jnp.arange(LANES)[:, None]
    ssd_expand = (lane == (jnp.arange(d_ssd)[None, :] // SSD_HEAD_DIM)).astype(BF16)
    mlstm_expand = (lane == (GATE_I0 + jnp.arange(d_m)[None, :] // dk)).astype(BF16)
    t = jnp.arange(CHUNK)
    tril = (t[:, None] >= t[None, :]).astype(BF16)
    p = {
        "w_cat": w_cat,
        "conv_ssd_w": conv_ssd_w[l], "conv_ssd_b": conv_ssd_b[l][None, :],
        "dt_bias_row": _pad_row(dt_bias[l], 0), "a_log_row": _pad_row(a_log[l], 0),
        "d_skip_row": jnp.repeat(d_skip[l], SSD_HEAD_DIM)[None, :], "ssd_norm_g": ssd_norm_g[l][None, :],
        "conv_m_w": conv_m_w[l], "conv_m_b": conv_m_b[l][None, :],
        "w_mq": w_mq[l].astype(BF16), "w_mk": w_mk[l].astype(BF16),
        "b_i_row": _pad_row(b_igate[l], GATE_I0), "b_f_row": _pad_row(b_fgate[l], GATE_I0),
        "mlstm_norm_g": mlstm_norm_g[l].reshape(1, d_m),
        "w_out": w_out[l].astype(BF16), "ln1_g": ln1_g[l][None, :], "ln1_b": ln1_b[l][None, :],
        "peer_wqT": peer_wq[l].T.astype(BF16), "peer_keys": peer_keys[l].astype(BF16),
        "peer_u": peer_u[l].astype(BF16),
        "peer_vT": peer_v[l].astype(BF16).T,
        "ln2_g_col": ln2_g[l][:, None], "ln2_b_col": ln2_b[l][:, None],
        "tril": tril, "ssd_expand": ssd_expand, "mlstm_expand": mlstm_expand,
    }
    return p, widths


def _tile_sizes(m):
    tm = 256 if m % 256 == 0 else LANES
    tb = 512 if m % 512 == 0 else LANES
    return tm, tb


def _peer_block(x1t, p, alpha):
    m = x1t.shape[1]
    tm, tb = _tile_sizes(m)
    s1, s2, st, alist, thr_rank = _peer_topk(x1t, p, tm)
    return _peer_dense(x1t, s1, s2, st, alist, thr_rank, p, alpha, tb)


def _layer_prompt(x, p, widths, alpha):
    bsz, seq, d = x.shape
    assert seq % CHUNK == 0 and seq >= CONV_K - 1
    x2 = x.reshape(bsz * seq, d)
    tm, _ = _tile_sizes(bsz * seq)
    z, xbc, u, v, o, gates = _inproj(x2, p["w_cat"], widths, tm)
    y_s, h_t = _ssd_chunk(xbc, z, gates, p, bsz, seq)
    h_m, c_new, n_new, m_new = _mlstm_chunk(u, v, o, gates, p, bsz, seq)
    x1t = _outproj(y_s, h_m, x2, p, alpha, 2 * tm if (bsz * seq) % (2 * tm) == 0 else tm)
    y = _peer_block(x1t, p, alpha).reshape(bsz, seq, d)
    n_mh = c_new.shape[1]
    d_ssd = z.shape[1]
    n_sh = d_ssd // SSD_HEAD_DIM
    hg = n_sh // SSD_GROUPS
    new_ssd = h_t.reshape(bsz, SSD_GROUPS, SSD_STATE, hg, SSD_HEAD_DIM).transpose(0, 1, 3, 4, 2)
    new_ssd = new_ssd.reshape(bsz, n_sh, SSD_HEAD_DIM, SSD_STATE)
    states = (new_ssd,
              xbc.reshape(bsz, seq, -1)[:, seq - (CONV_K - 1):, :],
              c_new,
              n_new[:, :, 0, :],
              m_new[:, 0, GATE_I0:GATE_I0 + n_mh],
              u.reshape(bsz, seq, -1)[:, seq - (CONV_K - 1):, :])
    return y, states


def _layer_sample(x, st, p, widths, alpha):
    st_ssd, st_ssd_conv, st_c, st_n, st_m, st_mconv = st
    nb, seq, d = x.shape
    assert seq == 1 and nb % LANES == 0
    x2 = x.reshape(nb, d)
    z, xbc, u, v, o, gates = _inproj(x2, p["w_cat"], widths, LANES)
    y_s, new_ssd, new_ssd_conv = _ssd_step(xbc, z, gates, st_ssd_conv, st_ssd, p)
    n_mh = st_c.shape[1]
    m_row = jnp.pad(st_m.astype(F32), ((0, 0), (GATE_I0, LANES - GATE_I0 - n_mh)))[:, None, :]
    h_m, c_new, n_new, m_new, new_mconv = _mlstm_step(u, v, o, gates, st_mconv, st_c, st_n, m_row, p)
    x1t = _outproj(y_s, h_m, x2, p, alpha, LANES)
    y = _peer_block(x1t, p, alpha).reshape(nb, seq, d)
    states = (new_ssd, new_ssd_conv, c_new, n_new, m_new[:, 0, GATE_I0:GATE_I0 + n_mh], new_mconv)
    return y, states


def kernel(x_prompt, x_sample, state_ssd, state_ssd_conv, state_mlstm_c, state_mlstm_n, state_mlstm_m,
           state_mlstm_conv, w_in, conv_ssd_w, conv_ssd_b, dt_bias, a_log, d_skip, ssd_norm_g, conv_m_w,
           conv_m_b, w_mq, w_mk, b_igate, b_fgate, mlstm_norm_g, w_out, ln1_g, ln1_b, peer_wq, peer_keys,
           peer_u, peer_v, ln2_g, ln2_b):
    depth = w_in.shape[0]
    alpha = (2.0 * depth) ** 0.25
    states = (state_ssd, state_ssd_conv, state_mlstm_c, state_mlstm_n, state_mlstm_m, state_mlstm_conv)
    yp, ys = x_prompt, x_sample
    p_list, s_list = [], []
    for l in range(depth):
        p, widths = _prep_layer(l, w_in, conv_ssd_w, conv_ssd_b, dt_bias, a_log, d_skip, ssd_norm_g, conv_m_w,
                                conv_m_b, w_mq, w_mk, b_igate, b_fgate, mlstm_norm_g, w_out, ln1_g, ln1_b,
                                peer_wq, peer_keys, peer_u, peer_v, ln2_g, ln2_b)
        yp, p_new = _layer_prompt(yp, p, widths, alpha)
        ys, s_new = _layer_sample(ys, tuple(s[l] for s in states), p, widths, alpha)
        p_list.append(p_new)
        s_list.append(s_new)
    p_out = tuple(jnp.stack([pn[i] for pn in p_list]) for i in range(6))
    s_out = tuple(jnp.stack([sn[i] for sn in s_list]) for i in range(6))
    return (yp, ys) + p_out + s_out
```

```python
import functools
import math

import jax
import jax.numpy as jnp
from jax import lax
from jax.experimental import pallas as pl
from jax.experimental.pallas import tpu as pltpu

F32 = jnp.float32
BF16 = jnp.bfloat16

LANES = 128
SUBLANES = 8
CHUNK = 128
CONV_K = 4
LN_EPS = 1e-5
VMEM_LIMIT = 58 * 1024 * 1024

SSD_HEAD_DIM = 64
SSD_STATE = 128
SSD_GROUPS = 2
PEER_TOPK = 16
GATE_I0 = 16
N_STATS = 4
PEER_CHUNK = 2048
STEP_SEQS = 4
OUTPROJ_ROWS = 256


def _cparams(*sem):
    return pltpu.CompilerParams(dimension_semantics=sem, vmem_limit_bytes=VMEM_LIMIT)


def _dot(a, b):
    return jnp.dot(a, b, preferred_element_type=F32)


def _dot_nt(a, b):
    return lax.dot_general(a, b, (((1,), (1,)), ((), ())), preferred_element_type=F32)


def _split3(x):
    h = x.astype(BF16)
    r = x - h.astype(F32)
    m = r.astype(BF16)
    lo = (r - m.astype(F32)).astype(BF16)
    return h, m, lo


def _dot_sel_l(sel_bf16, x):
    h, m, lo = _split3(x)
    return _dot(sel_bf16, h) + _dot(sel_bf16, m) + _dot(sel_bf16, lo)


def _dot_sel_r(x, sel_bf16):
    h, m, lo = _split3(x)
    return _dot(h, sel_bf16) + _dot(m, sel_bf16) + _dot(lo, sel_bf16)


def _sigmoid(x):
    return 1.0 / (1.0 + jnp.exp(-x))


def _silu(x):
    return x * _sigmoid(x)


def _softplus(x):
    return jnp.maximum(x, 0.0) + jnp.log1p(jnp.exp(-jnp.abs(x)))


def _log_sigmoid(x):
    return -_softplus(-x)


def _gelu_exact(x):
    return 0.5 * x * (1.0 + lax.erf(x * (1.0 / math.sqrt(2.0))))


def _inproj_kernel(x_ref, w_ref, *out_refs):
    xb = x_ref[...].astype(BF16)
    off = 0
    for ref in out_refs:
        n = ref.shape[-1]
        ref[...] = _dot(xb, w_ref[:, off:off + n])
        off += n


def _inproj(x2, w_cat, widths, tm):
    m, d = x2.shape
    n_all = w_cat.shape[1]
    return pl.pallas_call(
        _inproj_kernel,
        grid=(m // tm,),
        in_specs=[pl.BlockSpec((tm, d), lambda i: (i, 0)),
                  pl.BlockSpec((d, n_all), lambda i: (0, 0))],
        out_specs=[pl.BlockSpec((tm, n), lambda i: (i, 0)) for n in widths],
        out_shape=[jax.ShapeDtypeStruct((m, n), F32) for n in widths],
        compiler_params=_cparams("parallel"),
        name="inproj",
    )(x2, w_cat)


def _ssd_chunk_kernel(xbc_ref, z_ref, g_ref, cw_ref, cb_ref, dtb_ref, alog_ref, dsk_ref, ng_ref,
                      tril_ref, exp_ref, y_ref, hT_out_ref, xpad, hT, ybuf, *, nc, d_ssd, n_heads):
    c = pl.program_id(1)
    cd = xpad.shape[1]
    hg = n_heads // SSD_GROUPS
    gw = d_ssd // SSD_GROUPS

    @pl.when(c == 0)
    def _():
        xpad[0:SUBLANES, :] = jnp.zeros((SUBLANES, cd), F32)
        hT[...] = jnp.zeros_like(hT)

    xpad[SUBLANES:SUBLANES + CHUNK, :] = xbc_ref[...]
    conv = cb_ref[...]
    for j in range(CONV_K):
        r0 = SUBLANES - (CONV_K - 1) + j
        conv = conv + cw_ref[j:j + 1, :] * xpad[r0:r0 + CHUNK, :]
    xpad[0:SUBLANES, :] = xpad[CHUNK:CHUNK + SUBLANES, :]
    xc = _silu(conv)
    xs = xc[:, 0:d_ssd]

    lane = lax.broadcasted_iota(jnp.int32, (1, LANES), 1)
    a_row = jnp.where(lane < n_heads, -jnp.exp(alog_ref[...]), 0.0)
    dt = _softplus(g_ref[:, 0:LANES] + dtb_ref[...])
    acs = _dot_sel_l(tril_ref[...], dt * a_row)
    dt_t = dt.T
    acs_t = acs.T
    eacs = jnp.exp(acs)
    dec = jnp.exp(acs[CHUNK - 1:CHUNK, :] - acs) * dt
    both = _dot_sel_r(jnp.concatenate([eacs, dec], axis=0), exp_ref[...])
    eacs_full = both[0:CHUNK]
    dec_full = both[CHUNK:2 * CHUNK]

    row = lax.broadcasted_iota(jnp.int32, (CHUNK, CHUNK), 0)
    col = lax.broadcasted_iota(jnp.int32, (CHUNK, CHUNK), 1)
    causal = row >= col

    gs = range(SSD_GROUPS)
    b_g = [xc[:, d_ssd + g * SSD_STATE:d_ssd + (g + 1) * SSD_STATE] for g in gs]
    c_g = [xc[:, d_ssd + (SSD_GROUPS + g) * SSD_STATE:d_ssd + (SSD_GROUPS + g + 1) * SSD_STATE] for g in gs]
    b_gb = [x.astype(BF16) for x in b_g]
    c_gb = [x.astype(BF16) for x in c_g]
    cb = [_dot_nt(c_gb[g], b_gb[g]) for g in gs]
    h_old = [hT[g] for g in gs]
    y_inter = jnp.concatenate([_dot(c_gb[g], h_old[g].astype(BF16)) for g in gs], axis=1)
    for h in range(n_heads):
        diff = acs[:, h:h + 1] - acs_t[h:h + 1, :]
        lmat = jnp.exp(jnp.where(causal, diff, -jnp.inf))
        mm = cb[h // hg] * lmat * dt_t[h:h + 1, :]
        x_h = xs[:, h * SSD_HEAD_DIM:(h + 1) * SSD_HEAD_DIM]
        ybuf[:, h * SSD_HEAD_DIM:(h + 1) * SSD_HEAD_DIM] = _dot(mm.astype(BF16), x_h.astype(BF16))
    for g in gs:
        gsl = slice(g * gw, (g + 1) * gw)
        xd = xs[:, gsl] * dec_full[:, gsl]
        hT[g] = h_old[g] * eacs_full[CHUNK - 1:CHUNK, gsl] + _dot(b_g[g].T.astype(BF16), xd.astype(BF16))

    y = ybuf[...] + y_inter * eacs_full + dsk_ref[...] * xs
    y = y * _silu(z_ref[...])
    y = y * lax.rsqrt(jnp.mean(y * y, axis=-1, keepdims=True) + LN_EPS) * ng_ref[...]
    y_ref[...] = y.astype(y_ref.dtype)

    @pl.when(c == nc - 1)
    def _():
        hT_out_ref[0] = hT[...]


def _ssd_chunk(xbc, z, gates, p, bsz, seq):
    nc = seq // CHUNK
    cd = xbc.shape[1]
    d_ssd = z.shape[1]
    n_heads = d_ssd // SSD_HEAD_DIM
    gw = d_ssd // SSD_GROUPS
    tok = lambda b, c: (b * nc + c, 0)
    cst = lambda b, c: (0, 0)
    return pl.pallas_call(
        functools.partial(_ssd_chunk_kernel, nc=nc, d_ssd=d_ssd, n_heads=n_heads),
        grid=(bsz, nc),
        in_specs=[pl.BlockSpec((CHUNK, cd), tok),
                  pl.BlockSpec((CHUNK, d_ssd), tok),
                  pl.BlockSpec((CHUNK, 2 * LANES), tok),
                  pl.BlockSpec((CONV_K, cd), cst),
                  pl.BlockSpec((1, cd), cst),
                  pl.BlockSpec((1, LANES), cst),
                  pl.BlockSpec((1, LANES), cst),
                  pl.BlockSpec((1, d_ssd), cst),
                  pl.BlockSpec((1, d_ssd), cst),
                  pl.BlockSpec((CHUNK, CHUNK), cst),
                  pl.BlockSpec((LANES, d_ssd), cst)],
        out_specs=[pl.BlockSpec((CHUNK, d_ssd), tok),
                   pl.BlockSpec((1, SSD_GROUPS, SSD_STATE, gw), lambda b, c: (b, 0, 0, 0))],
        out_shape=[jax.ShapeDtypeStruct((bsz * seq, d_ssd), BF16),
                   jax.ShapeDtypeStruct((bsz, SSD_GROUPS, SSD_STATE, gw), F32)],
        scratch_shapes=[pltpu.VMEM((CHUNK + SUBLANES, cd), F32),
                        pltpu.VMEM((SSD_GROUPS, SSD_STATE, gw), F32),
                        pltpu.VMEM((CHUNK, d_ssd), F32)],
        compiler_params=_cparams("parallel", "arbitrary"),
        name="ssd_chunk",
    )(xbc, z, gates, p["conv_ssd_w"], p["conv_ssd_b"], p["dt_bias_row"], p["a_log_row"],
      p["d_skip_row"], p["ssd_norm_g"], p["tril"], p["ssd_expand"])


def _mlstm_chunk_kernel(u_ref, v_ref, o_ref, g_ref, cw_ref, cb_ref, wq_ref, wk_ref, bi_ref, bf_ref, ng_ref,
                        tril_ref, y_ref, c_out_ref, n_out_ref, m_out_ref, upad, c_st, n_st, m_st,
                        *, nc, n_heads, dk):
    c = pl.program_id(1)
    dm = upad.shape[1]

    @pl.when(c == 0)
    def _():
        upad[0:SUBLANES, :] = jnp.zeros((SUBLANES, dm), F32)
        c_st[...] = jnp.zeros_like(c_st)
        n_st[...] = jnp.zeros_like(n_st)
        m_st[...] = jnp.zeros_like(m_st)

    upad[SUBLANES:SUBLANES + CHUNK, :] = u_ref[...]
    conv = cb_ref[...]
    for j in range(CONV_K):
        r0 = SUBLANES - (CONV_K - 1) + j
        conv = conv + cw_ref[j:j + 1, :] * upad[r0:r0 + CHUNK, :]
    upad[0:SUBLANES, :] = upad[CHUNK:CHUNK + SUBLANES, :]
    uc = _silu(conv).astype(BF16)

    logi = g_ref[:, 0:LANES] + bi_ref[...]
    logf = _log_sigmoid(g_ref[:, LANES:2 * LANES] + bf_ref[...])
    bcs = _dot_sel_l(tril_ref[...], logf)
    m_prev = m_st[0:1, :]
    b_end = bcs[CHUNK - 1:CHUNK, :]
    g_all = b_end - bcs + logi
    m_new = jnp.maximum(b_end + m_prev, jnp.max(g_all, axis=0, keepdims=True))
    w_old = jnp.exp(b_end + m_prev - m_new)
    w_s_all = jnp.exp(g_all - m_new)
    inter_all = bcs + m_prev
    bcs_t = bcs.T
    logi_t = logi.T

    row = lax.broadcasted_iota(jnp.int32, (CHUNK, CHUNK), 0)
    col = lax.broadcasted_iota(jnp.int32, (CHUNK, CHUNK), 1)
    causal = row >= col
    scale = dk ** -0.5

    hs = range(n_heads)
    sls = [slice(h * dk, (h + 1) * dk) for h in hs]
    gls = [GATE_I0 + h for h in hs]
    q = [_dot(uc[:, sls[h]], wq_ref[h]) for h in hs]
    k = [_dot(uc[:, sls[h]], wk_ref[h]) * scale for h in hs]
    qb = [x.astype(BF16) for x in q]
    kb = [x.astype(BF16) for x in k]
    vb = [v_ref[:, sls[h]].astype(BF16) for h in hs]
    dmat = [jnp.where(causal, bcs[:, g:g + 1] - bcs_t[g:g + 1, :] + logi_t[g:g + 1, :], -jnp.inf) for g in gls]
    inter = [inter_all[:, g:g + 1] for g in gls]
    m_t = [jnp.maximum(inter[h], jnp.max(dmat[h], axis=1, keepdims=True)) for h in hs]
    w_inter = [jnp.exp(inter[h] - m_t[h]) for h in hs]
    qk = [_dot_nt(qb[h], kb[h]) for h in hs]
    c_old = [c_st[h] for h in hs]
    n_old = [n_st[h][0:1, :] for h in hs]
    qc = [_dot(qb[h], c_old[h].astype(BF16)) for h in hs]
    att = [qk[h] * jnp.exp(dmat[h] - m_t[h]) for h in hs]
    av = [_dot(att[h].astype(BF16), vb[h]) for h in hs]
    kw = [k[h] * w_s_all[:, gls[h]:gls[h] + 1] for h in hs]
    kv = [_dot(kw[h].T.astype(BF16), vb[h]) for h in hs]
    num = [av[h] + w_inter[h] * qc[h] for h in hs]
    den = [jnp.sum(att[h], axis=1, keepdims=True) + w_inter[h] * jnp.sum(q[h] * n_old[h], axis=1, keepdims=True)
           for h in hs]
    hout = [num[h] / jnp.maximum(jnp.abs(den[h]), jnp.exp(-m_t[h])) for h in hs]
    mu = [jnp.mean(x, axis=-1, keepdims=True) for x in hout]
    cen = [hout[h] - mu[h] for h in hs]
    var = [jnp.mean(jnp.square(x), axis=-1, keepdims=True) for x in cen]
    for h in hs:
        sl = sls[h]
        hn = cen[h] * lax.rsqrt(var[h] + LN_EPS) * ng_ref[:, sl]
        y_ref[:, sl] = (_sigmoid(o_ref[:, sl]) * hn).astype(y_ref.dtype)
        wo = w_old[:, gls[h]:gls[h] + 1]
        c_st[h] = wo * c_old[h] + kv[h]
        n_st[h] = jnp.broadcast_to(wo * n_old[h] + jnp.sum(kw[h], axis=0, keepdims=True), (SUBLANES, dk))

    m_st[...] = jnp.broadcast_to(m_new, (SUBLANES, LANES))

    @pl.when(c == nc - 1)
    def _():
        c_out_ref[0] = c_st[...]
        n_out_ref[0] = n_st[...]
        m_out_ref[0] = m_st[...]


def _mlstm_chunk(u, v, o, gates, p, bsz, seq):
    nc = seq // CHUNK
    dm = u.shape[1]
    n_heads, dk, _ = p["w_mq"].shape
    tok = lambda b, c: (b * nc + c, 0)
    cst = lambda b, c: (0, 0)
    cst3 = lambda b, c: (0, 0, 0)
    return pl.pallas_call(
        functools.partial(_mlstm_chunk_kernel, nc=nc, n_heads=n_heads, dk=dk),
        grid=(bsz, nc),
        in_specs=[pl.BlockSpec((CHUNK, dm), tok),
                  pl.BlockSpec((CHUNK, dm), tok),
                  pl.BlockSpec((CHUNK, dm), tok),
                  pl.BlockSpec((CHUNK, 2 * LANES), tok),
                  pl.BlockSpec((CONV_K, dm), cst),
                  pl.BlockSpec((1, dm), cst),
                  pl.BlockSpec((n_heads, dk, dk), cst3),
                  pl.BlockSpec((n_heads, dk, dk), cst3),
                  pl.BlockSpec((1, LANES), cst),
                  pl.BlockSpec((1, LANES), cst),
                  pl.BlockSpec((1, dm), cst),
                  pl.BlockSpec((CHUNK, CHUNK), cst)],
        out_specs=[pl.BlockSpec((CHUNK, dm), tok),
                   pl.BlockSpec((1, n_heads, dk, dk), lambda b, c: (b, 0, 0, 0)),
                   pl.BlockSpec((1, n_heads, SUBLANES, dk), lambda b, c: (b, 0, 0, 0)),
                   pl.BlockSpec((1, SUBLANES, LANES), lambda b, c: (b, 0, 0))],
        out_shape=[jax.ShapeDtypeStruct((bsz * seq, dm), BF16),
                   jax.ShapeDtypeStruct((bsz, n_heads, dk, dk), F32),
                   jax.ShapeDtypeStruct((bsz, n_heads, SUBLANES, dk), F32),
                   jax.ShapeDtypeStruct((bsz, SUBLANES, LANES), F32)],
        scratch_shapes=[pltpu.VMEM((CHUNK + SUBLANES, dm), F32),
                        pltpu.VMEM((n_heads, dk, dk), F32),
                        pltpu.VMEM((n_heads, SUBLANES, dk), F32),
                        pltpu.VMEM((SUBLANES, LANES), F32)],
        compiler_params=_cparams("parallel", "arbitrary"),
        name="mlstm_chunk",
    )(u, v, o, gates, p["conv_m_w"], p["conv_m_b"], p["w_mq"], p["w_mk"], p["b_i_row"], p["b_f_row"],
      p["mlstm_norm_g"], p["tril"])


def _col_bcast(row, n_rows):
    return jnp.broadcast_to(row, (LANES, row.shape[1])).T[0:n_rows, :]


def _per_sequence(one_sequence, n_seq_in, n_seq_out):
    def kernel(*refs, **static):
        seq_in, shared, seq_out = refs[:n_seq_in], refs[n_seq_in:len(refs) - n_seq_out], refs[len(refs) - n_seq_out:]
        for s in range(seq_in[0].shape[0]):
            one = slice(s, s + 1)
            one_sequence(*(r.at[one] for r in seq_in), *shared, *(r.at[one] for r in seq_out), **static)
    return kernel


def _ssd_step_one(xbc_ref, z_ref, g_ref, buf_ref, st_ref, cw_ref, cb_ref, dtb_ref, alog_ref, dsk_ref, ng_ref,
                  exp_ref, y_ref, st_out_ref, buf_out_ref, *, d_ssd, n_heads):
    gw = d_ssd // SSD_GROUPS
    x_row = xbc_ref[0]
    buf_out_ref[0, 0:CONV_K - 2, :] = buf_ref[0, 1:CONV_K - 1, :]
    buf_out_ref[0, CONV_K - 2:CONV_K - 1, :] = x_row
    conv = cb_ref[...] + cw_ref[CONV_K - 1:CONV_K, :] * x_row
    for j in range(CONV_K - 1):
        conv = conv + cw_ref[j:j + 1, :] * buf_ref[0, j:j + 1, :]
    xc = _silu(conv)
    xs = xc[:, 0:d_ssd]

    lane = lax.broadcasted_iota(jnp.int32, (1, LANES), 1)
    a_row = jnp.where(lane < n_heads, -jnp.exp(alog_ref[...]), 0.0)
    dt = _softplus(g_ref[0][:, 0:LANES] + dtb_ref[...])
    dec = jnp.exp(dt * a_row)
    both = _dot_sel_r(jnp.concatenate([jnp.broadcast_to(dt, (SUBLANES, LANES)),
                                       jnp.broadcast_to(dec, (SUBLANES, LANES))], axis=0), exp_ref[...])
    dt_full = both[0:1]
    dec_full = both[SUBLANES:SUBLANES + 1]
    dtx = dt_full * xs
    dtx_col = _col_bcast(dtx, d_ssd)
    dec_col = jnp.concatenate([jnp.broadcast_to(dec[:, h:h + 1], (SSD_HEAD_DIM, LANES)) for h in range(n_heads)],
                              axis=0)

    y_parts = []
    for g in range(SSD_GROUPS):
        b_g = xc[:, d_ssd + g * SSD_STATE:d_ssd + (g + 1) * SSD_STATE]
        c_g = xc[:, d_ssd + (SSD_GROUPS + g) * SSD_STATE:d_ssd + (SSD_GROUPS + g + 1) * SSD_STATE]
        h_g = st_ref[0, g * gw:(g + 1) * gw, :]
        cb = jnp.sum(c_g * b_g, axis=-1, keepdims=True)
        c8 = jnp.broadcast_to(c_g, (SUBLANES, SSD_STATE)).astype(BF16)
        y_int = _dot_nt(c8, h_g.astype(BF16))[0:1]
        sl = slice(g * gw, (g + 1) * gw)
        y_parts.append(cb * dtx[:, sl] + dec_full[:, sl] * y_int)
        st_out_ref[0, sl, :] = h_g * dec_col[sl, :] + dtx_col[sl, :] * b_g
    y = jnp.concatenate(y_parts, axis=1) + dsk_ref[...] * xs
    y = y * _silu(z_ref[0])
    y = y * lax.rsqrt(jnp.mean(y * y, axis=-1, keepdims=True) + LN_EPS) * ng_ref[...]
    y_ref[0] = y.astype(y_ref.dtype)


def _ssd_step(xbc, z, gates, buf, state, p):
    nb, cd = xbc.shape
    d_ssd = z.shape[1]
    n_heads = d_ssd // SSD_HEAD_DIM
    row3 = lambda b: (b, 0, 0)
    cst = lambda b: (0, 0)
    st2 = state.reshape(nb, d_ssd, SSD_STATE)
    sb = STEP_SEQS
    assert nb % sb == 0
    y, st_new, buf_new = pl.pallas_call(
        functools.partial(_per_sequence(_ssd_step_one, 5, 3), d_ssd=d_ssd, n_heads=n_heads),
        grid=(nb // sb,),
        in_specs=[pl.BlockSpec((sb, 1, cd), row3),
                  pl.BlockSpec((sb, 1, d_ssd), row3),
                  pl.BlockSpec((sb, 1, 2 * LANES), row3),
                  pl.BlockSpec((sb, CONV_K - 1, cd), row3),
                  pl.BlockSpec((sb, d_ssd, SSD_STATE), row3),
                  pl.BlockSpec((CONV_K, cd), cst),
                  pl.BlockSpec((1, cd), cst),
                  pl.BlockSpec((1, LANES), cst),
                  pl.BlockSpec((1, LANES), cst),
                  pl.BlockSpec((1, d_ssd), cst),
                  pl.BlockSpec((1, d_ssd), cst),
                  pl.BlockSpec((LANES, d_ssd), cst)],
        out_specs=[pl.BlockSpec((sb, 1, d_ssd), row3),
                   pl.BlockSpec((sb, d_ssd, SSD_STATE), row3),
                   pl.BlockSpec((sb, CONV_K - 1, cd), row3)],
        out_shape=[jax.ShapeDtypeStruct((nb, 1, d_ssd), BF16),
                   jax.ShapeDtypeStruct((nb, d_ssd, SSD_STATE), F32),
                   jax.ShapeDtypeStruct((nb, CONV_K - 1, cd), F32)],
        compiler_params=_cparams("parallel"),
        name="ssd_step",
    )(xbc.reshape(nb, 1, cd), z.reshape(nb, 1, d_ssd), gates.reshape(nb, 1, 2 * LANES), buf, st2,
      p["conv_ssd_w"], p["conv_ssd_b"], p["dt_bias_row"], p["a_log_row"], p["d_skip_row"], p["ssd_norm_g"],
      p["ssd_expand"])
    return y.reshape(nb, d_ssd), st_new.reshape(state.shape), buf_new


def _mlstm_step_one(u_ref, v_ref, o_ref, g_ref, buf_ref, c_ref, n_ref, m_ref, cw_ref, cb_ref, wq_ref, wk_ref,
                    bi_ref, bf_ref, ng_ref, hexp_ref, y_ref, c_out_ref, n_out_ref, m_out_ref, buf_out_ref,
                    *, n_heads, dk):
    buf_out_ref[0, 0:CONV_K - 2, :] = buf_ref[0, 1:CONV_K - 1, :]
    buf_out_ref[0, CONV_K - 2:CONV_K - 1, :] = u_ref[0]
    conv = cb_ref[...] + cw_ref[CONV_K - 1:CONV_K, :] * u_ref[0]
    for j in range(CONV_K - 1):
        conv = conv + cw_ref[j:j + 1, :] * buf_ref[0, j:j + 1, :]
    uc = jnp.broadcast_to(_silu(conv), (SUBLANES, conv.shape[1])).astype(BF16)

    gt = g_ref[0]
    logi = gt[:, 0:LANES] + bi_ref[...]
    logf = _log_sigmoid(gt[:, LANES:2 * LANES] + bf_ref[...])
    m_prev = m_ref[0]
    inter = logf + m_prev
    m_t = jnp.maximum(inter, logi)
    w_inter = jnp.exp(inter - m_t)
    w_in = jnp.exp(logi - m_t)
    floor = jnp.exp(-m_t)
    rows = jnp.concatenate([w_inter, w_in, floor, jnp.zeros((SUBLANES - 3, LANES), F32)], axis=0)
    full = _dot_sel_r(rows, hexp_ref[...])
    scale = dk ** -0.5

    for h in range(n_heads):
        sl = slice(h * dk, (h + 1) * dk)
        q = _dot(uc[:, sl], wq_ref[h])[0:1]
        k = _dot(uc[:, sl], wk_ref[h])[0:1] * scale
        v_h = v_ref[0][:, sl]
        wi = full[0:1, sl]
        ws = full[1:2, sl]
        fl = full[2:3, sl]
        c_h = c_ref[0, h]
        n_h = n_ref[0, h:h + 1, :]
        att = jnp.sum(q * k, axis=-1, keepdims=True) * ws
        q8 = jnp.broadcast_to(q, (SUBLANES, dk)).astype(BF16)
        num = att * v_h + wi * _dot(q8, c_h.astype(BF16))[0:1]
        den = att + wi * jnp.sum(q * n_h, axis=-1, keepdims=True)
        hout = num / jnp.maximum(jnp.abs(den), fl)
        mu = jnp.mean(hout, axis=-1, keepdims=True)
        var = jnp.mean(jnp.square(hout - mu), axis=-1, keepdims=True)
        hn = (hout - mu) * lax.rsqrt(var + LN_EPS) * ng_ref[:, sl]
        y_ref[0, :, sl] = (_sigmoid(o_ref[0][:, sl]) * hn).astype(y_ref.dtype)
        kw = k * ws
        kw_col = _col_bcast(kw, dk)
        wi_full = jnp.broadcast_to(wi, (dk, dk))
        for half in range(dk // LANES):
            hs = slice(half * LANES, (half + 1) * LANES)
            c_out_ref[0, h, :, hs] = wi_full[:, hs] * c_h[:, hs] + kw_col * v_h[:, hs]
        n_out_ref[0, h:h + 1, :] = wi * n_h + kw
    m_out_ref[0] = m_t


def _mlstm_step(u, v, o, gates, buf, c_st, n_st, m_row, p):
    nb, dm = u.shape
    n_heads, dk, _ = p["w_mq"].shape
    row3 = lambda b: (b, 0, 0)
    cst = lambda b: (0, 0)
    cst3 = lambda b: (0, 0, 0)
    r3 = lambda a: a.reshape(nb, 1, a.shape[-1])
    sb = STEP_SEQS
    assert nb % sb == 0
    y, c_new, n_new, m_new, buf_new = pl.pallas_call(
        functools.partial(_per_sequence(_mlstm_step_one, 8, 5), n_heads=n_heads, dk=dk),
        grid=(nb // sb,),
        in_specs=[pl.BlockSpec((sb, 1, dm), row3),
                  pl.BlockSpec((sb, 1, dm), row3),
                  pl.BlockSpec((sb, 1, dm), row3),
                  pl.BlockSpec((sb, 1, 2 * LANES), row3),
                  pl.BlockSpec((sb, CONV_K - 1, dm), row3),
                  pl.BlockSpec((sb, n_heads, dk, dk), lambda b: (b, 0, 0, 0)),
                  pl.BlockSpec((sb, n_heads, dk), row3),
                  pl.BlockSpec((sb, 1, LANES), row3),
                  pl.BlockSpec((CONV_K, dm), cst),
                  pl.BlockSpec((1, dm), cst),
                  pl.BlockSpec((n_heads, dk, dk), cst3),
                  pl.BlockSpec((n_heads, dk, dk), cst3),
                  pl.BlockSpec((1, LANES), cst),
                  pl.BlockSpec((1, LANES), cst),
                  pl.BlockSpec((1, dm), cst),
                  pl.BlockSpec((LANES, dm), cst)],
        out_specs=[pl.BlockSpec((sb, 1, dm), row3),
                   pl.BlockSpec((sb, n_heads, dk, dk), lambda b: (b, 0, 0, 0)),
                   pl.BlockSpec((sb, n_heads, dk), row3),
                   pl.BlockSpec((sb, 1, LANES), row3),
                   pl.BlockSpec((sb, CONV_K - 1, dm), row3)],
        out_shape=[jax.ShapeDtypeStruct((nb, 1, dm), BF16),
                   jax.ShapeDtypeStruct(c_st.shape, F32),
                   jax.ShapeDtypeStruct(n_st.shape, F32),
                   jax.ShapeDtypeStruct((nb, 1, LANES), F32),
                   jax.ShapeDtypeStruct((nb, CONV_K - 1, dm), F32)],
        compiler_params=_cparams("parallel"),
        name="mlstm_step",
    )(r3(u), r3(v), r3(o), r3(gates), buf, c_st, n_st, m_row, p["conv_m_w"], p["conv_m_b"], p["w_mq"],
      p["w_mk"], p["b_i_row"], p["b_f_row"], p["mlstm_norm_g"], p["mlstm_expand"])
    return y.reshape(nb, dm), c_new, n_new, m_new, buf_new


def _outproj_kernel(ys_ref, hm_ref, x_ref, w_ref, g_ref, b_ref, x1t_ref, *, alpha):
    d_ssd = ys_ref.shape[1]
    tm = ys_ref.shape[0]
    subs = [slice(r0, min(r0 + OUTPROJ_ROWS, tm)) for r0 in range(0, tm, OUTPROJ_ROWS)]
    mix = [_dot(ys_ref[rs, :], w_ref[0:d_ssd, :]) + _dot(hm_ref[rs, :], w_ref[d_ssd:, :]) for rs in subs]
    for rs, mx in zip(subs, mix):
        r = alpha * x_ref[rs, :] + mx
        mu = jnp.mean(r, axis=-1, keepdims=True)
        var = jnp.mean(jnp.square(r - mu), axis=-1, keepdims=True)
        x1 = (r - mu) * lax.rsqrt(var + LN_EPS) * g_ref[...] + b_ref[...]
        x1t_ref[:, rs] = x1.T


def _outproj(ys, hm, x2, p, alpha, tm):
    m, d = x2.shape
    d_ssd, d_m = ys.shape[1], hm.shape[1]
    return pl.pallas_call(
        functools.partial(_outproj_kernel, alpha=alpha),
        grid=(m // tm,),
        in_specs=[pl.BlockSpec((tm, d_ssd), lambda i: (i, 0)),
                  pl.BlockSpec((tm, d_m), lambda i: (i, 0)),
                  pl.BlockSpec((tm, d), lambda i: (i, 0)),
                  pl.BlockSpec((d_ssd + d_m, d), lambda i: (0, 0)),
                  pl.BlockSpec((1, d), lambda i: (0, 0)),
                  pl.BlockSpec((1, d), lambda i: (0, 0))],
        out_specs=pl.BlockSpec((d, tm), lambda i: (0, i)),
        out_shape=jax.ShapeDtypeStruct((d, m), F32),
        compiler_params=_cparams("parallel"),
        name="outproj_ln1",
    )(ys, hm, x2, p["w_out"], p["ln1_g"], p["ln1_b"])


def _oddeven_merge_sort_pairs(n):
    pairs = []
    p = 1
    while p < n:
        k = p
        while k >= 1:
            for j in range(k % p, n - k, 2 * k):
                for i in range(min(k, n - j - k)):
                    if (i + j) // (2 * p) == (i + j + k) // (2 * p):
                        pairs.append((i + j, i + j + k))
            k //= 2
        p *= 2
    return pairs


_SORT_TOPK = _oddeven_merge_sort_pairs(PEER_TOPK)


def _compare_exchange(lst, i, j):
    hi, lo = jnp.maximum(lst[i], lst[j]), jnp.minimum(lst[i], lst[j])
    lst[i], lst[j] = hi, lo


def _bitonic_to_sorted(t):
    d = PEER_TOPK // 2
    while d >= 1:
        for i in range(PEER_TOPK):
            if i & d == 0:
                _compare_exchange(t, i, i + d)
        d //= 2
    return t


def _merge_top(x, y):
    neg = jnp.full_like(x[0], -jnp.inf)
    x = x + [neg] * (PEER_TOPK - len(x))
    y = y + [neg] * (PEER_TOPK - len(y))
    return _bitonic_to_sorted([jnp.maximum(x[k], y[PEER_TOPK - 1 - k]) for k in range(PEER_TOPK)])


def _top_rows(s):
    lst = [s[SUBLANES * g:SUBLANES * (g + 1), :] for g in range(PEER_TOPK)]
    for i, j in _SORT_TOPK:
        _compare_exchange(lst, i, j)
    shift = SUBLANES // 2
    while shift >= 1:
        lst = _merge_top(lst, [pltpu.roll(a, shift, 0) for a in lst])
        shift //= 2
    return lst


def _peer_topk_kernel(x1t_ref, wqt_ref, keys_ref, s1_ref, s2_ref, st_ref, a_scr, thr_ref, q_scr, b_scr, *, n_heads):
    tq = x1t_ref.shape[1]
    q_scr[...] = _dot(wqt_ref[...], x1t_ref[...].astype(BF16))
    half = q_scr.shape[0] // (2 * n_heads)

    for h in range(n_heads):
        for side, (s_ref, top_scr) in enumerate(((s1_ref, a_scr), (s2_ref, b_scr))):
            r0 = (2 * h + side) * half
            s = _dot(keys_ref[h, side], q_scr[r0:r0 + half, :].astype(BF16))
            for t in range(tq // LANES):
                ls = slice(t * LANES, (t + 1) * LANES)
                s_ref[h, t] = s[:, ls]
                top = _top_rows(s[:, ls])
                for k in range(PEER_TOPK):
                    top_scr[k, h:h + 1, ls] = top[k][0:1, :]

    for t in range(tq // LANES):
        ls = slice(t * LANES, (t + 1) * LANES)
        a = [a_scr[k, :, ls] for k in range(PEER_TOPK)]
        b = [b_scr[k, :, ls] for k in range(PEER_TOPK)]
        single = PEER_TOPK // 2
        lists = [[a[i] + b[j] for j in range(PEER_TOPK // (i + 1))] for i in range(single)]
        lists.append([a[i] + b[0] for i in range(single, PEER_TOPK)])
        merged = lists[0]
        for other in lists[1:]:
            merged = _merge_top(merged, other)
        tau = merged[PEER_TOPK - 1]
        top = a[0] + b[0]
        zsum = jnp.zeros_like(tau)
        for cand in (c for lst in lists for c in lst):
            zsum = zsum + jnp.where(cand >= tau, jnp.exp(cand - top), 0.0)
        st_ref[0, :, ls] = tau
        st_ref[1, :, ls] = a[0]
        st_ref[2, :, ls] = b[0]
        st_ref[3, :, ls] = 1.0 / zsum
        for i in range(PEER_TOPK):
            sums = lists[i] if i < single else [lists[single][i - single]]
            thr = jnp.full_like(tau, jnp.inf)
            for k, cand in enumerate(sums):
                thr = jnp.where(cand >= tau, b[k], thr)
            thr_ref[i, :, ls] = thr


def _peer_topk(x1t, p, tq):
    d, m = x1t.shape
    n_heads, _, n_keys, half = p["peer_keys"].shape
    nq = p["peer_wqT"].shape[0]
    assert n_heads == SUBLANES and n_keys == PEER_TOPK * SUBLANES
    keyb = lambda i: (0, i, 0, 0)
    return pl.pallas_call(
        functools.partial(_peer_topk_kernel, n_heads=n_heads),
        grid=(m // tq,),
        in_specs=[pl.BlockSpec((d, tq), lambda i: (0, i)),
                  pl.BlockSpec((nq, d), lambda i: (0, 0)),
                  pl.BlockSpec((n_heads, 2, n_keys, half), lambda i: (0, 0, 0, 0))],
        out_specs=[pl.BlockSpec((n_heads, tq // LANES, n_keys, LANES), keyb),
                   pl.BlockSpec((n_heads, tq // LANES, n_keys, LANES), keyb),
                   pl.BlockSpec((N_STATS, SUBLANES, tq), lambda i: (0, 0, i)),
                   pl.BlockSpec((PEER_TOPK, SUBLANES, tq), lambda i: (0, 0, i)),
                   pl.BlockSpec((PEER_TOPK, SUBLANES, tq), lambda i: (0, 0, i))],
        out_shape=[jax.ShapeDtypeStruct((n_heads, m // LANES, n_keys, LANES), F32),
                   jax.ShapeDtypeStruct((n_heads, m // LANES, n_keys, LANES), F32),
                   jax.ShapeDtypeStruct((N_STATS, SUBLANES, m), F32),
                   jax.ShapeDtypeStruct((PEER_TOPK, SUBLANES, m), F32),
                   jax.ShapeDtypeStruct((PEER_TOPK, SUBLANES, m), F32)],
        scratch_shapes=[pltpu.VMEM((nq, tq), F32),
                        pltpu.VMEM((PEER_TOPK, SUBLANES, tq), F32)],
        compiler_params=_cparams("parallel"),
        name="peer_topk",
    )(x1t, p["peer_wqT"], p["peer_keys"])


def _peer_dense_kernel(x1t_ref, s1_ref, s2_ref, st_ref, al_ref, tr_ref, u_ref, vt_ref, g_ref, b_ref, out_ref,
                       xb_scr, e2_scr, c_scr, thr_scr, acc_scr, s_scr, w_scr, thr_r, cr, *, alpha, n_heads):
    j = pl.program_id(1)
    ec = u_ref.shape[0]
    tb = x1t_ref.shape[1]
    nt = tb // LANES
    n_keys = s1_ref.shape[2]
    na = ec // n_keys

    @pl.when(j == 0)
    def _():
        xb_scr[...] = x1t_ref[...].astype(BF16)
        for h in range(n_heads):
            for t in range(nt):
                ls = slice(t * LANES, (t + 1) * LANES)
                e2_scr[h, t] = jnp.exp(s2_ref[h, t] - st_ref[2, h:h + 1, ls])
                s1 = s1_ref[h, t]
                c_scr[h, t] = jnp.exp(s1 - st_ref[1, h:h + 1, ls]) * (0.5 * st_ref[3, h:h + 1, ls])
                thr = jnp.full((n_keys, LANES), jnp.inf, F32)
                for k in range(PEER_TOPK):
                    thr = jnp.where(s1 == al_ref[k, h:h + 1, ls], tr_ref[k, h:h + 1, ls], thr)
                thr_scr[h, t] = thr
        acc_scr[...] = jnp.zeros_like(acc_scr)

    s_scr[:, 0:tb] = _dot(u_ref[...], xb_scr[...])

    g0 = pl.multiple_of(j * na, na)
    for h in range(n_heads):
        for t in range(nt):
            ls = slice(t * LANES, (t + 1) * LANES)
            thr_blk = thr_scr[h, t, pl.ds(g0, na), :]
            c_blk = c_scr[h, t, pl.ds(g0, na), :]
            for al in range(na):
                thr_r[h * na + al, :, ls] = thr_blk[al:al + 1, :]
                cr[h * na + al, :, ls] = c_blk[al:al + 1, :]

    def first_key(al, carry):
        r0 = pl.multiple_of(al * n_keys, n_keys)
        for t in range(nt):
            ls = slice(t * LANES, (t + 1) * LANES)
            gate = jnp.zeros((n_keys, LANES), F32)
            for h in range(n_heads):
                gate = gate + jnp.where(s2_ref[h, t] >= thr_r[h * na + al][:, ls],
                                        e2_scr[h, t] * cr[h * na + al][:, ls], 0.0)
            x = s_scr[pl.ds(r0, n_keys), ls]
            hid = x + x * lax.erf(x * (1.0 / math.sqrt(2.0)))
            w_scr[pl.ds(r0, n_keys), ls] = (gate * hid).astype(BF16)
        return carry

    lax.fori_loop(0, na, first_key, 0)
    acc_scr[...] += _dot(vt_ref[...], w_scr[:, 0:tb])

    @pl.when(j == pl.num_programs(1) - 1)
    def _():
        r = alpha * x1t_ref[...] + acc_scr[...]
        mu = jnp.mean(r, axis=0, keepdims=True)
        var = jnp.mean(jnp.square(r - mu), axis=0, keepdims=True)
        y = (r - mu) * lax.rsqrt(var + LN_EPS) * g_ref[...] + b_ref[...]
        out_ref[...] = y.T


def _peer_dense(x1t, s1, s2, st, alist, thr_rank, p, alpha, tb):
    d, m = x1t.shape
    n_heads, _, n_keys, _ = s1.shape
    n_exp = p["peer_u"].shape[0]
    ec = PEER_CHUNK
    nj = n_exp // ec
    na = ec // n_keys
    assert na % SUBLANES == 0
    nt = tb // LANES
    keyb = lambda i, j: (0, i, 0, 0)
    tokb = lambda i, j: (0, 0, i)
    pitch = tb + LANES
    return pl.pallas_call(
        functools.partial(_peer_dense_kernel, alpha=alpha, n_heads=n_heads),
        grid=(m // tb, nj),
        in_specs=[pl.BlockSpec((d, tb), lambda i, j: (0, i)),
                  pl.BlockSpec((n_heads, nt, n_keys, LANES), keyb),
                  pl.BlockSpec((n_heads, nt, n_keys, LANES), keyb),
                  pl.BlockSpec((N_STATS, SUBLANES, tb), tokb),
                  pl.BlockSpec((PEER_TOPK, SUBLANES, tb), tokb),
                  pl.BlockSpec((PEER_TOPK, SUBLANES, tb), tokb),
                  pl.BlockSpec((ec, d), lambda i, j: (j, 0)),
                  pl.BlockSpec((d, ec), lambda i, j: (0, j)),
                  pl.BlockSpec((d, 1), lambda i, j: (0, 0)),
                  pl.BlockSpec((d, 1), lambda i, j: (0, 0))],
        out_specs=pl.BlockSpec((tb, d), lambda i, j: (i, 0)),
        out_shape=jax.ShapeDtypeStruct((m, d), F32),
        scratch_shapes=[pltpu.VMEM((d, tb), BF16),
                        pltpu.VMEM((n_heads, nt, n_keys, LANES), F32),
                        pltpu.VMEM((n_heads, nt, n_keys, LANES), F32),
                        pltpu.VMEM((n_heads, nt, n_keys, LANES), F32),
                        pltpu.VMEM((d, tb), F32),
                        pltpu.VMEM((ec, pitch), F32),
                        pltpu.VMEM((ec, pitch), BF16),
                        pltpu.VMEM((n_heads * na, 1, tb), F32),
                        pltpu.VMEM((n_heads * na, 1, tb), F32)],
        compiler_params=_cparams("parallel", "arbitrary"),
        name="peer_dense",
    )(x1t, s1, s2, st, alist, thr_rank, p["peer_u"], p["peer_vT"], p["ln2_g_col"], p["ln2_b_col"])


def _pad_row(vec, offset):
    return jnp.pad(vec.astype(F32), (offset, LANES - offset - vec.shape[0]))[None, :]


def _prep_layer(l, w_in, conv_ssd_w, conv_ssd_b, dt_bias, a_log, d_skip, ssd_norm_g, conv_m_w, conv_m_b,
                w_mq, w_mk, b_igate, b_fgate, mlstm_norm_g, w_out, ln1_g, ln1_b, peer_wq, peer_keys,
                peer_u, peer_v, ln2_g, ln2_b):
    d_model = w_in.shape[1]
    d_ssd = ssd_norm_g.shape[1]
    cd = conv_ssd_w.shape[2]
    n_sh = dt_bias.shape[1]
    d_m = conv_m_w.shape[2]
    n_mh = w_mq.shape[1]
    dk = w_mq.shape[2]
    assert n_sh <= GATE_I0 and GATE_I0 + n_mh <= LANES and d_ssd == n_sh * SSD_HEAD_DIM
    w = w_in[l]
    offs = [0]
    for n in (d_ssd, cd, n_sh, d_m, d_m, d_m, n_mh, n_mh):
        offs.append(offs[-1] + n)
    z_w, xbc_w, dt_w, u_w, v_w, o_w, i_w, f_w = (w[:, offs[k]:offs[k + 1]] for k in range(8))
    tail = ((0, 0), (0, LANES - GATE_I0 - n_mh))
    gate_a = jnp.pad(jnp.concatenate([jnp.pad(dt_w, ((0, 0), (0, GATE_I0 - n_sh))), i_w], axis=1), tail)
    gate_b = jnp.pad(f_w, ((0, 0), (GATE_I0, LANES - GATE_I0 - n_mh)))
    w_cat = jnp.concatenate([z_w, xbc_w, u_w, v_w, o_w, gate_a, gate_b], axis=1).astype(BF16)
    widths = (d_ssd, cd, d_m, d_m, d_m, 2 * LANES)

    lane = jnp.arange(LANES)[:, None]
    ssd_expand = (lane == (jnp.arange(d_ssd)[None, :] // SSD_HEAD_DIM)).astype(BF16)
    mlstm_expand = (lane == (GATE_I0 + jnp.arange(d_m)[None, :] // dk)).astype(BF16)
    t = jnp.arange(CHUNK)
    tril = (t[:, None] >= t[None, :]).astype(BF16)
    p = {
        "w_cat": w_cat,
        "conv_ssd_w": conv_ssd_w[l], "conv_ssd_b": conv_ssd_b[l][None, :],
        "dt_bias_row": _pad_row(dt_bias[l], 0), "a_log_row": _pad_row(a_log[l], 0),
        "d_skip_row": jnp.repeat(d_skip[l], SSD_HEAD_DIM)[None, :], "ssd_norm_g": ssd_norm_g[l][None, :],
        "conv_m_w": conv_m_w[l], "conv_m_b": conv_m_b[l][None, :],
        "w_mq": w_mq[l].astype(BF16), "w_mk": w_mk[l].astype(BF16),
        "b_i_row": _pad_row(b_igate[l], GATE_I0), "b_f_row": _pad_row(b_fgate[l], GATE_I0),
        "mlstm_norm_g": mlstm_norm_g[l].reshape(1, d_m),
        "w_out": w_out[l].astype(BF16), "ln1_g": ln1_g[l][None, :], "ln1_b": ln1_b[l][None, :],
        "peer_wqT": peer_wq[l].T.astype(BF16), "peer_keys": peer_keys[l].astype(BF16),
        "peer_u": peer_u[l].astype(BF16),
        "peer_vT": peer_v[l].astype(BF16).T,
        "ln2_g_col": ln2_g[l][:, None], "ln2_b_col": ln2_b[l][:, None],
        "tril": tril, "ssd_expand": ssd_expand, "mlstm_expand": mlstm_expand,
    }
    return p, widths


def _tile_sizes(m):
    tm = 256 if m % 256 == 0 else LANES
    tb = 512 if m % 512 == 0 else LANES
    return tm, tb


def _peer_block(x1t, p, alpha):
    m = x1t.shape[1]
    tm, tb = _tile_sizes(m)
    s1, s2, st, alist, thr_rank = _peer_topk(x1t, p, tm)
    return _peer_dense(x1t, s1, s2, st, alist, thr_rank, p, alpha, tb)


def _layer_prompt(x, p, widths, alpha):
    bsz, seq, d = x.shape
    assert seq % CHUNK == 0 and seq >= CONV_K - 1
    x2 = x.reshape(bsz * seq, d)
    tm, _ = _tile_sizes(bsz * seq)
    z, xbc, u, v, o, gates = _inproj(x2, p["w_cat"], widths, tm)
    y_s, h_t = _ssd_chunk(xbc, z, gates, p, bsz, seq)
    h_m, c_new, n_new, m_new = _mlstm_chunk(u, v, o, gates, p, bsz, seq)
    x1t = _outproj(y_s, h_m, x2, p, alpha, 2 * tm if (bsz * seq) % (2 * tm) == 0 else tm)
    y = _peer_block(x1t, p, alpha).reshape(bsz, seq, d)
    n_mh = c_new.shape[1]
    d_ssd = z.shape[1]
    n_sh = d_ssd // SSD_HEAD_DIM
    hg = n_sh // SSD_GROUPS
    new_ssd = h_t.reshape(bsz, SSD_GROUPS, SSD_STATE, hg, SSD_HEAD_DIM).transpose(0, 1, 3, 4, 2)
    new_ssd = new_ssd.reshape(bsz, n_sh, SSD_HEAD_DIM, SSD_STATE)
    states = (new_ssd,
              xbc.reshape(bsz, seq, -1)[:, seq - (CONV_K - 1):, :],
              c_new,
              n_new[:, :, 0, :],
              m_new[:, 0, GATE_I0:GATE_I0 + n_mh],
              u.reshape(bsz, seq, -1)[:, seq - (CONV_K - 1):, :])
    return y, states


def _layer_sample(x, st, p, widths, alpha):
    st_ssd, st_ssd_conv, st_c, st_n, st_m, st_mconv = st
    nb, seq, d = x.shape
    assert seq == 1 and nb % LANES == 0
    x2 = x.reshape(nb, d)
    z, xbc, u, v, o, gates = _inproj(x2, p["w_cat"], widths, LANES)
    y_s, new_ssd, new_ssd_conv = _ssd_step(xbc, z, gates, st_ssd_conv, st_ssd, p)
    n_mh = st_c.shape[1]
    m_row = jnp.pad(st_m.astype(F32), ((0, 0), (GATE_I0, LANES - GATE_I0 - n_mh)))[:, None, :]
    h_m, c_new, n_new, m_new, new_mconv = _mlstm_step(u, v, o, gates, st_mconv, st_c, st_n, m_row, p)
    x1t = _outproj(y_s, h_m, x2, p, alpha, LANES)
    y = _peer_block(x1t, p, alpha).reshape(nb, seq, d)
    states = (new_ssd, new_ssd_conv, c_new, n_new, m_new[:, 0, GATE_I0:GATE_I0 + n_mh], new_mconv)
    return y, states


def kernel(x_prompt, x_sample, state_ssd, state_ssd_conv, state_mlstm_c, state_mlstm_n, state_mlstm_m,
           state_mlstm_conv, w_in, conv_ssd_w, conv_ssd_b, dt_bias, a_log, d_skip, ssd_norm_g, conv_m_w,
           conv_m_b, w_mq, w_mk, b_igate, b_fgate, mlstm_norm_g, w_out, ln1_g, ln1_b, peer_wq, peer_keys,
           peer_u, peer_v, ln2_g, ln2_b):
    depth = w_in.shape[0]
    alpha = (2.0 * depth) ** 0.25
    states = (state_ssd, state_ssd_conv, state_mlstm_c, state_mlstm_n, state_mlstm_m, state_mlstm_conv)
    yp, ys = x_prompt, x_sample
    p_list, s_list = [], []
    for l in range(depth):
        p, widths = _prep_layer(l, w_in, conv_ssd_w, conv_ssd_b, dt_bias, a_log, d_skip, ssd_norm_g, conv_m_w,
                                conv_m_b, w_mq, w_mk, b_igate, b_fgate, mlstm_norm_g, w_out, ln1_g, ln1_b,
                                peer_wq, peer_keys, peer_u, peer_v, ln2_g, ln2_b)
        yp, p_new = _layer_prompt(yp, p, widths, alpha)
        ys, s_new = _layer_sample(ys, tuple(s[l] for s in states), p, widths, alpha)
        p_list.append(p_new)
        s_list.append(s_new)
    p_out = tuple(jnp.stack([pn[i] for pn in p_list]) for i in range(6))
    s_out = tuple(jnp.stack([sn[i] for sn in s_list]) for i in range(6))
    return (yp, ys) + p_out + s_out
```
